```python
import jax, jax.numpy as jnp
from jax import lax
import numpy as np

D_MODEL = 1024
BATCH = 2
SEQ = 8192
DEPTH = 4
DEC_BATCH = 128
DEC_SEQ = 1
PAST_LEN = 2048
PAGE_SIZE = 128

HQ = 32
KVH = 4
DH = 64
GQ = HQ // KVH
NSA_W = HQ * DH
KV_W = KVH * DH
ROT_DIM = DH // 4
ROPE_THETA = 500000.0
CMP_LEN = 32
CMP_STRIDE = 16
CMP_HID = 2 * DH
SEL_BLK = 64
N_SEL = 16
WINDOW = 512
QBLK = 128
FORCE_BONUS = 1000.0
NEG = -1.0e30
NSA_IN = 2 * NSA_W + 6 * KV_W + 3 * HQ
RH = 4
RDK = D_MODEL // RH
RDV = 2 * RDK
RET_W = RH * RDV
RET_IN = 2 * RH * RDK + 2 * RET_W
RET_CHUNK = 128
XPOS_BASE = 10000.0
N_MIXERS = 2
N_NSA = (DEPTH + 1) // 2
N_RET = DEPTH // 2
DEEPNORM_ALPHA = (2.0 * DEPTH) ** 0.25
DEEPNORM_BETA = (8.0 * DEPTH) ** -0.25
LN_EPS = 1e-5
F32 = jnp.float32

kernel_name = 'nsa_retention_hybrid_step'


def layer_norm(h, g, b):
    h32 = h.astype(F32)
    d = h32 - jnp.mean(h32, -1, keepdims=True)
    var = jnp.mean(d * d, -1, keepdims=True)
    return (d * lax.rsqrt(var + LN_EPS) * g.astype(F32) + b.astype(F32)).astype(h.dtype)


def masked_softmax(s, mask):
    p = jax.nn.softmax(jnp.where(mask, s.astype(F32), NEG), axis=-1)
    return jnp.where(mask, p, 0.0)


def rope_partial(x, pos):
    half = ROT_DIM // 2
    inv = ROPE_THETA ** (-jnp.arange(half, dtype=F32) / half)
    ang = pos.astype(F32)[:, None] * inv[None, :]
    cos = jnp.cos(ang)[None, :, None, :]
    sin = jnp.sin(ang)[None, :, None, :]
    x1 = x[..., :half].astype(F32)
    x2 = x[..., half:ROT_DIM].astype(F32)
    rot = jnp.concatenate([x1 * cos - x2 * sin, x1 * sin + x2 * cos], -1).astype(x.dtype)
    return jnp.concatenate([rot, x[..., ROT_DIM:]], -1)


def xpos_rotate(x, pos):
    half = RDK // 2
    inv = 1.0 / (XPOS_BASE ** jnp.linspace(0.0, 1.0, half, dtype=F32))
    ang = pos.astype(F32)[:, None] * inv[None, :]
    cos = jnp.cos(ang)[None, :, None, :]
    sin = jnp.sin(ang)[None, :, None, :]
    xe = x[..., 0::2].astype(F32)
    xo = x[..., 1::2].astype(F32)
    out = jnp.stack([xe * cos - xo * sin, xe * sin + xo * cos], -1).reshape(x.shape)
    return out.astype(x.dtype)


def nsa_project(x, pos, w_in):
    B, T, _ = x.shape
    proj = x @ w_in
    cuts = np.cumsum([NSA_W] + [KV_W] * 6 + [3 * HQ]).tolist()
    q, kc, vc, ks, vs, kw, vw, gl, z = jnp.split(proj, cuts, axis=-1)
    heads = lambda a: a.reshape(B, T, KVH, DH)
    q = rope_partial(q.reshape(B, T, HQ, DH), pos)
    kc = rope_partial(heads(kc), pos)
    ks = rope_partial(heads(ks), pos)
    kw = rope_partial(heads(kw), pos)
    g = jax.nn.sigmoid(gl.astype(F32)).astype(x.dtype).reshape(B, T, 3, HQ)
    return q, kc, heads(vc), ks, heads(vs), kw, heads(vw), g, z


def compress_tokens(k, pe, w1, w2):
    B, L = k.shape[:2]
    nc = (L - CMP_LEN) // CMP_STRIDE + 1
    idx = jnp.arange(nc)[:, None] * CMP_STRIDE + jnp.arange(CMP_LEN)[None, :]
    blk = k[:, idx] + pe[:, None, :]
    flat = blk.transpose(0, 1, 3, 2, 4).reshape(B, nc, KVH, CMP_LEN * DH)
    return jax.nn.gelu(flat @ w1) @ w2


def to_sel_blocks(k):
    B, L = k.shape[:2]
    ns = -(-L // SEL_BLK)
    k = jnp.pad(k, ((0, 0), (0, ns * SEL_BLK - L), (0, 0), (0, 0)))
    return k.reshape(B, ns, SEL_BLK, KVH, DH).transpose(0, 3, 1, 2, 4)


def nsa_attend(q, qpos, g, kc, vc, ks_blk, vs_blk, kw, vw, kw_pos):
    B, T = q.shape[:2]
    qg = q.reshape(B, T, KVH, GQ, DH).transpose(0, 2, 3, 1, 4) * (DH ** -0.5)
    nc = kc.shape[1]
    c_start = jnp.arange(nc) * CMP_STRIDE
    m_c = (c_start + CMP_LEN - 1)[None, :] <= qpos[:, None]
    p_c = masked_softmax(jnp.einsum('bkgtd,bckd->bkgtc', qg, kc), m_c)
    o_c = jnp.einsum('bkgtc,bckd->bkgtd', p_c.astype(vc.dtype), vc)
    ns = ks_blk.shape[2]
    s_start = jnp.arange(ns) * SEL_BLK
    ov = jnp.clip(jnp.minimum(c_start[:, None] + CMP_LEN, s_start[None, :] + SEL_BLK)
                  - jnp.maximum(c_start[:, None], s_start[None, :]), 0, None).astype(F32) / CMP_LEN
    imp = jnp.einsum('bktc,cs->bkts', p_c.sum(2), ov)
    blk_q = qpos // SEL_BLK
    j = jnp.arange(ns)
    valid = s_start[None, :] <= qpos[:, None]
    forced = (j[None, :] == 0) | (j[None, :] == blk_q[:, None]) | (j[None, :] == blk_q[:, None] - 1)
    score = jnp.where(valid, imp + FORCE_BONUS * forced.astype(F32), NEG)
    _, sel = lax.top_k(score, min(N_SEL, ns))
    n = sel.shape[-1]
    bi = jnp.arange(B)[:, None, None, None]
    hi = jnp.arange(KVH)[None, :, None, None]
    ksel = ks_blk[bi, hi, sel].reshape(B, KVH, T, n * SEL_BLK, DH)
    vsel = vs_blk[bi, hi, sel].reshape(B, KVH, T, n * SEL_BLK, DH)
    spos = (sel[..., None] * SEL_BLK + jnp.arange(SEL_BLK)).reshape(B, KVH, T, n * SEL_BLK)
    m_s = (spos <= qpos[None, None, :, None])[:, :, None]
    p_s = masked_softmax(jnp.einsum('bkgtd,bktsd->bkgts', qg, ksel), m_s)
    o_s = jnp.einsum('bkgts,bktsd->bkgtd', p_s.astype(vsel.dtype), vsel)
    dist = qpos[:, None] - kw_pos[None, :]
    m_w = (kw_pos[None, :] >= 0) & (dist >= 0) & (dist <= WINDOW)
    p_w = masked_softmax(jnp.einsum('bkgtd,bskd->bkgts', qg, kw), m_w)
    o_w = jnp.einsum('bkgts,bskd->bkgtd', p_w.astype(vw.dtype), vw)
    gg = g.reshape(B, T, 3, KVH, GQ).transpose(2, 0, 3, 4, 1)[..., None]
    o = gg[0] * o_c + gg[1] * o_s + gg[2] * o_w
    return o.transpose(0, 3, 1, 2, 4).reshape(B, T, NSA_W)


def nsa_prompt(x, w_in, w_out, pe_k, w1_k, w2_k, pe_v, w1_v, w2_v):
    B, T, _ = x.shape
    pos = jnp.arange(T, dtype=jnp.int32)
    q, kcr, vcr, ks, vs, kw, vw, g, z = nsa_project(x, pos, w_in)
    kc = compress_tokens(kcr, pe_k, w1_k, w2_k)
    vc = compress_tokens(vcr, pe_v, w1_v, w2_v)
    ks_blk, vs_blk = to_sel_blocks(ks), to_sel_blocks(vs)
    wpad = ((0, 0), (WINDOW, 0), (0, 0), (0, 0))
    kw_pad, vw_pad = jnp.pad(kw, wpad), jnp.pad(vw, wpad)

    def one_block(b):
        st = b * QBLK
        qpos = st + jnp.arange(QBLK, dtype=jnp.int32)
        kpos = st - WINDOW + jnp.arange(WINDOW + QBLK, dtype=jnp.int32)
        sl = lambda a, n: lax.dynamic_slice_in_dim(a, st, n, axis=1)
        return nsa_attend(sl(q, QBLK), qpos, sl(g, QBLK), kc, vc, ks_blk, vs_blk,
                          sl(kw_pad, WINDOW + QBLK), sl(vw_pad, WINDOW + QBLK), kpos)

    o = lax.map(one_block, jnp.arange(T // QBLK, dtype=jnp.int32))
    o = o.transpose(1, 0, 2, 3).reshape(B, T, NSA_W)
    y = (o * jax.nn.silu(z)) @ w_out
    wb = min(WINDOW, T)
    return y, (kcr, vcr, ks, vs, kw[:, T - wb:], vw[:, T - wb:])


def nsa_sample(x, page_table, ck_c, cv_c, ck_s, cv_s, ck_w, cv_w,
               w_in, w_out, pe_k, w1_k, w2_k, pe_v, w1_v, w2_v):
    B, T, _ = x.shape
    past = page_table.shape[1] * PAGE_SIZE
    pos = past + jnp.arange(T, dtype=jnp.int32)
    q, kcr, vcr, ks, vs, kw, vw, g, z = nsa_project(x, pos, w_in)
    gather = lambda pool: pool[page_table].reshape(B, past, KVH, DH)
    kc = compress_tokens(jnp.concatenate([gather(ck_c), kcr], 1), pe_k, w1_k, w2_k)
    vc = compress_tokens(jnp.concatenate([gather(cv_c), vcr], 1), pe_v, w1_v, w2_v)
    ks_blk = to_sel_blocks(jnp.concatenate([gather(ck_s), ks], 1))
    vs_blk = to_sel_blocks(jnp.concatenate([gather(cv_s), vs], 1))
    wb = ck_w.shape[1]
    kw_all = jnp.concatenate([ck_w, kw], 1)
    vw_all = jnp.concatenate([cv_w, vw], 1)
    kpos = past - wb + jnp.arange(wb + T, dtype=jnp.int32)
    o = nsa_attend(q, pos, g, kc, vc, ks_blk, vs_blk, kw_all, vw_all, kpos)
    y = (o * jax.nn.silu(z)) @ w_out
    return y, (kcr, vcr, ks, vs, kw_all[:, T:], vw_all[:, T:])


def retention_chunkwise(q, k, v, s0):
    B, T, H, _ = q.shape
    c = RET_CHUNK if T % RET_CHUNK == 0 else T
    nch = T // c
    log_g = jnp.log1p(-jnp.power(2.0, -5.0 - jnp.arange(H, dtype=F32)))
    i = jnp.arange(c, dtype=F32)
    diff = i[:, None] - i[None, :]
    dmask = jnp.where(diff >= 0, jnp.exp(jnp.maximum(diff, 0.0)[None] * log_g[:, None, None]), 0.0)
    q_dec = jnp.exp((i + 1.0)[None, :] * log_g[:, None])[..., None]
    k_dec = jnp.exp((c - 1.0 - i)[None, :] * log_g[:, None])[..., None]
    c_dec = jnp.exp(c * log_g)[:, None, None]
    chunks = lambda a: a.astype(F32).reshape(B, nch, c, H, a.shape[-1]).transpose(1, 0, 3, 2, 4)

    def step(s, xs):
        qc, kc, vc = xs
        inner = jnp.einsum('bhid,bhjd->bhij', qc, kc) * dmask
        o = jnp.einsum('bhij,bhje->bhie', inner, vc) + jnp.einsum('bhid,bhde->bhie', qc * q_dec, s)
        s = s * c_dec + jnp.einsum('bhjd,bhje->bhde', kc * k_dec, vc)
        return s, o

    s, o = lax.scan(step, s0.astype(F32), (chunks(q), chunks(k), chunks(v)))
    return o.transpose(1, 0, 3, 2, 4).reshape(B, T, H, RDV), s


def retention_mixer(x, pos, s0, w_in, gn_g, w_out):
    B, T, _ = x.shape
    proj = x @ w_in
    q, k, v, z = jnp.split(proj, [RH * RDK, 2 * RH * RDK, 2 * RH * RDK + RET_W], axis=-1)
    q = xpos_rotate(q.reshape(B, T, RH, RDK), pos)
    k = xpos_rotate(k.reshape(B, T, RH, RDK), pos) * (RDK ** -0.5)
    o, s = retention_chunkwise(q, k, v.reshape(B, T, RH, RDV), s0)
    d = o - jnp.mean(o, -1, keepdims=True)
    o = d * lax.rsqrt(jnp.mean(d * d, -1, keepdims=True) + LN_EPS)
    o = (o.reshape(B, T, RET_W) * gn_g.astype(F32)).astype(x.dtype)
    y = (jax.nn.silu(z) * o) @ w_out
    return y, s


def setup_inputs(seed: int = 0) -> dict:
    key = jax.random.key(seed)
    ks = jax.random.split(key, 24)
    n_pages = PAST_LEN // PAGE_SIZE
    n_used = DEC_BATCH * n_pages
    n_pool = n_used + max(1, n_used // 4)
    wb = min(WINDOW, PAST_LEN)
    nrm = lambda k, shape, s: jax.random.normal(k, shape, F32) * s
    pool_shape = (N_NSA, n_pool, PAGE_SIZE, KVH, DH)
    win_shape = (N_NSA, DEC_BATCH, wb, KVH, DH)
    page_table = jax.random.permutation(ks[9], n_pool)[:n_used].reshape(DEC_BATCH, n_pages).astype(jnp.int32)
    return {
        'x_prompt': nrm(ks[0], (BATCH, SEQ, D_MODEL), 1.0),
        'x_sample': nrm(ks[1], (DEC_BATCH, DEC_SEQ, D_MODEL), 1.0),
        'cache_k_cmp': nrm(ks[2], pool_shape, 1.0),
        'cache_v_cmp': nrm(ks[3], pool_shape, 1.0),
        'cache_k_sel': nrm(ks[4], pool_shape, 1.0),
        'cache_v_sel': nrm(ks[5], pool_shape, 1.0),
        'cache_k_win': nrm(ks[6], win_shape, 1.0),
        'cache_v_win': nrm(ks[7], win_shape, 1.0),
        'state_ret': nrm(ks[8], (N_RET, DEC_BATCH, RH, RDK, RDV), 0.5),
        'page_table': page_table,
        'nsa_w_in': nrm(ks[10], (N_NSA, D_MODEL, NSA_IN), D_MODEL ** -0.5),
        'nsa_w_out': nrm(ks[11], (N_NSA, NSA_W, D_MODEL), DEEPNORM_BETA * NSA_W ** -0.5),
        'nsa_pe_k': nrm(ks[12], (N_NSA, CMP_LEN, DH), 0.02),
        'nsa_w1_k': nrm(ks[13], (N_NSA, CMP_LEN * DH, CMP_HID), (CMP_LEN * DH) ** -0.5),
        'nsa_w2_k': nrm(ks[14], (N_NSA, CMP_HID, DH), CMP_HID ** -0.5),
        'nsa_pe_v': nrm(ks[15], (N_NSA, CMP_LEN, DH), 0.02),
        'nsa_w1_v': nrm(ks[16], (N_NSA, CMP_LEN * DH, CMP_HID), (CMP_LEN * DH) ** -0.5),
        'nsa_w2_v': nrm(ks[17], (N_NSA, CMP_HID, DH), CMP_HID ** -0.5),
        'ret_w_in': nrm(ks[18], (N_RET, D_MODEL, RET_IN), D_MODEL ** -0.5),
        'ret_gn_g': 1.0 + nrm(ks[19], (N_RET, RET_W), 0.02),
        'ret_w_out': nrm(ks[20], (N_RET, RET_W, D_MODEL), DEEPNORM_BETA * RET_W ** -0.5),
        'ln_g': 1.0 + nrm(ks[21], (DEPTH, D_MODEL), 0.02),
        'ln_b': nrm(ks[22], (DEPTH, D_MODEL), 0.02),
    }


def reference(x_prompt, x_sample, cache_k_cmp, cache_v_cmp, cache_k_sel, cache_v_sel,
              cache_k_win, cache_v_win, state_ret, page_table,
              nsa_w_in, nsa_w_out, nsa_pe_k, nsa_w1_k, nsa_w2_k, nsa_pe_v, nsa_w1_v, nsa_w2_v,
              ret_w_in, ret_gn_g, ret_w_out, ln_g, ln_b):
    xp, xs = x_prompt, x_sample
    bp, tp = xp.shape[:2]
    ts = xs.shape[1]
    past = page_table.shape[1] * PAGE_SIZE
    nsa_p, nsa_s, ret_p, ret_s = [], [], [], []
    for i in range(DEPTH):
        li = i // N_MIXERS
        if i % N_MIXERS == 0:
            w = (nsa_w_in[li], nsa_w_out[li], nsa_pe_k[li], nsa_w1_k[li], nsa_w2_k[li],
                 nsa_pe_v[li], nsa_w1_v[li], nsa_w2_v[li])
            yp, st_p = nsa_prompt(xp, *w)
            ys, st_s = nsa_sample(xs, page_table, cache_k_cmp[li], cache_v_cmp[li], cache_k_sel[li],
                                  cache_v_sel[li], cache_k_win[li], cache_v_win[li], *w)
            nsa_p.append(st_p)
            nsa_s.append(st_s)
        else:
            s0 = jnp.zeros((bp, RH, RDK, RDV), F32)
            yp, sp = retention_mixer(xp, jnp.arange(tp, dtype=jnp.int32), s0,
                                     ret_w_in[li], ret_gn_g[li], ret_w_out[li])
            ys, ss = retention_mixer(xs, past + jnp.arange(ts, dtype=jnp.int32), state_ret[li],
                                     ret_w_in[li], ret_gn_g[li], ret_w_out[li])
            ret_p.append(sp)
            ret_s.append(ss)
        xp = layer_norm(DEEPNORM_ALPHA * xp + yp, ln_g[i], ln_b[i])
        xs = layer_norm(DEEPNORM_ALPHA * xs + ys, ln_g[i], ln_b[i])
    stk = lambda lst, j: jnp.stack([e[j] for e in lst])
    return (xp, xs,
            stk(nsa_p, 0), stk(nsa_s, 0), stk(nsa_p, 1), stk(nsa_s, 1),
            stk(nsa_p, 2), stk(nsa_s, 2), stk(nsa_p, 3), stk(nsa_s, 3),
            stk(nsa_p, 4), stk(nsa_s, 4), stk(nsa_p, 5), stk(nsa_s, 5),
            jnp.stack(ret_p), jnp.stack(ret_s))
```

```python
import functools

import jax
import jax.numpy as jnp
from jax import lax
from jax.experimental import pallas as pl
from jax.experimental.pallas import tpu as pltpu

F32 = jnp.float32
MXU_DTYPE = jnp.bfloat16

HQ, KVH, DH = 32, 4, 64
GQ = HQ // KVH
NSA_W, KV_W = HQ * DH, KVH * DH
ROT_DIM, ROPE_THETA = DH // 4, 500000.0
CMP_LEN, CMP_STRIDE, CMP_HID = 32, 16, 2 * DH
SEL_BLK, N_SEL, WINDOW, QBLK = 64, 16, 512, 128
FORCE_BONUS, NEG = 1000.0, -1.0e30
RH, RET_CHUNK, XPOS_BASE = 4, 128, 10000.0
PAGE_SIZE, LN_EPS = 128, 1e-5

LANES = 128
VMEM_LIMIT_BYTES = 56 * 1024 * 1024
CMP_GROUP = 2048
GATE_PAD = LANES
REMOVED = -3.0e38


def _params(n_axes):
    return pltpu.CompilerParams(dimension_semantics=("arbitrary",) * n_axes,
                                vmem_limit_bytes=VMEM_LIMIT_BYTES)


def _dot(a, b):
    return jnp.dot(a, b, preferred_element_type=F32)


def _dot_nt(a, b):
    return lax.dot_general(a, b, (((1,), (1,)), ((), ())), preferred_element_type=F32)


def _dot_tn(a, b):
    return lax.dot_general(a, b, (((0,), (0,)), ((), ())), preferred_element_type=F32)


def _lane_tile(t, reps):
    return t if reps == 1 else jnp.concatenate([t] * reps, axis=1)


def _rotate(a, c, sa, sb, shift):
    n = a.shape[1]
    reps = n // c.shape[1]
    c, sa, sb = _lane_tile(c, reps), _lane_tile(sa, reps), _lane_tile(sb, reps)
    return a * c + pltpu.roll(a, shift, 1) * sa + pltpu.roll(a, n - shift, 1) * sb


def _masked_softmax(s, mask):
    s = jnp.where(mask, s, NEG)
    m = jnp.max(s, axis=-1, keepdims=True)
    e = jnp.where(mask, jnp.exp(s - m), 0.0)
    d = jnp.sum(e, axis=-1, keepdims=True)
    return e * (1.0 / jnp.where(d > 0.0, d, 1.0))


def _split3(x):
    hi = x.astype(MXU_DTYPE)
    r = x - hi.astype(F32)
    mid = r.astype(MXU_DTYPE)
    lo = (r - mid.astype(F32)).astype(MXU_DTYPE)
    return hi, mid, lo


def _topk_mask(score, idx, n_pick, axis):
    big = jnp.float32(score.shape[axis])
    sel = jnp.zeros(score.shape, F32)
    for _ in range(n_pick):
        m = jnp.max(score, axis=axis, keepdims=True)
        first = jnp.min(jnp.where(score == m, idx, big), axis=axis, keepdims=True)
        pick = idx == first
        sel = jnp.where(pick, 1.0, sel)
        score = jnp.where(pick, REMOVED, score)
    return sel


def _overlap(c_idx, s_idx):
    cs, ss = c_idx * CMP_STRIDE, s_idx * SEL_BLK
    ov = jnp.minimum(cs + CMP_LEN, ss + SEL_BLK) - jnp.maximum(cs, ss)
    return jnp.maximum(ov, 0).astype(F32) * (1.0 / CMP_LEN)


def _nsa_inproj_kernel(x_ref, w_ref, c_ref, sa_ref, sb_ref, q_ref, kv_ref, khm_ref, g_ref, z_ref):
    xb = x_ref[...].astype(MXU_DTYPE)
    c, sa, sb = c_ref[...], sa_ref[...], sb_ref[...]
    half = ROT_DIM // 2
    col = 0
    for ch in range(NSA_W // 512):
        a = _dot(xb, w_ref[:, col:col + 512])
        a = _rotate(a, c, sa, sb, half) * (DH ** -0.5)
        for h in range(512 // DH):
            q_ref[0, ch * (512 // DH) + h] = a[:, h * DH:(h + 1) * DH].astype(q_ref.dtype)
        col += 512
    for j in range(6):
        a = _dot(xb, w_ref[:, col:col + KV_W])
        if j % 2 == 0:
            a = _rotate(a, c, sa, sb, half)
        kv_ref[j] = a
        if j >= 2:
            for h in range(KVH):
                khm_ref[j - 2, 0, h] = a[:, h * DH:(h + 1) * DH].astype(khm_ref.dtype)
        col += KV_W
    a = _dot(xb, w_ref[:, col:col + GATE_PAD])
    g_ref[...] = 1.0 / (1.0 + jnp.exp(-a))
    col += GATE_PAD
    for ch in range(NSA_W // 512):
        z_ref[:, ch * 512:(ch + 1) * 512] = _dot(xb, w_ref[:, col:col + 512])
        col += 512


def _nsa_inproj(x2d, w, tabs, nb, t, tm):
    m, d = x2d.shape
    nt = t // tm
    n_cols = w.shape[1]
    row = lambda i: (i, 0)
    tab = pl.BlockSpec((tm, LANES), lambda i: (i % nt, 0))
    return pl.pallas_call(
        _nsa_inproj_kernel,
        grid=(m // tm,),
        in_specs=[pl.BlockSpec((tm, d), row), pl.BlockSpec((d, n_cols), lambda i: (0, 0)), tab, tab, tab],
        out_specs=[
            pl.BlockSpec((1, HQ, tm, DH), lambda i: (i // nt, 0, i % nt, 0)),
            pl.BlockSpec((6, tm, KV_W), lambda i: (0, i, 0)),
            pl.BlockSpec((4, 1, KVH, tm, DH), lambda i: (0, i // nt, 0, i % nt, 0)),
            pl.BlockSpec((tm, GATE_PAD), row),
            pl.BlockSpec((tm, NSA_W), row),
        ],
        out_shape=[
            jax.ShapeDtypeStruct((nb, HQ, t, DH), MXU_DTYPE),
            jax.ShapeDtypeStruct((6, m, KV_W), F32),
            jax.ShapeDtypeStruct((4, nb, KVH, t, DH), MXU_DTYPE),
            jax.ShapeDtypeStruct((m, GATE_PAD), F32),
            jax.ShapeDtypeStruct((m, NSA_W), F32),
        ],
        compiler_params=_params(1),
        name="nsa_inproj",
    )(x2d, w, *tabs)


def _compress_kernel(*refs, n_prefetch, n_tok_refs, has_halo, head_major, n_chunks):
    tok_refs = refs[n_prefetch:n_prefetch + n_tok_refs]
    pos = n_prefetch + n_tok_refs
    halo_ref = refs[pos] if has_halo else None
    pos += int(has_halo)
    wbig_ref, pe_ref, w1_ref, w2_ref, out_ref, r_ref = refs[pos:pos + 6]
    per_ref = n_chunks // n_tok_refs
    r_ref[n_chunks:, :] = jnp.zeros((8, CMP_STRIDE * KV_W), F32)
    split = KV_W // LANES
    for j in range(CMP_STRIDE * split):
        lanes = slice(j * LANES, (j + 1) * LANES)
        for p, ref in enumerate(tok_refs):
            r_ref[p * per_ref:(p + 1) * per_ref, lanes] = ref[0, pl.ds(j, per_ref, stride=CMP_STRIDE * split), :]
        if has_halo:
            r_ref[n_chunks:n_chunks + 1, lanes] = halo_ref[0, j:j + 1, :]
    ab = _dot(r_ref[...].astype(MXU_DTYPE), wbig_ref[...])
    pe8 = jnp.broadcast_to(pe_ref[...], (8, CMP_LEN * DH)).astype(MXU_DTYPE)
    bias = _dot(pe8, w1_ref[...])[0:1, :]
    w2 = w2_ref[...]
    for h in range(KVH):
        base = h * 2 * CMP_HID
        first_half = ab[0:n_chunks, base:base + CMP_HID]
        second_half = ab[1:n_chunks + 1, base + CMP_HID:base + 2 * CMP_HID]
        u = first_half + second_half + bias
        hid = 0.5 * u * (1.0 + jnp.tanh(0.7978845608028654 * (u + 0.044715 * (u * u * u))))
        tok = _dot(hid.astype(MXU_DTYPE), w2).astype(out_ref.dtype)
        if head_major:
            out_ref[0, h] = tok
        else:
            out_ref[0, :, h * DH:(h + 1) * DH] = tok


def _compress_weights(pe, w1, w2):
    w1r = w1.reshape(2, CMP_STRIDE, DH, CMP_HID)
    eye = jnp.eye(KVH, dtype=w1.dtype)
    wbig = jnp.einsum("sldn,hg->lhdgsn", w1r, eye).reshape(CMP_STRIDE * KV_W, KVH * 2 * CMP_HID)
    return (wbig.astype(MXU_DTYPE), pe.reshape(1, CMP_LEN * DH), w1.astype(MXU_DTYPE), w2.astype(MXU_DTYPE))


def _compress_prompt(k_tok, weights):
    nb, t, _ = k_tok.shape
    gt = min(CMP_GROUP, t)
    n_chunks = gt // CMP_STRIDE
    ng = t // gt
    last_halo = t // CMP_STRIDE - 1
    wbig, pe, w1, w2 = weights
    split = KV_W // LANES
    k_tok = k_tok.reshape(nb, t * split, LANES)
    full = lambda a: pl.BlockSpec(a.shape, lambda b, g: (0,) * a.ndim)
    kern = functools.partial(_compress_kernel, n_prefetch=0, n_tok_refs=1, has_halo=True, head_major=True,
                             n_chunks=n_chunks)
    return pl.pallas_call(
        kern,
        grid=(nb, ng),
        in_specs=[pl.BlockSpec((1, gt * split, LANES), lambda b, g: (b, g, 0)),
                  pl.BlockSpec((1, CMP_STRIDE * split, LANES),
                               lambda b, g: (b, jnp.minimum((g + 1) * n_chunks, last_halo), 0)),
                  full(wbig), full(pe), full(w1), full(w2)],
        out_specs=pl.BlockSpec((1, KVH, n_chunks, DH), lambda b, g: (b, 0, g, 0)),
        out_shape=jax.ShapeDtypeStruct((nb, KVH, t // CMP_STRIDE, DH), MXU_DTYPE),
        scratch_shapes=[pltpu.VMEM((n_chunks + 8, CMP_STRIDE * KV_W), F32)],
        compiler_params=_params(2),
        name="nsa_compress_prompt",
    )(k_tok, k_tok, wbig, pe, w1, w2)


def _compress_paged(pool, page_table, weights):
    nb, n_pages = page_table.shape
    n_chunks = n_pages * PAGE_SIZE // CMP_STRIDE
    wbig, pe, w1, w2 = weights
    split = KV_W // LANES
    pool = pool.reshape(pool.shape[0], PAGE_SIZE * split, LANES)
    full = lambda a: pl.BlockSpec(a.shape, lambda b, pt: (0,) * a.ndim)
    page = lambda p: pl.BlockSpec((1, PAGE_SIZE * split, LANES), lambda b, pt: (pt[b, p], 0, 0))
    kern = functools.partial(_compress_kernel, n_prefetch=1, n_tok_refs=n_pages, has_halo=False,
                             head_major=False, n_chunks=n_chunks)
    return pl.pallas_call(
        kern,
        grid_spec=pltpu.PrefetchScalarGridSpec(
            num_scalar_prefetch=1,
            grid=(nb,),
            in_specs=[page(p) for p in range(n_pages)] + [full(wbig), full(pe), full(w1), full(w2)],
            out_specs=pl.BlockSpec((1, n_chunks, KV_W), lambda b, pt: (b, 0, 0)),
            scratch_shapes=[pltpu.VMEM((n_chunks + 8, CMP_STRIDE * KV_W), F32)],
        ),
        out_shape=jax.ShapeDtypeStruct((nb, n_chunks, KV_W), MXU_DTYPE),
        compiler_params=_params(1),
        name="nsa_compress_paged",
    )(page_table, *([pool] * n_pages), wbig, pe, w1, w2)


def _nsa_attn_prompt_kernel(q_ref, kc_ref, vc_ref, ks_ref, vs_ref, kw_ref, vw_ref, g_ref, o_ref,
                            m_sc, l_sc, acc_sc, *, seq_len, tk):
    st = pl.program_id(2) * QBLK
    rows = GQ * QBLK
    ncp = kc_ref.shape[2]
    ns = seq_len // SEL_BLK
    q = q_ref[0].reshape(rows, DH)
    qpos = st + (lax.broadcasted_iota(jnp.int32, (rows, 1), 0) & (QBLK - 1))

    c_idx = lax.broadcasted_iota(jnp.int32, (1, ncp), 1)
    p_c = _masked_softmax(_dot_nt(q, kc_ref[0, 0]), c_idx * CMP_STRIDE + (CMP_LEN - 1) <= qpos)
    o_c = _dot(p_c.astype(MXU_DTYPE), vc_ref[0, 0])

    p_sum = jnp.sum(p_c.reshape(GQ, QBLK, ncp), axis=0)
    s_idx = lax.broadcasted_iota(jnp.int32, (ns, 1), 0)
    ov_t = _overlap(c_idx, s_idx).astype(MXU_DTYPE)
    imp_t = sum(_dot_nt(ov_t, part) for part in _split3(p_sum))
    qpos_t = st + lax.broadcasted_iota(jnp.int32, (1, QBLK), 1)
    blk_q = qpos_t // SEL_BLK
    forced = (s_idx == 0) | (s_idx == blk_q) | (s_idx == blk_q - 1)
    valid = s_idx * SEL_BLK <= qpos_t
    score = jnp.where(valid, imp_t + FORCE_BONUS * forced.astype(F32), NEG)
    sel_t = _topk_mask(score, s_idx.astype(F32), min(N_SEL, ns), 0)
    sel = jnp.where(valid, sel_t, 0.0).T.astype(MXU_DTYPE)

    m_sc[...] = jnp.full(m_sc.shape, NEG, F32)
    l_sc[...] = jnp.zeros(l_sc.shape, F32)
    acc_sc[...] = jnp.zeros(acc_sc.shape, F32)
    q_col = st + lax.broadcasted_iota(jnp.int32, (QBLK, 1), 0)

    def sweep(kt, carry):
        k0 = pl.multiple_of(kt * tk, tk)
        s = _dot_nt(q, ks_ref[0, 0, 0, pl.ds(k0, tk), :]).reshape(GQ, QBLK, tk)
        kpos = k0 + lax.broadcasted_iota(jnp.int32, (1, tk), 1)
        expand = (s_idx == kpos // SEL_BLK).astype(MXU_DTYPE)
        ok = ((_dot(sel, expand) > 0.5) & (kpos <= q_col))[None]
        s = jnp.where(ok, s, NEG)
        m_old = m_sc[...]
        m_new = jnp.maximum(m_old, jnp.max(s, axis=-1, keepdims=True))
        alpha = jnp.exp(m_old - m_new)
        p = jnp.where(ok, jnp.exp(s - m_new), 0.0)
        l_sc[...] = alpha * l_sc[...] + jnp.sum(p, axis=-1, keepdims=True)
        pv = _dot(p.reshape(rows, tk).astype(MXU_DTYPE), vs_ref[0, 0, 0, pl.ds(k0, tk), :])
        acc_sc[...] = alpha * acc_sc[...] + pv.reshape(GQ, QBLK, DH)
        m_sc[...] = m_new
        return carry

    lax.fori_loop(0, (st + QBLK + tk - 1) // tk, sweep, 0)
    o_s = acc_sc[...] * (1.0 / l_sc[...])

    span = WINDOW + QBLK
    ws = pl.multiple_of(jnp.maximum(st - WINDOW, 0), QBLK)
    dist = qpos - (ws + lax.broadcasted_iota(jnp.int32, (1, span), 1))
    p_w = _masked_softmax(_dot_nt(q, kw_ref[0, 0, 0, pl.ds(ws, span), :]), (dist >= 0) & (dist <= WINDOW))
    o_w = _dot(p_w.astype(MXU_DTYPE), vw_ref[0, 0, 0, pl.ds(ws, span), :])

    g = g_ref[0, 0]
    for h in range(GQ):
        r = slice(h * QBLK, (h + 1) * QBLK)
        o_ref[0, :, h * DH:(h + 1) * DH] = (g[:, h:h + 1] * o_c[r] + g[:, GQ + h:GQ + h + 1] * o_s[h]
                                            + g[:, 2 * GQ + h:2 * GQ + h + 1] * o_w[r])


def _nsa_attn_prompt(q_hm, kc_hm, vc_hm, khm, gates):
    nb, _, t, _ = q_hm.shape
    ncp = kc_hm.shape[2]
    tk = min(512, t)
    cmp_spec = pl.BlockSpec((1, 1, ncp, DH), lambda b, h, i: (b, h, 0, 0))
    seq_spec = lambda j: pl.BlockSpec((1, 1, 1, t, DH), lambda b, h, i: (j, b, h, 0, 0))
    kern = functools.partial(_nsa_attn_prompt_kernel, seq_len=t, tk=tk)
    return pl.pallas_call(
        kern,
        grid=(nb, KVH, t // QBLK),
        in_specs=[pl.BlockSpec((1, GQ, QBLK, DH), lambda b, h, i: (b, h, i, 0)),
                  cmp_spec, cmp_spec, seq_spec(0), seq_spec(1), seq_spec(2), seq_spec(3),
                  pl.BlockSpec((1, 1, QBLK, 3 * GQ), lambda b, h, i: (b, h, i, 0))],
        out_specs=pl.BlockSpec((1, QBLK, GQ * DH), lambda b, h, i: (b, i, h)),
        out_shape=jax.ShapeDtypeStruct((nb, t, NSA_W), F32),
        scratch_shapes=[pltpu.VMEM((GQ, QBLK, 1), F32), pltpu.VMEM((GQ, QBLK, 1), F32),
                        pltpu.VMEM((GQ, QBLK, DH), F32)],
        compiler_params=_params(3),
        name="nsa_attn_prompt",
    )(q_hm, kc_hm, vc_hm, khm, khm, khm, khm, gates)


def _block_diag_fold(x):
    r_kv = lax.broadcasted_iota(jnp.int32, (HQ, KV_W), 0) // GQ
    l_kv = lax.broadcasted_iota(jnp.int32, (HQ, KV_W), 1) // DH
    x = jnp.where(r_kv == l_kv, x, 0.0)
    return sum(x[..., h * DH:(h + 1) * DH] for h in range(KVH))


def _nsa_select_kernel(q_ref, kc_ref, vc_ref, oc_ref, sel_ref, *, past):
    bb, ncp = kc_ref.shape[0], kc_ref.shape[1]
    nsp = sel_ref.shape[1]
    q = q_ref[...]
    s = jnp.einsum("bqd,bkd->bqk", q, kc_ref[...], preferred_element_type=F32)
    c_idx = lax.broadcasted_iota(jnp.int32, (1, 1, ncp), 2)
    p_c = _masked_softmax(s, c_idx * CMP_STRIDE + (CMP_LEN - 1) <= past)
    o_c = jnp.einsum("bqk,bkd->bqd", p_c.astype(MXU_DTYPE), vc_ref[...], preferred_element_type=F32)
    oc_ref[...] = _block_diag_fold(o_c)
    ov = _overlap(lax.broadcasted_iota(jnp.int32, (ncp, 1), 0),
                  lax.broadcasted_iota(jnp.int32, (1, nsp), 1)).astype(MXU_DTYPE)
    imp_rows = sum(_dot(part, ov) for part in _split3(p_c.reshape(bb * HQ, ncp)))
    imp = jnp.sum(imp_rows.reshape(bb * KVH, GQ, nsp), axis=1)
    s_idx = lax.broadcasted_iota(jnp.int32, (1, nsp), 1)
    blk_q = past // SEL_BLK
    forced = (s_idx == 0) | (s_idx == blk_q) | (s_idx == blk_q - 1)
    valid = s_idx * SEL_BLK <= past
    score = jnp.where(valid, imp + FORCE_BONUS * forced.astype(F32), NEG)
    n_blocks = past // SEL_BLK + 1
    sel = _topk_mask(score, s_idx.astype(F32), min(N_SEL, n_blocks), 1)
    sel_ref[...] = jnp.where(valid, sel, 0.0)


def _nsa_select(q_bd, kc_tm, vc_tm, past, bb):
    nb, ncp, _ = kc_tm.shape
    nsp = LANES
    assert past // SEL_BLK + 1 <= nsp
    kern = functools.partial(_nsa_select_kernel, past=past)
    blk = lambda shape: pl.BlockSpec(shape, lambda i: (i, 0, 0))
    return pl.pallas_call(
        kern,
        grid=(nb // bb,),
        in_specs=[blk((bb, HQ, KV_W)), blk((bb, ncp, KV_W)), blk((bb, ncp, KV_W))],
        out_specs=[blk((bb, HQ, DH)), pl.BlockSpec((bb * KVH, nsp), lambda i: (i, 0))],
        out_shape=[jax.ShapeDtypeStruct((nb, HQ, DH), F32), jax.ShapeDtypeStruct((nb * KVH, nsp), F32)],
        compiler_params=_params(1),
        name="nsa_select_decode",
    )(q_bd, kc_tm, vc_tm)


def _nsa_attn_decode_kernel(*refs, n_pages):
    pt_ref = refs[0]
    del pt_ref
    q_ref, oc_ref, sel_ref, new_ref, g_ref = refs[1:6]
    k_pages = refs[6:6 + n_pages]
    v_pages = refs[6 + n_pages:6 + 2 * n_pages]
    kwin_ref, vwin_ref, o_ref = refs[6 + 2 * n_pages:]
    q = q_ref[0]
    qf = q.astype(F32)
    sel4 = sel_ref[0]
    sel_rows = jnp.concatenate([jnp.broadcast_to(sel4[k:k + 1], (GQ, sel4.shape[1])) for k in range(KVH)], axis=0)
    new = new_ref[0]
    first_half = lax.broadcasted_iota(jnp.int32, (1, PAGE_SIZE), 1) < SEL_BLK

    scores, masks = [], []
    for p in range(n_pages):
        scores.append(_dot_nt(q, k_pages[p][0].astype(MXU_DTYPE)))
        masks.append(jnp.where(first_half, sel_rows[:, 2 * p:2 * p + 1], sel_rows[:, 2 * p + 1:2 * p + 2]) > 0.5)
    own = 2 * n_pages
    s_new = jnp.sum(qf * new[0:1, :].astype(MXU_DTYPE).astype(F32), axis=-1, keepdims=True)
    ok_new = sel_rows[:, own:own + 1] > 0.5
    s_new = jnp.where(ok_new, s_new, NEG)
    m = s_new
    for s, ok in zip(scores, masks):
        m = jnp.maximum(m, jnp.max(jnp.where(ok, s, NEG), axis=-1, keepdims=True))
    e_new = jnp.where(ok_new, jnp.exp(s_new - m), 0.0)
    den = e_new
    acc = e_new * new[1:2, :].astype(MXU_DTYPE).astype(F32)
    for p, (s, ok) in enumerate(zip(scores, masks)):
        e = jnp.where(ok, jnp.exp(s - m), 0.0)
        den = den + jnp.sum(e, axis=-1, keepdims=True)
        acc = acc + _dot(e.astype(MXU_DTYPE), v_pages[p][0].astype(MXU_DTYPE))
    o_s = _block_diag_fold(acc * (1.0 / den))

    s_w = _dot_nt(q, kwin_ref[0].astype(MXU_DTYPE))
    s_wn = jnp.sum(qf * new[2:3, :].astype(MXU_DTYPE).astype(F32), axis=-1, keepdims=True)
    m_w = jnp.maximum(jnp.max(s_w, axis=-1, keepdims=True), s_wn)
    e_w, e_wn = jnp.exp(s_w - m_w), jnp.exp(s_wn - m_w)
    den_w = jnp.sum(e_w, axis=-1, keepdims=True) + e_wn
    acc_w = _dot(e_w.astype(MXU_DTYPE), vwin_ref[0].astype(MXU_DTYPE)) + e_wn * new[3:4, :].astype(MXU_DTYPE).astype(F32)
    o_w = _block_diag_fold(acc_w * (1.0 / den_w))

    g = g_ref[0]
    o_ref[0] = g[:, 0:1] * oc_ref[0] + g[:, 1:2] * o_s + g[:, 2:3] * o_w


def _nsa_attn_decode(page_table, q_bd, o_c, sel, new_rows, gates, pool_k, pool_v, win_k, win_v):
    nb, n_pages = page_table.shape
    wb = win_k.shape[1]
    nsp = sel.shape[-1]
    per_b = lambda shape: pl.BlockSpec((1,) + shape, lambda b, pt: (b, 0, 0))
    page = lambda p: pl.BlockSpec((1, PAGE_SIZE, KV_W), lambda b, pt: (pt[b, p], 0, 0))
    kern = functools.partial(_nsa_attn_decode_kernel, n_pages=n_pages)
    return pl.pallas_call(
        kern,
        grid_spec=pltpu.PrefetchScalarGridSpec(
            num_scalar_prefetch=1,
            grid=(nb,),
            in_specs=([per_b((HQ, KV_W)), per_b((HQ, DH)), per_b((KVH, nsp)), per_b((4, KV_W)), per_b((HQ, 3))]
                      + [page(p) for p in range(n_pages)] + [page(p) for p in range(n_pages)]
                      + [per_b((wb, KV_W)), per_b((wb, KV_W))]),
            out_specs=per_b((HQ, DH)),
        ),
        out_shape=jax.ShapeDtypeStruct((nb, HQ, DH), F32),
        compiler_params=_params(1),
        name="nsa_attn_decode",
    )(page_table, q_bd, o_c, sel.reshape(nb, KVH, nsp), new_rows, gates,
      *([pool_k] * n_pages), *([pool_v] * n_pages), win_k, win_v)


def _outproj_ln_kernel(x_ref, o_ref, z_ref, w_ref, g_ref, b_ref, y_ref, *, alpha):
    z = z_ref[...]
    a = o_ref[...] * (z * (1.0 / (1.0 + jnp.exp(-z))))
    h = alpha * x_ref[...] + _dot(a.astype(MXU_DTYPE), w_ref[...])
    d = h - jnp.mean(h, axis=-1, keepdims=True)
    var = jnp.mean(d * d, axis=-1, keepdims=True)
    y_ref[...] = d * lax.rsqrt(var + LN_EPS) * g_ref[...] + b_ref[...]


def _outproj_ln(x2d, o2d, z2d, w, ln_g, ln_b, alpha, tm):
    m, d = x2d.shape
    wdt = o2d.shape[1]
    row = lambda n: pl.BlockSpec((tm, n), lambda i: (i, 0))
    fixed = lambda shape: pl.BlockSpec(shape, lambda i: (0, 0))
    return pl.pallas_call(
        functools.partial(_outproj_ln_kernel, alpha=alpha),
        grid=(m // tm,),
        in_specs=[row(d), row(wdt), row(wdt), fixed((wdt, d)), fixed((1, d)), fixed((1, d))],
        out_specs=row(d),
        out_shape=jax.ShapeDtypeStruct((m, d), F32),
        compiler_params=_params(1),
        name="outproj_ln",
    )(x2d, o2d, z2d, w, ln_g.reshape(1, d), ln_b.reshape(1, d))


def _ret_inproj_kernel(x_ref, w_ref, c_ref, sa_ref, sb_ref, q_ref, k_ref, v_ref, z_ref, *, rdk):
    xb = x_ref[...].astype(MXU_DTYPE)
    c, sa, sb = c_ref[...], sa_ref[...], sb_ref[...]
    qk_w = RH * rdk
    col = 0
    for dst, scale in ((q_ref, 1.0), (k_ref, rdk ** -0.5)):
        for ch in range(qk_w // 512):
            a = _rotate(_dot(xb, w_ref[:, col:col + 512]), c, sa, sb, 1)
            dst[:, ch * 512:(ch + 1) * 512] = a * scale
            col += 512
    for dst in (v_ref, z_ref):
        for ch in range(2 * qk_w // 512):
            dst[:, ch * 512:(ch + 1) * 512] = _dot(xb, w_ref[:, col:col + 512])
            col += 512


def _ret_inproj(x2d, w, tabs, t, tm):
    m, d = x2d.shape
    nt = t // tm
    rdk = d // RH
    row = lambda n: pl.BlockSpec((tm, n), lambda i: (i, 0))
    tab = pl.BlockSpec((tm, rdk), lambda i: (i % nt, 0))
    return pl.pallas_call(
        functools.partial(_ret_inproj_kernel, rdk=rdk),
        grid=(m // tm,),
        in_specs=[row(d), pl.BlockSpec(w.shape, lambda i: (0, 0)), tab, tab, tab],
        out_specs=[row(d), row(d), row(2 * d), row(2 * d)],
        out_shape=[jax.ShapeDtypeStruct((m, d), F32), jax.ShapeDtypeStruct((m, d), F32),
                   jax.ShapeDtypeStruct((m, 2 * d), F32), jax.ShapeDtypeStruct((m, 2 * d), F32)],
        compiler_params=_params(1),
        name="ret_inproj",
    )(x2d, w, *tabs)


def _group_norm(o, gain):
    d = o - jnp.mean(o, axis=-1, keepdims=True)
    return d * lax.rsqrt(jnp.mean(d * d, axis=-1, keepdims=True) + LN_EPS) * gain


def _ret_prompt_kernel(q_ref, k_ref, v_ref, dm_ref, dec_ref, gn_ref, o_ref, s_ref, s_sc):
    ci = pl.program_id(2)

    @pl.when(ci == 0)
    def _():
        s_sc[...] = jnp.zeros(s_sc.shape, F32)

    q, k, v = q_ref[0], k_ref[0], v_ref[0].astype(MXU_DTYPE)
    dec = dec_ref[0]
    s = s_sc[...]
    inner = _dot_nt(q.astype(MXU_DTYPE), k.astype(MXU_DTYPE)) * dm_ref[0]
    o = _dot(inner.astype(MXU_DTYPE), v) + _dot((q * dec[:, 0:1]).astype(MXU_DTYPE), s.astype(MXU_DTYPE))
    s_new = s * dec[0:1, 2:3] + _dot_tn((k * dec[:, 1:2]).astype(MXU_DTYPE), v)
    s_sc[...] = s_new
    o_ref[0] = _group_norm(o, gn_ref[...])

    @pl.when(ci == pl.num_programs(2) - 1)
    def _():
        s_ref[0, 0] = s_new


def _ret_prompt(q, k, v, dmask, dec, gn_g):
    nb, t, qk_w = q.shape
    rdk = qk_w // RH
    rdv = v.shape[2] // RH
    c = dmask.shape[1]
    tok = lambda n: pl.BlockSpec((1, c, n), lambda b, h, i: (b, i, h))
    per_h = lambda shape: pl.BlockSpec((1,) + shape, lambda b, h, i: (h, 0, 0))
    return pl.pallas_call(
        _ret_prompt_kernel,
        grid=(nb, RH, t // c),
        in_specs=[tok(rdk), tok(rdk), tok(rdv), per_h((c, c)), per_h((c, 8)),
                  pl.BlockSpec((1, rdv), lambda b, h, i: (0, h))],
        out_specs=[tok(rdv), pl.BlockSpec((1, 1, rdk, rdv), lambda b, h, i: (b, h, 0, 0))],
        out_shape=[jax.ShapeDtypeStruct((nb, t, RH * rdv), F32),
                   jax.ShapeDtypeStruct((nb, RH, rdk, rdv), F32)],
        scratch_shapes=[pltpu.VMEM((rdk, rdv), F32)],
        compiler_params=_params(3),
        name="ret_prompt",
    )(q, k, v, dmask, dec, gn_g.reshape(1, RH * rdv))


def _ret_decode_kernel(dec_ref, q_ref, k_ref, v_ref, s0_ref, gn_ref, o_ref, s_ref, *, rdk, rdv):
    pad = LANES
    first_row = lax.broadcasted_iota(jnp.int32, (pad, 1), 0) == 0
    for h in range(RH):
        q = q_ref[0, :, h * rdk:(h + 1) * rdk]
        k = k_ref[0, :, h * rdk:(h + 1) * rdk]
        v = v_ref[0, :, h * rdv:(h + 1) * rdv]
        dmask, q_dec, k_dec, c_dec = dec_ref[h, 0], dec_ref[h, 1], dec_ref[h, 2], dec_ref[h, 3]
        s0 = s0_ref[0, h]
        k_pad = jnp.where(first_row, jnp.broadcast_to(k * k_dec, (pad, rdk)), 0.0).astype(MXU_DTYPE)
        v_pad = jnp.broadcast_to(v, (pad, rdv)).astype(MXU_DTYPE)
        s_ref[0, h] = s0 * c_dec + _dot_tn(k_pad, v_pad)
        qk = jnp.sum(q.astype(MXU_DTYPE).astype(F32) * k.astype(MXU_DTYPE).astype(F32), axis=-1, keepdims=True)
        inner = (qk * dmask).astype(MXU_DTYPE).astype(F32)
        q_pad = jnp.broadcast_to(q * q_dec, (8, rdk)).astype(MXU_DTYPE)
        o = inner * v.astype(MXU_DTYPE).astype(F32) + _dot(q_pad, s0.astype(MXU_DTYPE))[0:1, :]
        o_ref[0, :, h * rdv:(h + 1) * rdv] = _group_norm(o, gn_ref[:, h * rdv:(h + 1) * rdv])


def _ret_decode(q, k, v, s0, dec, gn_g):
    nb, qk_w = q.shape
    rdk, rdv = qk_w // RH, v.shape[1] // RH
    vec = lambda n: pl.BlockSpec((1, 1, n), lambda b: (b, 0, 0))
    state = pl.BlockSpec((1, RH, rdk, rdv), lambda b: (b, 0, 0, 0))
    return pl.pallas_call(
        functools.partial(_ret_decode_kernel, rdk=rdk, rdv=rdv),
        grid=(nb,),
        in_specs=[pl.BlockSpec(memory_space=pltpu.SMEM), vec(qk_w), vec(qk_w), vec(RH * rdv), state,
                  pl.BlockSpec((1, RH * rdv), lambda b: (0, 0))],
        out_specs=[vec(RH * rdv), state],
        out_shape=[jax.ShapeDtypeStruct((nb, 1, RH * rdv), F32), jax.ShapeDtypeStruct(s0.shape, F32)],
        compiler_params=_params(1),
        name="ret_decode",
    )(dec, q.reshape(nb, 1, qk_w), k.reshape(nb, 1, qk_w), v.reshape(nb, 1, RH * rdv), s0,
      gn_g.reshape(1, RH * rdv))


def _rope_tables(pos):
    half = ROT_DIM // 2
    inv = ROPE_THETA ** (-jnp.arange(half, dtype=F32) / half)
    ang = pos.astype(F32)[:, None] * inv[None, :]
    lane = jnp.arange(LANES) % DH
    cos, sin = jnp.cos(ang)[:, lane % half], jnp.sin(ang)[:, lane % half]
    c = jnp.where(lane < ROT_DIM, cos, 1.0)
    sa = jnp.where((lane >= half) & (lane < ROT_DIM), sin, 0.0)
    sb = jnp.where(lane < half, -sin, 0.0)
    return c, sa, sb


def _xpos_tables(pos, rdk):
    half = rdk // 2
    inv = 1.0 / (XPOS_BASE ** jnp.linspace(0.0, 1.0, half, dtype=F32))
    ang = pos.astype(F32)[:, None] * inv[None, :]
    lane = jnp.arange(rdk)
    cos, sin = jnp.cos(ang)[:, lane // 2], jnp.sin(ang)[:, lane // 2]
    sa = jnp.where(lane % 2 == 1, sin, 0.0)
    sb = jnp.where(lane % 2 == 0, -sin, 0.0)
    return cos, sa, sb


def _decay_tables(c):
    log_g = jnp.log1p(-jnp.power(2.0, -5.0 - jnp.arange(RH, dtype=F32)))
    i = jnp.arange(c, dtype=F32)
    diff = i[:, None] - i[None, :]
    dmask = jnp.where(diff >= 0, jnp.exp(jnp.maximum(diff, 0.0)[None] * log_g[:, None, None]), 0.0)
    q_dec = jnp.exp((i + 1.0)[None, :] * log_g[:, None])
    k_dec = jnp.exp((c - 1.0 - i)[None, :] * log_g[:, None])
    c_dec = jnp.exp(c * log_g)
    return dmask, q_dec, k_dec, c_dec


def _nsa_w_in_layout(w):
    kv_end = NSA_W + 6 * KV_W
    gates = jnp.pad(w[:, kv_end:kv_end + 3 * HQ], ((0, 0), (0, GATE_PAD - 3 * HQ)))
    return jnp.concatenate([w[:, :kv_end], gates, w[:, kv_end + 3 * HQ:]], axis=1).astype(MXU_DTYPE)


def _row_tile(m):
    for tm in (256, 128, 64, 32, 16, 8):
        if m % tm == 0:
            return tm
    raise ValueError(f"row count {m} is not a multiple of 8")


def _nsa_layer(xp, xs, page_table, pools, wins, w_in, w_out, cmp_k, cmp_v, ln_g, ln_b, alpha):
    nb, t, d = xp.shape
    nd = xs.shape[0]
    past = page_table.shape[1] * PAGE_SIZE
    assert xs.shape[1] == 1 and t % QBLK == 0 and t >= WINDOW + QBLK
    w_in = _nsa_w_in_layout(w_in)
    w_out = w_out.astype(MXU_DTYPE)
    heads = lambda a: a.reshape(a.shape[0], -1, KVH, DH)

    xp2 = xp.reshape(nb * t, d)
    tm = _row_tile(t)
    q_hm, kv, khm, g, z = _nsa_inproj(xp2, w_in, _rope_tables(jnp.arange(t)), nb, t, tm)
    kv = kv.reshape(6, nb, t, KV_W)
    kc_hm = _compress_prompt(kv[0], cmp_k)
    vc_hm = _compress_prompt(kv[1], cmp_v)
    gates = g[:, :3 * HQ].reshape(nb, t, 3, KVH, GQ).transpose(0, 3, 1, 2, 4).reshape(nb, KVH, t, 3 * GQ)
    o = _nsa_attn_prompt(q_hm, kc_hm, vc_hm, khm, gates)
    yp = _outproj_ln(xp2, o.reshape(nb * t, NSA_W), z, w_out, ln_g, ln_b, alpha, tm).reshape(nb, t, d)
    wb = min(WINDOW, t)
    state_p = [heads(kv[j]) for j in range(4)] + [heads(kv[4][:, t - wb:]), heads(kv[5][:, t - wb:])]

    xs2 = xs.reshape(nd, d)
    tms = _row_tile(nd)
    pos_s = jnp.full((nd,), past, jnp.int32)
    q_s, kv_s, _, g_s, z_s = _nsa_inproj(xs2, w_in, _rope_tables(pos_s), 1, nd, tms)
    q_tok = q_s[0].transpose(1, 0, 2).astype(F32).reshape(nd, KVH, GQ, 1, DH)
    q_bd = (q_tok * jnp.eye(KVH, dtype=F32)[None, :, None, :, None]).reshape(nd, HQ, KV_W).astype(MXU_DTYPE)
    pool = lambda a: a.reshape(a.shape[0], PAGE_SIZE, KV_W)
    kc_tm = _compress_paged(pool(pools[0]), page_table, cmp_k)
    vc_tm = _compress_paged(pool(pools[1]), page_table, cmp_v)
    o_c, sel = _nsa_select(q_bd, kc_tm, vc_tm, past, min(16, nd))
    new_rows = jnp.stack([kv_s[2], kv_s[3], kv_s[4], kv_s[5]], axis=1)
    gates_s = g_s[:, :3 * HQ].reshape(nd, 3, HQ).transpose(0, 2, 1)
    win_k, win_v = (w.reshape(nd, -1, KV_W) for w in wins)
    o_s = _nsa_attn_decode(page_table, q_bd, o_c, sel, new_rows, gates_s, pool(pools[2]), pool(pools[3]),
                           win_k, win_v)
    ys = _outproj_ln(xs2, o_s.reshape(nd, NSA_W), z_s, w_out, ln_g, ln_b, alpha, tms).reshape(nd, 1, d)
    new_win = lambda w, row: jnp.concatenate([w, row[:, None, :]], axis=1)[:, 1:]
    state_s = [heads(kv_s[j][:, None, :]) for j in range(4)] + [heads(new_win(win_k, kv_s[4])),
                                                                 heads(new_win(win_v, kv_s[5]))]
    return yp, ys, state_p, state_s


def _ret_layer(xp, xs, s0, past, w_in, gn_g, w_out, ln_g, ln_b, alpha):
    nb, t, d = xp.shape
    nd = xs.shape[0]
    rdk = d // RH
    assert xs.shape[1] == 1 and t % RET_CHUNK == 0
    w_in = w_in.astype(MXU_DTYPE)
    w_out = w_out.astype(MXU_DTYPE)

    xp2 = xp.reshape(nb * t, d)
    tm = _row_tile(t)
    q, k, v, z = _ret_inproj(xp2, w_in, _xpos_tables(jnp.arange(t), rdk), t, tm)
    dmask, q_dec, k_dec, c_dec = _decay_tables(RET_CHUNK)
    dec = jnp.stack([q_dec, k_dec, jnp.broadcast_to(c_dec[:, None], q_dec.shape)]
                    + [jnp.zeros_like(q_dec)] * 5, axis=-1)
    o, sp = _ret_prompt(q.reshape(nb, t, d), k.reshape(nb, t, d), v.reshape(nb, t, 2 * d), dmask, dec, gn_g)
    yp = _outproj_ln(xp2, o.reshape(nb * t, 2 * d), z, w_out, ln_g, ln_b, alpha, tm).reshape(nb, t, d)

    xs2 = xs.reshape(nd, d)
    tms = _row_tile(nd)
    q, k, v, z = _ret_inproj(xs2, w_in, _xpos_tables(jnp.full((nd,), past, jnp.int32), rdk), nd, tms)
    dmask, q_dec, k_dec, c_dec = _decay_tables(1)
    dec = jnp.stack([dmask[:, 0, 0], q_dec[:, 0], k_dec[:, 0], c_dec], axis=-1)
    o, ss = _ret_decode(q, k, v, s0, dec, gn_g)
    ys = _outproj_ln(xs2, o.reshape(nd, 2 * d), z, w_out, ln_g, ln_b, alpha, tms).reshape(nd, 1, d)
    return yp, ys, sp, ss


def kernel(x_prompt, x_sample, cache_k_cmp, cache_v_cmp, cache_k_sel, cache_v_sel, cache_k_win, cache_v_win,
           state_ret, page_table, nsa_w_in, nsa_w_out, nsa_pe_k, nsa_w1_k, nsa_w2_k, nsa_pe_v, nsa_w1_v,
           nsa_w2_v, ret_w_in, ret_gn_g, ret_w_out, ln_g, ln_b):
    depth = ln_g.shape[0]
    alpha = (2.0 * depth) ** 0.25
    past = page_table.shape[1] * PAGE_SIZE
    xp, xs = x_prompt, x_sample
    nsa_p, nsa_s, ret_p, ret_s = [], [], [], []
    for i in range(depth):
        li = i // 2
        if i % 2 == 0:
            cmp_k = _compress_weights(nsa_pe_k[li], nsa_w1_k[li], nsa_w2_k[li])
            cmp_v = _compress_weights(nsa_pe_v[li], nsa_w1_v[li], nsa_w2_v[li])
            xp, xs, st_p, st_s = _nsa_layer(
                xp, xs, page_table,
                (cache_k_cmp[li], cache_v_cmp[li], cache_k_sel[li], cache_v_sel[li]),
                (cache_k_win[li], cache_v_win[li]),
                nsa_w_in[li], nsa_w_out[li], cmp_k, cmp_v, ln_g[i], ln_b[i], alpha)
            nsa_p.append(st_p)
            nsa_s.append(st_s)
        else:
            xp, xs, sp, ss = _ret_layer(xp, xs, state_ret[li], past, ret_w_in[li], ret_gn_g[li], ret_w_out[li],
                                        ln_g[i], ln_b[i], alpha)
            ret_p.append(sp)
            ret_s.append(ss)
    stk = lambda lst, j: jnp.stack([e[j] for e in lst])
    return (xp, xs,
            stk(nsa_p, 0), stk(nsa_s, 0), stk(nsa_p, 1), stk(nsa_s, 1),
            stk(nsa_p, 2), stk(nsa_s, 2), stk(nsa_p, 3), stk(nsa_s, 3),
            stk(nsa_p, 4), stk(nsa_s, 4), stk(nsa_p, 5), stk(nsa_s, 5),
            jnp.stack(ret_p), jnp.stack(ret_s))
```

```python
import functools

import jax
import jax.numpy as jnp
from jax import lax
from jax.experimental import pallas as pl
from jax.experimental.pallas import tpu as pltpu

F32 = jnp.float32
MXU_DTYPE = jnp.bfloat16

HQ, KVH, DH = 32, 4, 64
GQ = HQ // KVH
NSA_W, KV_W = HQ * DH, KVH * DH
ROT_DIM, ROPE_THETA = DH // 4, 500000.0
CMP_LEN, CMP_STRIDE, CMP_HID = 32, 16, 2 * DH
SEL_BLK, N_SEL, WINDOW, QBLK = 64, 16, 512, 128
FORCE_BONUS, NEG = 1000.0, -1.0e30
RH, RET_CHUNK, XPOS_BASE = 4, 128, 10000.0
PAGE_SIZE, LN_EPS = 128, 1e-5

LANES = 128
VMEM_LIMIT_BYTES = 56 * 1024 * 1024
CMP_GROUP = 2048
GATE_PAD = LANES
KEY_TILE = 2 * SEL_BLK
SEL_TILE = 4 * KEY_TILE
WIN_TILE = WINDOW + QBLK
ONES_ROWS = 16
LOG2E = 1.4426950408889634
REMOVED = -3.0e38
RUNNING_MAX_INIT = -1.0e29


def _params(n_axes):
    return pltpu.CompilerParams(dimension_semantics=("arbitrary",) * n_axes,
                                vmem_limit_bytes=VMEM_LIMIT_BYTES)


def _dot(a, b):
    return jnp.dot(a, b, preferred_element_type=F32)


def _dot_nt(a, b):
    return lax.dot_general(a, b, (((1,), (1,)), ((), ())), preferred_element_type=F32)


def _dot_tn(a, b):
    return lax.dot_general(a, b, (((0,), (0,)), ((), ())), preferred_element_type=F32)


def _lane_tile(t, reps):
    return t if reps == 1 else jnp.concatenate([t] * reps, axis=1)


def _rotate(a, c, sa, sb, shift):
    n = a.shape[1]
    reps = n // c.shape[1]
    c, sa, sb = _lane_tile(c, reps), _lane_tile(sa, reps), _lane_tile(sb, reps)
    return a * c + pltpu.roll(a, shift, 1) * sa + pltpu.roll(a, n - shift, 1) * sb


def _masked_softmax(s, mask):
    s = jnp.where(mask, s, NEG)
    m = jnp.max(s, axis=-1, keepdims=True)
    e = jnp.where(mask, jnp.exp2(s - m), 0.0)
    d = jnp.sum(e, axis=-1, keepdims=True)
    return e * (1.0 / jnp.where(d > 0.0, d, 1.0))


def _split3(x):
    hi = x.astype(MXU_DTYPE)
    r = x - hi.astype(F32)
    mid = r.astype(MXU_DTYPE)
    lo = (r - mid.astype(F32)).astype(MXU_DTYPE)
    return hi, mid, lo


def _topk_mask(score, idx, n_pick, axis):
    big = jnp.float32(score.shape[axis])
    sel = jnp.zeros(score.shape, F32)
    for _ in range(n_pick):
        m = jnp.max(score, axis=axis, keepdims=True)
        first = jnp.min(jnp.where(score == m, idx, big), axis=axis, keepdims=True)
        pick = idx == first
        sel = jnp.where(pick, 1.0, sel)
        score = jnp.where(pick, REMOVED, score)
    return sel


def _overlap(c_idx, s_idx):
    cs, ss = c_idx * CMP_STRIDE, s_idx * SEL_BLK
    ov = jnp.minimum(cs + CMP_LEN, ss + SEL_BLK) - jnp.maximum(cs, ss)
    return jnp.maximum(ov, 0).astype(F32) * (1.0 / CMP_LEN)


def _alias_specs(prev):
    return [pl.BlockSpec(memory_space=pl.ANY)] * len(prev)


def _nsa_inproj_kernel(x_ref, w_ref, c_ref, sa_ref, sb_ref, *rest):
    q_ref, kv_ref, khm_ref, vt_ref, gt_ref, z_ref = rest[-12:-6]
    cache_refs = rest[-6:]
    xb = x_ref[...].astype(MXU_DTYPE)
    c, sa, sb = c_ref[...], sa_ref[...], sb_ref[...]
    half = ROT_DIM // 2
    col = 0
    for ch in range(NSA_W // 512):
        a = _dot(xb, w_ref[:, col:col + 512])
        a = _rotate(a, c, sa, sb, half) * (DH ** -0.5 * LOG2E)
        for h in range(512 // DH):
            q_ref[0, ch * (512 // DH) + h] = a[:, h * DH:(h + 1) * DH].astype(q_ref.dtype)
        col += 512
    for j in range(6):
        a = _dot(xb, w_ref[:, col:col + KV_W])
        if j % 2 == 0:
            a = _rotate(a, c, sa, sb, half)
        kv_ref[j] = a
        a_t = a.T
        cache_refs[j][0, 0] = a_t
        if j in (2, 4):
            for h in range(KVH):
                khm_ref[j // 2 - 1, 0, h] = a[:, h * DH:(h + 1) * DH].astype(khm_ref.dtype)
        if j in (3, 5):
            vt_ref[j // 2 - 1, 0] = a_t.astype(vt_ref.dtype)
        col += KV_W
    a = _dot(xb, w_ref[:, col:col + GATE_PAD])
    gt_ref[0] = (1.0 / (1.0 + jnp.exp(-a))).T
    col += GATE_PAD
    for ch in range(NSA_W // 512):
        z_ref[:, ch * 512:(ch + 1) * 512] = _dot(xb, w_ref[:, col:col + 512])
        col += 512


def _nsa_inproj(x2d, w, tabs, nb, t, tm, li, n_layers, prev):
    m, d = x2d.shape
    nt = t // tm
    n_cols = w.shape[1]
    row = lambda i: (i, 0)
    tab = pl.BlockSpec((tm, LANES), lambda i: (i % nt, 0))
    cache_spec = pl.BlockSpec((1, 1, KV_W, tm), lambda i: (li, i // nt, 0, i % nt))
    cache_shape = jax.ShapeDtypeStruct((n_layers, nb, KV_W, t), F32)
    n_in = 5
    return pl.pallas_call(
        _nsa_inproj_kernel,
        grid=(m // tm,),
        in_specs=[pl.BlockSpec((tm, d), row), pl.BlockSpec((d, n_cols), lambda i: (0, 0)), tab, tab, tab]
        + _alias_specs(prev),
        out_specs=[
            pl.BlockSpec((1, HQ, tm, DH), lambda i: (i // nt, 0, i % nt, 0)),
            pl.BlockSpec((6, tm, KV_W), lambda i: (0, i, 0)),
            pl.BlockSpec((2, 1, KVH, tm, DH), lambda i: (0, i // nt, 0, i % nt, 0)),
            pl.BlockSpec((2, 1, KV_W, tm), lambda i: (0, i // nt, 0, i % nt)),
            pl.BlockSpec((1, GATE_PAD, tm), lambda i: (i // nt, 0, i % nt)),
            pl.BlockSpec((tm, NSA_W), row),
        ] + [cache_spec] * 6,
        out_shape=[
            jax.ShapeDtypeStruct((nb, HQ, t, DH), MXU_DTYPE),
            jax.ShapeDtypeStruct((6, m, KV_W), F32),
            jax.ShapeDtypeStruct((2, nb, KVH, t, DH), MXU_DTYPE),
            jax.ShapeDtypeStruct((2, nb, KV_W, t), MXU_DTYPE),
            jax.ShapeDtypeStruct((nb, GATE_PAD, t), F32),
            jax.ShapeDtypeStruct((m, NSA_W), F32),
        ] + [cache_shape] * 6,
        input_output_aliases={n_in + j: 6 + j for j in range(len(prev))},
        compiler_params=_params(1),
        name="nsa_inproj",
    )(x2d, w, *tabs, *prev)


def _compress_kernel(*refs, n_prefetch, n_pages, layout, n_chunks):
    split = KV_W // LANES
    pos = n_prefetch
    if n_pages:
        page_refs = refs[pos:pos + n_pages]
        pos += n_pages
    else:
        tok_ref, halo_ref = refs[pos:pos + 2]
        pos += 2
    wbig_ref, pe_ref, w1_ref, w2_ref, out_ref, r_ref = refs[pos:pos + 6]
    r_ref[n_chunks:, :] = jnp.zeros((8, CMP_STRIDE * KV_W), F32)
    if n_pages:
        tok_sc = refs[pos + 6]
        for p, ref in enumerate(page_refs):
            page = ref[0, 0].T
            for s in range(split):
                tok_sc[s, p * PAGE_SIZE:(p + 1) * PAGE_SIZE, :] = page[:, s * LANES:(s + 1) * LANES]
        for j in range(CMP_STRIDE * split):
            r_ref[0:n_chunks, j * LANES:(j + 1) * LANES] = (
                tok_sc[j % split, pl.ds(j // split, n_chunks, stride=CMP_STRIDE), :])
    else:
        for j in range(CMP_STRIDE * split):
            lanes = slice(j * LANES, (j + 1) * LANES)
            r_ref[0:n_chunks, lanes] = tok_ref[0, pl.ds(j, n_chunks, stride=CMP_STRIDE * split), :]
            r_ref[n_chunks:n_chunks + 1, lanes] = halo_ref[0, j:j + 1, :]
    ab = _dot(r_ref[...].astype(MXU_DTYPE), wbig_ref[...])
    pe8 = jnp.broadcast_to(pe_ref[...], (8, CMP_LEN * DH)).astype(MXU_DTYPE)
    bias = _dot(pe8, w1_ref[...])[0:1, :]
    w2 = w2_ref[...]
    toks = []
    for h in range(KVH):
        base = h * 2 * CMP_HID
        first_half = ab[0:n_chunks, base:base + CMP_HID]
        second_half = ab[1:n_chunks + 1, base + CMP_HID:base + 2 * CMP_HID]
        u = first_half + second_half + bias
        hid = 0.5 * u * (1.0 + jnp.tanh(0.7978845608028654 * (u + 0.044715 * (u * u * u))))
        toks.append(_dot(hid.astype(MXU_DTYPE), w2))
    if layout == "head_major":
        for h in range(KVH):
            out_ref[0, h] = toks[h].astype(out_ref.dtype)
    elif layout == "token_major":
        out_ref[0] = jnp.concatenate(toks, axis=1).astype(out_ref.dtype)
    else:
        out_ref[0] = jnp.concatenate(toks, axis=1).T.astype(out_ref.dtype)


def _compress_weights(pe, w1, w2):
    w1r = w1.reshape(2, CMP_STRIDE, DH, CMP_HID)
    eye = jnp.eye(KVH, dtype=w1.dtype)
    wbig = jnp.einsum("sldn,hg->lhdgsn", w1r, eye).reshape(CMP_STRIDE * KV_W, KVH * 2 * CMP_HID)
    return (wbig.astype(MXU_DTYPE), pe.reshape(1, CMP_LEN * DH), w1.astype(MXU_DTYPE), w2.astype(MXU_DTYPE))


def _compress_prompt(k_tok, weights, layout):
    nb, t, _ = k_tok.shape
    gt = min(CMP_GROUP, t)
    n_chunks = gt // CMP_STRIDE
    ng = t // gt
    last_halo = t // CMP_STRIDE - 1
    wbig, pe, w1, w2 = weights
    split = KV_W // LANES
    k_tok = k_tok.reshape(nb, t * split, LANES)
    full = lambda a: pl.BlockSpec(a.shape, lambda b, g: (0,) * a.ndim)
    kern = functools.partial(_compress_kernel, n_prefetch=0, n_pages=0, layout=layout, n_chunks=n_chunks)
    if layout == "head_major":
        out_spec = pl.BlockSpec((1, KVH, n_chunks, DH), lambda b, g: (b, 0, g, 0))
        out_shape = jax.ShapeDtypeStruct((nb, KVH, t // CMP_STRIDE, DH), MXU_DTYPE)
    else:
        out_spec = pl.BlockSpec((1, KV_W, n_chunks), lambda b, g: (b, 0, g))
        out_shape = jax.ShapeDtypeStruct((nb, KV_W, t // CMP_STRIDE), MXU_DTYPE)
    return pl.pallas_call(
        kern,
        grid=(nb, ng),
        in_specs=[pl.BlockSpec((1, gt * split, LANES), lambda b, g: (b, g, 0)),
                  pl.BlockSpec((1, CMP_STRIDE * split, LANES),
                               lambda b, g: (b, jnp.minimum((g + 1) * n_chunks, last_halo), 0)),
                  full(wbig), full(pe), full(w1), full(w2)],
        out_specs=out_spec,
        out_shape=out_shape,
        scratch_shapes=[pltpu.VMEM((n_chunks + 8, CMP_STRIDE * KV_W), F32)],
        compiler_params=_params(2),
        name="nsa_compress_prompt",
    )(k_tok, k_tok, wbig, pe, w1, w2)


def _compress_paged(pool, li, page_table, weights):
    nb, n_pages = page_table.shape
    n_chunks = n_pages * PAGE_SIZE // CMP_STRIDE
    wbig, pe, w1, w2 = weights
    split = KV_W // LANES
    full = lambda a: pl.BlockSpec(a.shape, lambda b, pt: (0,) * a.ndim)
    page = lambda p: pl.BlockSpec((1, 1, KV_W, PAGE_SIZE), lambda b, pt: (li, pt[b, p], 0, 0))
    kern = functools.partial(_compress_kernel, n_prefetch=1, n_pages=n_pages, layout="token_major",
                             n_chunks=n_chunks)
    return pl.pallas_call(
        kern,
        grid_spec=pltpu.PrefetchScalarGridSpec(
            num_scalar_prefetch=1,
            grid=(nb,),
            in_specs=[page(p) for p in range(n_pages)] + [full(wbig), full(pe), full(w1), full(w2)],
            out_specs=pl.BlockSpec((1, n_chunks, KV_W), lambda b, pt: (b, 0, 0)),
            scratch_shapes=[pltpu.VMEM((n_chunks + 8, CMP_STRIDE * KV_W), F32),
                            pltpu.VMEM((split, n_pages * PAGE_SIZE, LANES), F32)],
        ),
        out_shape=jax.ShapeDtypeStruct((nb, n_chunks, KV_W), MXU_DTYPE),
        compiler_params=_params(1),
        name="nsa_compress_paged",
    )(page_table, *([pool] * n_pages), wbig, pe, w1, w2)


def _nsa_attn_prompt_kernel(q_ref, kc_ref, vct_ref, ks_ref, vst_ref, kw_ref, vwt_ref, gc_ref, gs_ref, gw_ref,
                            o_ref, selb_sc, bias_sc, wbias_sc, ot_sc, m_sc, alpha_sc, acc_sc, p_sc, s_sc, *, seq_len):
    st = pl.program_id(2) * QBLK
    ncp = kc_ref.shape[2]
    ns = seq_len // SEL_BLK
    n_pairs = GQ // 2
    qpos = st + lax.broadcasted_iota(jnp.int32, (1, QBLK), 1)
    q_pairs = [q_ref[0, 2 * j:2 * j + 2].reshape(2 * QBLK, DH) for j in range(n_pairs)]
    both = lambda b: jnp.concatenate([b, b], axis=1)

    c_row = lax.broadcasted_iota(jnp.int32, (ncp, 1), 0)
    keep_c = (c_row * CMP_STRIDE + (CMP_LEN - 1) <= qpos).astype(F32)
    bias_c = both((1.0 - keep_c) * NEG)
    keep_c = both(keep_c)
    kc, vct = kc_ref[0, 0], vct_ref[0]
    p_sum = jnp.zeros((ncp, QBLK), F32)
    scores = [_dot_nt(kc, q_pairs[j]) for j in range(n_pairs)]
    probs = []
    for j in range(n_pairs):
        s = scores[j] + bias_c
        e = jnp.exp2(s - jnp.max(s, axis=0, keepdims=True)) * keep_c
        den = jnp.sum(e, axis=0, keepdims=True)
        p = e * (1.0 / jnp.where(den > 0.0, den, 1.0))
        probs.append(p.astype(MXU_DTYPE))
        p_sum = p_sum + p[:, :QBLK] + p[:, QBLK:]
    o_c = [_dot(vct, p) for p in probs]

    s_idx = lax.broadcasted_iota(jnp.int32, (ns, 1), 0)
    ov_t = _overlap(lax.broadcasted_iota(jnp.int32, (1, ncp), 1), s_idx).astype(MXU_DTYPE)
    imp_t = sum(_dot(ov_t, part) for part in _split3(p_sum))
    blk_q = qpos // SEL_BLK
    forced = (s_idx == 0) | (s_idx == blk_q) | (s_idx == blk_q - 1)
    valid = s_idx * SEL_BLK <= qpos
    score = jnp.where(valid, imp_t + FORCE_BONUS * forced.astype(F32), NEG)
    sel_t = _topk_mask(score, s_idx.astype(F32), min(N_SEL, ns), 0)
    selb_sc[...] = jnp.where(valid & (sel_t > 0.5), 0.0, NEG)

    k_row = lax.broadcasted_iota(jnp.int32, (KEY_TILE, 1), 0)

    def sel_bias(j, carry):
        k0 = pl.multiple_of(j * KEY_TILE, KEY_TILE)
        blocks = [jnp.broadcast_to(selb_sc[pl.ds(2 * j + i, 1), :], (SEL_BLK, QBLK)) for i in range(2)]
        bias_sc[pl.ds(k0, KEY_TILE), :] = jnp.where(k0 + k_row <= qpos, jnp.concatenate(blocks, axis=0), NEG)
        return carry

    n_tiles = (st + QBLK + SEL_TILE - 1) // SEL_TILE
    lax.fori_loop(0, n_tiles * (SEL_TILE // KEY_TILE), sel_bias, 0)
    ws = pl.multiple_of(jnp.maximum(st - WINDOW, 0), KEY_TILE)
    for j in range(WIN_TILE // KEY_TILE):
        dist = qpos - (ws + j * KEY_TILE + k_row)
        wbias_sc[j * KEY_TILE:(j + 1) * KEY_TILE, :] = jnp.where((dist >= 0) & (dist <= WINDOW), 0.0, NEG)

    def attend(k_ref, vt_ref, b_ref, base, n, tile):
        m_sc[...] = jnp.full(m_sc.shape, RUNNING_MAX_INIT, F32)
        acc_sc[...] = jnp.zeros(acc_sc.shape, F32)
        alpha_sc[...] = jnp.ones(alpha_sc.shape, F32)
        p_sc[:, :tile] = jnp.zeros((n_pairs, tile, 2 * QBLK), MXU_DTYPE)
        ones_rows = jnp.ones((ONES_ROWS, tile), MXU_DTYPE)

        def key_start(i):
            return pl.multiple_of(base + i * tile, KEY_TILE)

        def fold(i):
            vt_tile = jnp.concatenate([vt_ref[0, 0, :, pl.ds(key_start(i), tile)], ones_rows], axis=0)
            for j in range(n_pairs):
                acc_sc[j] = alpha_sc[j] * acc_sc[j] + _dot(vt_tile, p_sc[j, :tile])

        def scores(i):
            k_tile = k_ref[0, 0, 0, pl.ds(key_start(i), tile), :]
            return [_dot_nt(k_tile, q_ref[0, 2 * j:2 * j + 2].reshape(2 * QBLK, DH)) for j in range(n_pairs)]

        def softmax(i, raw):
            bias = both(b_ref[pl.ds(pl.multiple_of(i * tile, KEY_TILE), tile), :])
            for j in range(n_pairs):
                s = raw(j) + bias
                m_old = m_sc[j]
                m_new = jnp.maximum(m_old, jnp.max(s, axis=0, keepdims=True))
                p_sc[j, :tile] = jnp.exp2(s - m_new).astype(MXU_DTYPE)
                alpha_sc[j] = jnp.exp2(m_old - m_new)
                m_sc[j] = m_new

        if isinstance(n, int):
            for i in range(n):
                fold(max(i - 1, 0))
                tile_scores = scores(i)
                softmax(i, lambda j: tile_scores[j])
        else:
            for j, s in enumerate(scores(0)):
                s_sc[j] = s

            def step(i, carry):
                fold(jnp.maximum(i - 1, 0))
                ahead = scores(jnp.minimum(i + 1, n - 1))
                softmax(i, lambda j: s_sc[j])
                for j in range(n_pairs):
                    s_sc[j] = ahead[j]
                return carry

            lax.fori_loop(0, n, step, 0)
        fold(n - 1)
        return [acc_sc[j, :DH] * (1.0 / acc_sc[j, DH:DH + 1]) for j in range(n_pairs)]

    o_s = attend(ks_ref, vst_ref, bias_sc, 0, n_tiles, SEL_TILE)
    o_w = attend(kw_ref, vwt_ref, wbias_sc, ws, 1, WIN_TILE)
    gc, gs, gw = gc_ref[0], gs_ref[0], gw_ref[0]
    for h in range(GQ):
        j, cols = h // 2, slice((h % 2) * QBLK, (h % 2 + 1) * QBLK)
        ot_sc[h * DH:(h + 1) * DH, :] = (gc[h:h + 1] * o_c[j][:, cols] + gs[h:h + 1] * o_s[j][:, cols]
                                         + gw[h:h + 1] * o_w[j][:, cols])
    o_ref[0] = ot_sc[...].T


def _nsa_attn_prompt(q_hm, kc_hm, vc_t, khm, v_t, gates_t):
    nb, _, t, _ = q_hm.shape
    ncp = kc_hm.shape[2]
    ns = t // SEL_BLK
    k_spec = lambda j: pl.BlockSpec((1, 1, 1, t, DH), lambda b, h, i: (j, b, h, 0, 0))
    vt_spec = lambda j: pl.BlockSpec((1, 1, DH, t), lambda b, h, i: (j, b, h, 0))
    gate_spec = lambda br: pl.BlockSpec((1, GQ, QBLK), lambda b, h, i: (b, br * KVH + h, i))
    kern = functools.partial(_nsa_attn_prompt_kernel, seq_len=t)
    return pl.pallas_call(
        kern,
        grid=(nb, KVH, t // QBLK),
        in_specs=[pl.BlockSpec((1, GQ, QBLK, DH), lambda b, h, i: (b, h, i, 0)),
                  pl.BlockSpec((1, 1, ncp, DH), lambda b, h, i: (b, h, 0, 0)),
                  pl.BlockSpec((1, DH, ncp), lambda b, h, i: (b, h, 0)),
                  k_spec(0), vt_spec(0), k_spec(1), vt_spec(1),
                  gate_spec(0), gate_spec(1), gate_spec(2)],
        out_specs=pl.BlockSpec((1, QBLK, GQ * DH), lambda b, h, i: (b, i, h)),
        out_shape=jax.ShapeDtypeStruct((nb, t, NSA_W), F32),
        scratch_shapes=[pltpu.VMEM((ns, QBLK), F32), pltpu.VMEM((t, QBLK), F32),
                        pltpu.VMEM((WIN_TILE, QBLK), F32), pltpu.VMEM((GQ * DH, QBLK), F32),
                        pltpu.VMEM((GQ // 2, 1, 2 * QBLK), F32), pltpu.VMEM((GQ // 2, 1, 2 * QBLK), F32),
                        pltpu.VMEM((GQ // 2, DH + ONES_ROWS, 2 * QBLK), F32),
                        pltpu.VMEM((GQ // 2, max(SEL_TILE, WIN_TILE), 2 * QBLK), MXU_DTYPE),
                        pltpu.VMEM((GQ // 2, SEL_TILE, 2 * QBLK), F32)],
        compiler_params=_params(3),
        name="nsa_attn_prompt",
    )(q_hm, kc_hm, vc_t, khm, v_t, khm, v_t, gates_t, gates_t, gates_t)


def _block_diag_fold(x):
    r_kv = lax.broadcasted_iota(jnp.int32, (HQ, KV_W), 0) // GQ
    l_kv = lax.broadcasted_iota(jnp.int32, (HQ, KV_W), 1) // DH
    x = jnp.where(r_kv == l_kv, x, 0.0)
    return sum(x[..., h * DH:(h + 1) * DH] for h in range(KVH))


def _nsa_select_kernel(q_ref, kc_ref, vc_ref, oc_ref, sel_ref, *, past):
    bb, ncp = kc_ref.shape[0], kc_ref.shape[1]
    nsp = sel_ref.shape[1]
    q = q_ref[...]
    s = jnp.einsum("bqd,bkd->bqk", q, kc_ref[...], preferred_element_type=F32)
    c_idx = lax.broadcasted_iota(jnp.int32, (1, 1, ncp), 2)
    p_c = _masked_softmax(s, c_idx * CMP_STRIDE + (CMP_LEN - 1) <= past)
    o_c = jnp.einsum("bqk,bkd->bqd", p_c.astype(MXU_DTYPE), vc_ref[...], preferred_element_type=F32)
    oc_ref[...] = _block_diag_fold(o_c)
    ov = _overlap(lax.broadcasted_iota(jnp.int32, (ncp, 1), 0),
                  lax.broadcasted_iota(jnp.int32, (1, nsp), 1)).astype(MXU_DTYPE)
    imp_rows = sum(_dot(part, ov) for part in _split3(p_c.reshape(bb * HQ, ncp)))
    imp = jnp.sum(imp_rows.reshape(bb * KVH, GQ, nsp), axis=1)
    s_idx = lax.broadcasted_iota(jnp.int32, (1, nsp), 1)
    blk_q = past // SEL_BLK
    forced = (s_idx == 0) | (s_idx == blk_q) | (s_idx == blk_q - 1)
    valid = s_idx * SEL_BLK <= past
    score = jnp.where(valid, imp + FORCE_BONUS * forced.astype(F32), NEG)
    n_blocks = past // SEL_BLK + 1
    sel = _topk_mask(score, s_idx.astype(F32), min(N_SEL, n_blocks), 1)
    sel_ref[...] = jnp.where(valid, sel, 0.0)


def _nsa_select(q_bd, kc_tm, vc_tm, past, bb):
    nb, ncp, _ = kc_tm.shape
    nsp = LANES
    assert past // SEL_BLK + 1 <= nsp
    kern = functools.partial(_nsa_select_kernel, past=past)
    blk = lambda shape: pl.BlockSpec(shape, lambda i: (i, 0, 0))
    return pl.pallas_call(
        kern,
        grid=(nb // bb,),
        in_specs=[blk((bb, HQ, KV_W)), blk((bb, ncp, KV_W)), blk((bb, ncp, KV_W))],
        out_specs=[blk((bb, HQ, DH)), pl.BlockSpec((bb * KVH, nsp), lambda i: (i, 0))],
        out_shape=[jax.ShapeDtypeStruct((nb, HQ, DH), F32), jax.ShapeDtypeStruct((nb * KVH, nsp), F32)],
        compiler_params=_params(1),
        name="nsa_select_decode",
    )(q_bd, kc_tm, vc_tm)


def _nsa_attn_decode_kernel(*refs, n_pages, n_prev):
    q_ref, oc_ref, sel_ref, new_ref, newc_ref, g_ref = refs[1:7]
    k_pages = refs[7:7 + n_pages]
    v_pages = refs[7 + n_pages:7 + 2 * n_pages]
    pos = 7 + 2 * n_pages
    kwin_ref, vwin_ref = refs[pos:pos + 2]
    o_ref, kwin_out, vwin_out = refs[pos + 2 + n_prev:pos + 5 + n_prev]
    q = q_ref[0]
    qf = q.astype(F32)
    sel4 = sel_ref[0]
    sel_rows = jnp.concatenate([jnp.broadcast_to(sel4[k:k + 1], (GQ, sel4.shape[1])) for k in range(KVH)], axis=0)
    new = new_ref[0]
    first_half = lax.broadcasted_iota(jnp.int32, (1, PAGE_SIZE), 1) < SEL_BLK

    scores, masks = [], []
    for p in range(n_pages):
        scores.append(_dot(q, k_pages[p][0, 0].astype(MXU_DTYPE)))
        masks.append(jnp.where(first_half, sel_rows[:, 2 * p:2 * p + 1], sel_rows[:, 2 * p + 1:2 * p + 2]) > 0.5)
    own = 2 * n_pages
    s_new = jnp.sum(qf * new[0:1, :].astype(MXU_DTYPE).astype(F32), axis=-1, keepdims=True)
    ok_new = sel_rows[:, own:own + 1] > 0.5
    s_new = jnp.where(ok_new, s_new, NEG)
    m = s_new
    for s, ok in zip(scores, masks):
        m = jnp.maximum(m, jnp.max(jnp.where(ok, s, NEG), axis=-1, keepdims=True))
    e_new = jnp.where(ok_new, jnp.exp2(s_new - m), 0.0)
    den = e_new
    acc = e_new * new[1:2, :].astype(MXU_DTYPE).astype(F32)
    for p, (s, ok) in enumerate(zip(scores, masks)):
        e = jnp.where(ok, jnp.exp2(s - m), 0.0)
        den = den + jnp.sum(e, axis=-1, keepdims=True)
        acc = acc + _dot_nt(e.astype(MXU_DTYPE), v_pages[p][0, 0].astype(MXU_DTYPE))
    o_s = _block_diag_fold(acc * (1.0 / den))

    kwin, vwin = kwin_ref[0, 0], vwin_ref[0, 0]
    s_w = _dot(q, kwin.astype(MXU_DTYPE))
    s_wn = jnp.sum(qf * new[2:3, :].astype(MXU_DTYPE).astype(F32), axis=-1, keepdims=True)
    m_w = jnp.maximum(jnp.max(s_w, axis=-1, keepdims=True), s_wn)
    e_w, e_wn = jnp.exp2(s_w - m_w), jnp.exp2(s_wn - m_w)
    den_w = jnp.sum(e_w, axis=-1, keepdims=True) + e_wn
    acc_w = _dot_nt(e_w.astype(MXU_DTYPE), vwin.astype(MXU_DTYPE)) + e_wn * new[3:4, :].astype(MXU_DTYPE).astype(F32)
    o_w = _block_diag_fold(acc_w * (1.0 / den_w))

    g = g_ref[0]
    o_ref[0] = g[:, 0:1] * oc_ref[0] + g[:, 1:2] * o_s + g[:, 2:3] * o_w

    wb = kwin.shape[1]
    last = lax.broadcasted_iota(jnp.int32, (1, wb), 1) == wb - 1
    kwin_out[0, 0] = jnp.where(last, newc_ref[0, 0], pltpu.roll(kwin, wb - 1, 1))
    vwin_out[0, 0] = jnp.where(last, newc_ref[0, 1], pltpu.roll(vwin, wb - 1, 1))


def _nsa_attn_decode(page_table, q_bd, o_c, sel, new_rows, new_cols, gates, pool_k, pool_v, win_k, win_v,
                     li, prev):
    nb, n_pages = page_table.shape
    n_layers, _, _, wb = win_k.shape
    nsp = sel.shape[-1]
    per_b = lambda shape: pl.BlockSpec((1,) + shape, lambda b, pt: (b,) + (0,) * len(shape))
    page = lambda p: pl.BlockSpec((1, 1, KV_W, PAGE_SIZE), lambda b, pt: (li, pt[b, p], 0, 0))
    win = pl.BlockSpec((1, 1, KV_W, wb), lambda b, pt: (li, b, 0, 0))
    kern = functools.partial(_nsa_attn_decode_kernel, n_pages=n_pages, n_prev=len(prev))
    n_in = 7 + 2 * n_pages + 2
    win_shape = jax.ShapeDtypeStruct((n_layers, nb, KV_W, wb), F32)
    return pl.pallas_call(
        kern,
        grid_spec=pltpu.PrefetchScalarGridSpec(
            num_scalar_prefetch=1,
            grid=(nb,),
            in_specs=([per_b((HQ, KV_W)), per_b((HQ, DH)), per_b((KVH, nsp)), per_b((4, KV_W)),
                       per_b((2, KV_W, 1)), per_b((HQ, 3))]
                      + [page(p) for p in range(n_pages)] + [page(p) for p in range(n_pages)]
                      + [win, win] + _alias_specs(prev)),
            out_specs=[per_b((HQ, DH)), win, win],
        ),
        out_shape=[jax.ShapeDtypeStruct((nb, HQ, DH), F32), win_shape, win_shape],
        input_output_aliases={n_in + j: 1 + j for j in range(len(prev))},
        compiler_params=_params(1),
        name="nsa_attn_decode",
    )(page_table, q_bd, o_c, sel.reshape(nb, KVH, nsp), new_rows, new_cols, gates,
      *([pool_k] * n_pages), *([pool_v] * n_pages), win_k, win_v, *prev)


def _outproj_ln_kernel(x_ref, o_ref, z_ref, w_ref, g_ref, b_ref, y_ref, *, alpha):
    z = z_ref[...]
    a = o_ref[...] * (z * (1.0 / (1.0 + jnp.exp(-z))))
    h = alpha * x_ref[...] + _dot(a.astype(MXU_DTYPE), w_ref[...])
    d = h - jnp.mean(h, axis=-1, keepdims=True)
    var = jnp.mean(d * d, axis=-1, keepdims=True)
    y_ref[...] = d * lax.rsqrt(var + LN_EPS) * g_ref[...] + b_ref[...]


def _outproj_ln(x2d, o2d, z2d, w, ln_g, ln_b, alpha, tm):
    m, d = x2d.shape
    wdt = o2d.shape[1]
    row = lambda n: pl.BlockSpec((tm, n), lambda i: (i, 0))
    fixed = lambda shape: pl.BlockSpec(shape, lambda i: (0, 0))
    return pl.pallas_call(
        functools.partial(_outproj_ln_kernel, alpha=alpha),
        grid=(m // tm,),
        in_specs=[row(d), row(wdt), row(wdt), fixed((wdt, d)), fixed((1, d)), fixed((1, d))],
        out_specs=row(d),
        out_shape=jax.ShapeDtypeStruct((m, d), F32),
        compiler_params=_params(1),
        name="outproj_ln",
    )(x2d, o2d, z2d, w, ln_g.reshape(1, d), ln_b.reshape(1, d))


def _ret_inproj_kernel(x_ref, w_ref, c_ref, sa_ref, sb_ref, q_ref, k_ref, v_ref, z_ref, *, rdk):
    xb = x_ref[...].astype(MXU_DTYPE)
    c, sa, sb = c_ref[...], sa_ref[...], sb_ref[...]
    qk_w = RH * rdk
    col = 0
    for dst, scale in ((q_ref, 1.0), (k_ref, rdk ** -0.5)):
        for ch in range(qk_w // 512):
            a = _rotate(_dot(xb, w_ref[:, col:col + 512]), c, sa, sb, 1)
            dst[:, ch * 512:(ch + 1) * 512] = a * scale
            col += 512
    for dst in (v_ref, z_ref):
        for ch in range(2 * qk_w // 512):
            dst[:, ch * 512:(ch + 1) * 512] = _dot(xb, w_ref[:, col:col + 512])
            col += 512


def _ret_inproj(x2d, w, tabs, t, tm):
    m, d = x2d.shape
    nt = t // tm
    rdk = d // RH
    row = lambda n: pl.BlockSpec((tm, n), lambda i: (i, 0))
    tab = pl.BlockSpec((tm, rdk), lambda i: (i % nt, 0))
    return pl.pallas_call(
        functools.partial(_ret_inproj_kernel, rdk=rdk),
        grid=(m // tm,),
        in_specs=[row(d), pl.BlockSpec(w.shape, lambda i: (0, 0)), tab, tab, tab],
        out_specs=[row(d), row(d), row(2 * d), row(2 * d)],
        out_shape=[jax.ShapeDtypeStruct((m, d), F32), jax.ShapeDtypeStruct((m, d), F32),
                   jax.ShapeDtypeStruct((m, 2 * d), F32), jax.ShapeDtypeStruct((m, 2 * d), F32)],
        compiler_params=_params(1),
        name="ret_inproj",
    )(x2d, w, *tabs)


def _group_norm(o, gain):
    d = o - jnp.mean(o, axis=-1, keepdims=True)
    return d * lax.rsqrt(jnp.mean(d * d, axis=-1, keepdims=True) + LN_EPS) * gain


def _ret_prompt_kernel(q_ref, k_ref, v_ref, dm_ref, dec_ref, gn_ref, o_ref, s_ref, s_sc):
    ci = pl.program_id(2)

    @pl.when(ci == 0)
    def _():
        s_sc[...] = jnp.zeros(s_sc.shape, F32)

    q, k, v = q_ref[0], k_ref[0], v_ref[0].astype(MXU_DTYPE)
    dec = dec_ref[0]
    s = s_sc[...]
    inner = _dot_nt(q.astype(MXU_DTYPE), k.astype(MXU_DTYPE)) * dm_ref[0]
    o = _dot(inner.astype(MXU_DTYPE), v) + _dot((q * dec[:, 0:1]).astype(MXU_DTYPE), s.astype(MXU_DTYPE))
    s_new = s * dec[0:1, 2:3] + _dot_tn((k * dec[:, 1:2]).astype(MXU_DTYPE), v)
    s_sc[...] = s_new
    o_ref[0] = _group_norm(o, gn_ref[...])

    @pl.when(ci == pl.num_programs(2) - 1)
    def _():
        s_ref[0, 0] = s_new


def _ret_prompt(q, k, v, dmask, dec, gn_g):
    nb, t, qk_w = q.shape
    rdk = qk_w // RH
    rdv = v.shape[2] // RH
    c = dmask.shape[1]
    tok = lambda n: pl.BlockSpec((1, c, n), lambda b, h, i: (b, i, h))
    per_h = lambda shape: pl.BlockSpec((1,) + shape, lambda b, h, i: (h, 0, 0))
    return pl.pallas_call(
        _ret_prompt_kernel,
        grid=(nb, RH, t // c),
        in_specs=[tok(rdk), tok(rdk), tok(rdv), per_h((c, c)), per_h((c, 8)),
                  pl.BlockSpec((1, rdv), lambda b, h, i: (0, h))],
        out_specs=[tok(rdv), pl.BlockSpec((1, 1, rdk, rdv), lambda b, h, i: (b, h, 0, 0))],
        out_shape=[jax.ShapeDtypeStruct((nb, t, RH * rdv), F32),
                   jax.ShapeDtypeStruct((nb, RH, rdk, rdv), F32)],
        scratch_shapes=[pltpu.VMEM((rdk, rdv), F32)],
        compiler_params=_params(3),
        name="ret_prompt",
    )(q, k, v, dmask, dec, gn_g.reshape(1, RH * rdv))


def _ret_decode_kernel(dec_ref, q_ref, k_ref, v_ref, s0_ref, gn_ref, *rest, rdk, rdv):
    o_ref, s_ref = rest[-2:]
    pad = LANES
    first_row = lax.broadcasted_iota(jnp.int32, (pad, 1), 0) == 0
    for h in range(RH):
        q = q_ref[0, :, h * rdk:(h + 1) * rdk]
        k = k_ref[0, :, h * rdk:(h + 1) * rdk]
        v = v_ref[0, :, h * rdv:(h + 1) * rdv]
        dmask, q_dec, k_dec, c_dec = dec_ref[h, 0], dec_ref[h, 1], dec_ref[h, 2], dec_ref[h, 3]
        s0 = s0_ref[0, 0, h]
        k_pad = jnp.where(first_row, jnp.broadcast_to(k * k_dec, (pad, rdk)), 0.0).astype(MXU_DTYPE)
        v_pad = jnp.broadcast_to(v, (pad, rdv)).astype(MXU_DTYPE)
        s_ref[0, 0, h] = s0 * c_dec + _dot_tn(k_pad, v_pad)
        qk = jnp.sum(q.astype(MXU_DTYPE).astype(F32) * k.astype(MXU_DTYPE).astype(F32), axis=-1, keepdims=True)
        inner = (qk * dmask).astype(MXU_DTYPE).astype(F32)
        q_pad = jnp.broadcast_to(q * q_dec, (8, rdk)).astype(MXU_DTYPE)
        o = inner * v.astype(MXU_DTYPE).astype(F32) + _dot(q_pad, s0.astype(MXU_DTYPE))[0:1, :]
        o_ref[0, :, h * rdv:(h + 1) * rdv] = _group_norm(o, gn_ref[:, h * rdv:(h + 1) * rdv])


def _ret_decode(q, k, v, state, li, dec, gn_g, prev):
    nb, qk_w = q.shape
    rdk, rdv = qk_w // RH, v.shape[1] // RH
    vec = lambda n: pl.BlockSpec((1, 1, n), lambda b: (b, 0, 0))
    st_spec = pl.BlockSpec((1, 1, RH, rdk, rdv), lambda b: (li, b, 0, 0, 0))
    return pl.pallas_call(
        functools.partial(_ret_decode_kernel, rdk=rdk, rdv=rdv),
        grid=(nb,),
        in_specs=[pl.BlockSpec(memory_space=pltpu.SMEM), vec(qk_w), vec(qk_w), vec(RH * rdv), st_spec,
                  pl.BlockSpec((1, RH * rdv), lambda b: (0, 0))] + _alias_specs(prev),
        out_specs=[vec(RH * rdv), st_spec],
        out_shape=[jax.ShapeDtypeStruct((nb, 1, RH * rdv), F32), jax.ShapeDtypeStruct(state.shape, F32)],
        input_output_aliases={6 + j: 1 + j for j in range(len(prev))},
        compiler_params=_params(1),
        name="ret_decode",
    )(dec, q.reshape(nb, 1, qk_w), k.reshape(nb, 1, qk_w), v.reshape(nb, 1, RH * rdv), state,
      gn_g.reshape(1, RH * rdv), *prev)


def _rope_tables(pos):
    half = ROT_DIM // 2
    inv = ROPE_THETA ** (-jnp.arange(half, dtype=F32) / half)
    ang = pos.astype(F32)[:, None] * inv[None, :]
    lane = jnp.arange(LANES) % DH
    cos, sin = jnp.cos(ang)[:, lane % half], jnp.sin(ang)[:, lane % half]
    c = jnp.where(lane < ROT_DIM, cos, 1.0)
    sa = jnp.where((lane >= half) & (lane < ROT_DIM), sin, 0.0)
    sb = jnp.where(lane < half, -sin, 0.0)
    return c, sa, sb


def _xpos_tables(pos, rdk):
    half = rdk // 2
    inv = 1.0 / (XPOS_BASE ** jnp.linspace(0.0, 1.0, half, dtype=F32))
    ang = pos.astype(F32)[:, None] * inv[None, :]
    lane = jnp.arange(rdk)
    cos, sin = jnp.cos(ang)[:, lane // 2], jnp.sin(ang)[:, lane // 2]
    sa = jnp.where(lane % 2 == 1, sin, 0.0)
    sb = jnp.where(lane % 2 == 0, -sin, 0.0)
    return cos, sa, sb


def _decay_tables(c):
    log_g = jnp.log1p(-jnp.power(2.0, -5.0 - jnp.arange(RH, dtype=F32)))
    i = jnp.arange(c, dtype=F32)
    diff = i[:, None] - i[None, :]
    dmask = jnp.where(diff >= 0, jnp.exp(jnp.maximum(diff, 0.0)[None] * log_g[:, None, None]), 0.0)
    q_dec = jnp.exp((i + 1.0)[None, :] * log_g[:, None])
    k_dec = jnp.exp((c - 1.0 - i)[None, :] * log_g[:, None])
    c_dec = jnp.exp(c * log_g)
    return dmask, q_dec, k_dec, c_dec


def _nsa_w_in_layout(w):
    kv_end = NSA_W + 6 * KV_W
    gates = jnp.pad(w[:, kv_end:kv_end + 3 * HQ], ((0, 0), (0, GATE_PAD - 3 * HQ)))
    return jnp.concatenate([w[:, :kv_end], gates, w[:, kv_end + 3 * HQ:]], axis=1).astype(MXU_DTYPE)


def _channel_major(cache):
    return jnp.moveaxis(cache, -3, -1).reshape(cache.shape[:-3] + (KV_W, cache.shape[-3]))


def _token_major(slab):
    return jnp.moveaxis(slab.reshape(slab.shape[:-2] + (KVH, DH, slab.shape[-1])), -1, -3)


def _row_tile(m):
    for tm in (256, 128):
        if m % tm == 0:
            return tm
    raise ValueError(f"row count {m} is not a multiple of {LANES}")


def _nsa_layer(xp, xs, li, n_layers, page_table, pools, wins, w_in, w_out, cmp_k, cmp_v, ln_g, ln_b, alpha,
               prev_p, prev_s, prev_win):
    nb, t, d = xp.shape
    nd = xs.shape[0]
    past = page_table.shape[1] * PAGE_SIZE
    assert xs.shape[1] == 1 and t % SEL_TILE == 0 and t >= WIN_TILE
    w_in = _nsa_w_in_layout(w_in)
    w_out = w_out.astype(MXU_DTYPE)

    xp2 = xp.reshape(nb * t, d)
    tm = _row_tile(t)
    q_hm, kv, khm, v_t, g_t, z, *cache_p = _nsa_inproj(xp2, w_in, _rope_tables(jnp.arange(t)), nb, t, tm,
                                                        li, n_layers, prev_p)
    kv = kv.reshape(6, nb, t, KV_W)
    kc_hm = _compress_prompt(kv[0], cmp_k, "head_major")
    vc_t = _compress_prompt(kv[1], cmp_v, "channel_major")
    o = _nsa_attn_prompt(q_hm, kc_hm, vc_t, khm, v_t, g_t)
    yp = _outproj_ln(xp2, o.reshape(nb * t, NSA_W), z, w_out, ln_g, ln_b, alpha, tm).reshape(nb, t, d)

    xs2 = xs.reshape(nd, d)
    tms = _row_tile(nd)
    pos_s = jnp.full((nd,), past, jnp.int32)
    q_s, kv_s, _, _, g_s, z_s, *cache_s = _nsa_inproj(xs2, w_in, _rope_tables(pos_s), 1, nd, tms,
                                                      li, n_layers, prev_s)
    q_tok = q_s[0].transpose(1, 0, 2).astype(F32).reshape(nd, KVH, GQ, 1, DH)
    q_bd = (q_tok * jnp.eye(KVH, dtype=F32)[None, :, None, :, None]).reshape(nd, HQ, KV_W).astype(MXU_DTYPE)
    kc_tm = _compress_paged(pools[0], li, page_table, cmp_k)
    vc_tm = _compress_paged(pools[1], li, page_table, cmp_v)
    o_c, sel = _nsa_select(q_bd, kc_tm, vc_tm, past, min(16, nd))
    new_rows = jnp.stack([kv_s[2], kv_s[3], kv_s[4], kv_s[5]], axis=1)
    new_cols = jnp.stack([kv_s[4], kv_s[5]], axis=1)[..., None]
    gates_s = g_s[0, :3 * HQ].reshape(3, HQ, nd).transpose(2, 1, 0)
    o_s, *win_new = _nsa_attn_decode(page_table, q_bd, o_c, sel, new_rows, new_cols, gates_s,
                                     pools[2], pools[3], wins[0], wins[1], li, prev_win)
    ys = _outproj_ln(xs2, o_s.reshape(nd, NSA_W), z_s, w_out, ln_g, ln_b, alpha, tms).reshape(nd, 1, d)
    return yp, ys, cache_p, cache_s, win_new


def _ret_layer(xp, xs, li, state, past, w_in, gn_g, w_out, ln_g, ln_b, alpha, prev_state):
    nb, t, d = xp.shape
    nd = xs.shape[0]
    rdk = d // RH
    assert xs.shape[1] == 1 and t % RET_CHUNK == 0
    w_in = w_in.astype(MXU_DTYPE)
    w_out = w_out.astype(MXU_DTYPE)

    xp2 = xp.reshape(nb * t, d)
    tm = _row_tile(t)
    q, k, v, z = _ret_inproj(xp2, w_in, _xpos_tables(jnp.arange(t), rdk), t, tm)
    dmask, q_dec, k_dec, c_dec = _decay_tables(RET_CHUNK)
    dec = jnp.stack([q_dec, k_dec, jnp.broadcast_to(c_dec[:, None], q_dec.shape)]
                    + [jnp.zeros_like(q_dec)] * 5, axis=-1)
    o, sp = _ret_prompt(q.reshape(nb, t, d), k.reshape(nb, t, d), v.reshape(nb, t, 2 * d), dmask, dec, gn_g)
    yp = _outproj_ln(xp2, o.reshape(nb * t, 2 * d), z, w_out, ln_g, ln_b, alpha, tm).reshape(nb, t, d)

    xs2 = xs.reshape(nd, d)
    tms = _row_tile(nd)
    q, k, v, z = _ret_inproj(xs2, w_in, _xpos_tables(jnp.full((nd,), past, jnp.int32), rdk), nd, tms)
    dmask, q_dec, k_dec, c_dec = _decay_tables(1)
    dec = jnp.stack([dmask[:, 0, 0], q_dec[:, 0], k_dec[:, 0], c_dec], axis=-1)
    o, ss = _ret_decode(q, k, v, state, li, dec, gn_g, prev_state)
    ys = _outproj_ln(xs2, o.reshape(nd, 2 * d), z, w_out, ln_g, ln_b, alpha, tms).reshape(nd, 1, d)
    return yp, ys, sp, ss


def kernel(x_prompt, x_sample, cache_k_cmp, cache_v_cmp, cache_k_sel, cache_v_sel, cache_k_win, cache_v_win,
           state_ret, page_table, nsa_w_in, nsa_w_out, nsa_pe_k, nsa_w1_k, nsa_w2_k, nsa_pe_v, nsa_w1_v,
           nsa_w2_v, ret_w_in, ret_gn_g, ret_w_out, ln_g, ln_b):
    depth = ln_g.shape[0]
    alpha = (2.0 * depth) ** 0.25
    n_nsa = nsa_w_in.shape[0]
    past = page_table.shape[1] * PAGE_SIZE
    t = x_prompt.shape[1]
    pools = [_channel_major(c) for c in (cache_k_cmp, cache_v_cmp, cache_k_sel, cache_v_sel)]
    wins = [_channel_major(c) for c in (cache_k_win, cache_v_win)]
    xp, xs = x_prompt, x_sample
    cache_p, cache_s, win_new, ret_p, ret_s = [], [], [], [], []
    for i in range(depth):
        li = i // 2
        if i % 2 == 0:
            cmp_k = _compress_weights(nsa_pe_k[li], nsa_w1_k[li], nsa_w2_k[li])
            cmp_v = _compress_weights(nsa_pe_v[li], nsa_w1_v[li], nsa_w2_v[li])
            xp, xs, cache_p, cache_s, win_new = _nsa_layer(
                xp, xs, li, n_nsa, page_table, pools, wins, nsa_w_in[li], nsa_w_out[li], cmp_k, cmp_v,
                ln_g[i], ln_b[i], alpha, cache_p, cache_s, win_new)
        else:
            xp, xs, sp, ss = _ret_layer(xp, xs, li, state_ret, past, ret_w_in[li], ret_gn_g[li], ret_w_out[li],
                                        ln_g[i], ln_b[i], alpha, ret_s)
            ret_p.append(sp)
            ret_s = [ss]
    wb = min(WINDOW, t)
    out_p = [_token_major(c) for c in cache_p[:4]] + [_token_major(c[..., t - wb:]) for c in cache_p[4:]]
    out_s = [jnp.moveaxis(_token_major(c), 2, 1) for c in cache_s[:4]] + [_token_major(w) for w in win_new]
    return (xp, xs,
            out_p[0], out_s[0], out_p[1], out_s[1], out_p[2], out_s[2], out_p[3], out_s[3],
            out_p[4], out_s[4], out_p[5], out_s[5],
            jnp.stack(ret_p), ret_s[0])
```

```python
import functools

import jax
import jax.numpy as jnp
from jax import lax
from jax.experimental import pallas as pl
from jax.experimental.pallas import tpu as pltpu

F32 = jnp.float32
MXU_DTYPE = jnp.bfloat16

HQ, KVH, DH = 32, 4, 64
GQ = HQ // KVH
NSA_W, KV_W = HQ * DH, KVH * DH
ROT_DIM, ROPE_THETA = DH // 4, 500000.0
CMP_LEN, CMP_STRIDE, CMP_HID = 32, 16, 2 * DH
SEL_BLK, N_SEL, WINDOW, QBLK = 64, 16, 512, 128
FORCE_BONUS, NEG = 1000.0, -1.0e30
RH, RET_CHUNK, XPOS_BASE = 4, 128, 10000.0
PAGE_SIZE, LN_EPS = 128, 1e-5

LANES = 128
VMEM_LIMIT_BYTES = 56 * 1024 * 1024
CMP_GROUP = 2048
GATE_PAD = LANES
KEY_TILE = 2 * SEL_BLK
SEL_TILE = 4 * KEY_TILE
WIN_TILE = WINDOW + QBLK
ONES_ROWS = 16
LOG2E = 1.4426950408889634
REMOVED = -3.0e38
RUNNING_MAX_INIT = -1.0e29


def _params(n_axes):
    return pltpu.CompilerParams(dimension_semantics=("arbitrary",) * n_axes,
                                vmem_limit_bytes=VMEM_LIMIT_BYTES)


def _dot(a, b):
    return jnp.dot(a, b, preferred_element_type=F32)


def _dot_nt(a, b):
    return lax.dot_general(a, b, (((1,), (1,)), ((), ())), preferred_element_type=F32)


def _dot_tn(a, b):
    return lax.dot_general(a, b, (((0,), (0,)), ((), ())), preferred_element_type=F32)


def _lane_tile(t, reps):
    return t if reps == 1 else jnp.concatenate([t] * reps, axis=1)


def _rotate(a, c, sa, sb, shift):
    n = a.shape[1]
    reps = n // c.shape[1]
    c, sa, sb = _lane_tile(c, reps), _lane_tile(sa, reps), _lane_tile(sb, reps)
    return a * c + pltpu.roll(a, shift, 1) * sa + pltpu.roll(a, n - shift, 1) * sb


def _masked_softmax(s, mask):
    s = jnp.where(mask, s, NEG)
    m = jnp.max(s, axis=-1, keepdims=True)
    e = jnp.where(mask, jnp.exp2(s - m), 0.0)
    d = jnp.sum(e, axis=-1, keepdims=True)
    return e * (1.0 / jnp.where(d > 0.0, d, 1.0))


def _split3(x):
    hi = x.astype(MXU_DTYPE)
    r = x - hi.astype(F32)
    mid = r.astype(MXU_DTYPE)
    lo = (r - mid.astype(F32)).astype(MXU_DTYPE)
    return hi, mid, lo


def _topk_mask(score, idx, n_pick, axis):
    big = jnp.float32(score.shape[axis])
    sel = jnp.zeros(score.shape, F32)
    for _ in range(n_pick):
        m = jnp.max(score, axis=axis, keepdims=True)
        first = jnp.min(jnp.where(score == m, idx, big), axis=axis, keepdims=True)
        pick = idx == first
        sel = jnp.where(pick, 1.0, sel)
        score = jnp.where(pick, REMOVED, score)
    return sel


def _overlap(c_idx, s_idx):
    cs, ss = c_idx * CMP_STRIDE, s_idx * SEL_BLK
    ov = jnp.minimum(cs + CMP_LEN, ss + SEL_BLK) - jnp.maximum(cs, ss)
    return jnp.maximum(ov, 0).astype(F32) * (1.0 / CMP_LEN)


def _alias_specs(prev):
    return [pl.BlockSpec(memory_space=pl.ANY)] * len(prev)


def _nsa_inproj_kernel(x_ref, w_ref, c_ref, sa_ref, sb_ref, *rest):
    q_ref, kv_ref, khm_ref, vt_ref, gt_ref, z_ref = rest[-12:-6]
    cache_refs = rest[-6:]
    xb = x_ref[...].astype(MXU_DTYPE)
    c, sa, sb = c_ref[...], sa_ref[...], sb_ref[...]
    half = ROT_DIM // 2
    col = 0
    for ch in range(NSA_W // 512):
        a = _dot(xb, w_ref[:, col:col + 512])
        a = _rotate(a, c, sa, sb, half) * (DH ** -0.5 * LOG2E)
        for h in range(512 // DH):
            q_ref[0, ch * (512 // DH) + h] = a[:, h * DH:(h + 1) * DH].astype(q_ref.dtype)
        col += 512
    for j in range(6):
        a = _dot(xb, w_ref[:, col:col + KV_W])
        if j % 2 == 0:
            a = _rotate(a, c, sa, sb, half)
        kv_ref[j] = a
        a_t = a.T
        cache_refs[j][0, 0] = a_t
        if j in (2, 4):
            for h in range(KVH):
                khm_ref[j // 2 - 1, 0, h] = a[:, h * DH:(h + 1) * DH].astype(khm_ref.dtype)
        if j in (3, 5):
            vt_ref[j // 2 - 1, 0] = a_t.astype(vt_ref.dtype)
        col += KV_W
    a = _dot(xb, w_ref[:, col:col + GATE_PAD])
    gt_ref[0] = (1.0 / (1.0 + jnp.exp(-a))).T
    col += GATE_PAD
    for ch in range(NSA_W // 512):
        z_ref[:, ch * 512:(ch + 1) * 512] = _dot(xb, w_ref[:, col:col + 512])
        col += 512


def _nsa_inproj(x2d, w, tabs, nb, t, tm, li, n_layers, prev):
    m, d = x2d.shape
    nt = t // tm
    n_cols = w.shape[1]
    row = lambda i: (i, 0)
    tab = pl.BlockSpec((tm, LANES), lambda i: (i % nt, 0))
    cache_spec = pl.BlockSpec((1, 1, KV_W, tm), lambda i: (li, i // nt, 0, i % nt))
    cache_shape = jax.ShapeDtypeStruct((n_layers, nb, KV_W, t), F32)
    n_in = 5
    return pl.pallas_call(
        _nsa_inproj_kernel,
        grid=(m // tm,),
        in_specs=[pl.BlockSpec((tm, d), row), pl.BlockSpec((d, n_cols), lambda i: (0, 0)), tab, tab, tab]
        + _alias_specs(prev),
        out_specs=[
            pl.BlockSpec((1, HQ, tm, DH), lambda i: (i // nt, 0, i % nt, 0)),
            pl.BlockSpec((6, tm, KV_W), lambda i: (0, i, 0)),
            pl.BlockSpec((2, 1, KVH, tm, DH), lambda i: (0, i // nt, 0, i % nt, 0)),
            pl.BlockSpec((2, 1, KV_W, tm), lambda i: (0, i // nt, 0, i % nt)),
            pl.BlockSpec((1, GATE_PAD, tm), lambda i: (i // nt, 0, i % nt)),
            pl.BlockSpec((tm, NSA_W), row),
        ] + [cache_spec] * 6,
        out_shape=[
            jax.ShapeDtypeStruct((nb, HQ, t, DH), MXU_DTYPE),
            jax.ShapeDtypeStruct((6, m, KV_W), F32),
            jax.ShapeDtypeStruct((2, nb, KVH, t, DH), MXU_DTYPE),
            jax.ShapeDtypeStruct((2, nb, KV_W, t), MXU_DTYPE),
            jax.ShapeDtypeStruct((nb, GATE_PAD, t), F32),
            jax.ShapeDtypeStruct((m, NSA_W), F32),
        ] + [cache_shape] * 6,
        input_output_aliases={n_in + j: 6 + j for j in range(len(prev))},
        compiler_params=_params(1),
        name="nsa_inproj",
    )(x2d, w, *tabs, *prev)


def _compress_kernel(*refs, n_prefetch, n_pages, layout, n_chunks):
    split = KV_W // LANES
    pos = n_prefetch
    if n_pages:
        page_refs = refs[pos:pos + n_pages]
        pos += n_pages
    else:
        tok_ref, halo_ref = refs[pos:pos + 2]
        pos += 2
    wpair_ref, pe_ref, w1_ref, w2_ref, out_ref, r_ref = refs[pos:pos + 6]
    r_ref[:, n_chunks:, :] = jnp.zeros((split, 8, CMP_STRIDE * LANES), F32)
    if n_pages:
        tok_sc = refs[pos + 6]
        for p, ref in enumerate(page_refs):
            page = ref[0, 0].T
            for s in range(split):
                tok_sc[s, p * PAGE_SIZE:(p + 1) * PAGE_SIZE, :] = page[:, s * LANES:(s + 1) * LANES]
        for s in range(split):
            for l in range(CMP_STRIDE):
                r_ref[s, 0:n_chunks, l * LANES:(l + 1) * LANES] = tok_sc[s, pl.ds(l, n_chunks, stride=CMP_STRIDE), :]
    else:
        for s in range(split):
            for l in range(CMP_STRIDE):
                lanes = slice(l * LANES, (l + 1) * LANES)
                j = l * split + s
                r_ref[s, 0:n_chunks, lanes] = tok_ref[0, pl.ds(j, n_chunks, stride=CMP_STRIDE * split), :]
                r_ref[s, n_chunks:n_chunks + 1, lanes] = halo_ref[0, j:j + 1, :]
    heads_per_tile = LANES // DH
    ab = [_dot(r_ref[s].astype(MXU_DTYPE), wpair_ref[...]) for s in range(split)]
    pe8 = jnp.broadcast_to(pe_ref[...], (8, CMP_LEN * DH)).astype(MXU_DTYPE)
    bias = _dot(pe8, w1_ref[...])[0:1, :]
    w2 = w2_ref[...]
    toks = []
    for h in range(KVH):
        base = (h % heads_per_tile) * 2 * CMP_HID
        first_half = ab[h // heads_per_tile][0:n_chunks, base:base + CMP_HID]
        second_half = ab[h // heads_per_tile][1:n_chunks + 1, base + CMP_HID:base + 2 * CMP_HID]
        u = first_half + second_half + bias
        hid = 0.5 * u * (1.0 + jnp.tanh(0.7978845608028654 * (u + 0.044715 * (u * u * u))))
        toks.append(_dot(hid.astype(MXU_DTYPE), w2))
    if layout == "head_major":
        for h in range(KVH):
            out_ref[0, h] = toks[h].astype(out_ref.dtype)
    elif layout == "token_major":
        out_ref[0] = jnp.concatenate(toks, axis=1).astype(out_ref.dtype)
    else:
        out_ref[0] = jnp.concatenate(toks, axis=1).T.astype(out_ref.dtype)


def _compress_weights(pe, w1, w2):
    heads = LANES // DH
    w1r = w1.reshape(2, CMP_STRIDE, DH, CMP_HID)
    eye = jnp.eye(heads, dtype=w1.dtype)
    wpair = jnp.einsum("sldn,hg->lhdgsn", w1r, eye).reshape(CMP_STRIDE * LANES, heads * 2 * CMP_HID)
    return (wpair.astype(MXU_DTYPE), pe.reshape(1, CMP_LEN * DH), w1.astype(MXU_DTYPE), w2.astype(MXU_DTYPE))


def _compress_prompt(k_tok, weights, layout):
    nb, t, _ = k_tok.shape
    gt = min(CMP_GROUP, t)
    n_chunks = gt // CMP_STRIDE
    ng = t // gt
    last_halo = t // CMP_STRIDE - 1
    wpair, pe, w1, w2 = weights
    split = KV_W // LANES
    k_tok = k_tok.reshape(nb, t * split, LANES)
    full = lambda a: pl.BlockSpec(a.shape, lambda b, g: (0,) * a.ndim)
    kern = functools.partial(_compress_kernel, n_prefetch=0, n_pages=0, layout=layout, n_chunks=n_chunks)
    if layout == "head_major":
        out_spec = pl.BlockSpec((1, KVH, n_chunks, DH), lambda b, g: (b, 0, g, 0))
        out_shape = jax.ShapeDtypeStruct((nb, KVH, t // CMP_STRIDE, DH), MXU_DTYPE)
    else:
        out_spec = pl.BlockSpec((1, KV_W, n_chunks), lambda b, g: (b, 0, g))
        out_shape = jax.ShapeDtypeStruct((nb, KV_W, t // CMP_STRIDE), MXU_DTYPE)
    return pl.pallas_call(
        kern,
        grid=(nb, ng),
        in_specs=[pl.BlockSpec((1, gt * split, LANES), lambda b, g: (b, g, 0)),
                  pl.BlockSpec((1, CMP_STRIDE * split, LANES),
                               lambda b, g: (b, jnp.minimum((g + 1) * n_chunks, last_halo), 0)),
                  full(wpair), full(pe), full(w1), full(w2)],
        out_specs=out_spec,
        out_shape=out_shape,
        scratch_shapes=[pltpu.VMEM((split, n_chunks + 8, CMP_STRIDE * LANES), F32)],
        compiler_params=_params(2),
        name="nsa_compress_prompt",
    )(k_tok, k_tok, wpair, pe, w1, w2)


def _compress_paged(pool, li, page_table, weights):
    nb, n_pages = page_table.shape
    n_chunks = n_pages * PAGE_SIZE // CMP_STRIDE
    wpair, pe, w1, w2 = weights
    split = KV_W // LANES
    full = lambda a: pl.BlockSpec(a.shape, lambda b, pt: (0,) * a.ndim)
    page = lambda p: pl.BlockSpec((1, 1, KV_W, PAGE_SIZE), lambda b, pt: (li, pt[b, p], 0, 0))
    kern = functools.partial(_compress_kernel, n_prefetch=1, n_pages=n_pages, layout="token_major",
                             n_chunks=n_chunks)
    return pl.pallas_call(
        kern,
        grid_spec=pltpu.PrefetchScalarGridSpec(
            num_scalar_prefetch=1,
            grid=(nb,),
            in_specs=[page(p) for p in range(n_pages)] + [full(wpair), full(pe), full(w1), full(w2)],
            out_specs=pl.BlockSpec((1, n_chunks, KV_W), lambda b, pt: (b, 0, 0)),
            scratch_shapes=[pltpu.VMEM((split, n_chunks + 8, CMP_STRIDE * LANES), F32),
                            pltpu.VMEM((split, n_pages * PAGE_SIZE, LANES), F32)],
        ),
        out_shape=jax.ShapeDtypeStruct((nb, n_chunks, KV_W), MXU_DTYPE),
        compiler_params=_params(1),
        name="nsa_compress_paged",
    )(page_table, *([pool] * n_pages), wpair, pe, w1, w2)


def _nsa_attn_prompt_kernel(q_ref, kc_ref, vct_ref, ks_ref, vst_ref, kw_ref, vwt_ref, gc_ref, gs_ref, gw_ref,
                            o_ref, selb_sc, bias_sc, ot_sc, m_sc, alpha_sc, acc_sc, p_sc, s_sc, *, seq_len):
    st = pl.program_id(2) * QBLK
    ncp = kc_ref.shape[2]
    ns = seq_len // SEL_BLK
    n_pairs = GQ // 2
    qpos = st + lax.broadcasted_iota(jnp.int32, (1, QBLK), 1)
    q_pairs = [q_ref[0, 2 * j:2 * j + 2].reshape(2 * QBLK, DH) for j in range(n_pairs)]
    both = lambda b: jnp.concatenate([b, b], axis=1)

    c_row = lax.broadcasted_iota(jnp.int32, (ncp, 1), 0)
    keep_c = (c_row * CMP_STRIDE + (CMP_LEN - 1) <= qpos).astype(F32)
    bias_c = both((1.0 - keep_c) * NEG)
    keep_c = both(keep_c)
    kc, vct = kc_ref[0, 0], vct_ref[0]
    k_row = lax.broadcasted_iota(jnp.int32, (KEY_TILE, 1), 0)
    ws = pl.multiple_of(jnp.maximum(st - WINDOW, 0), KEY_TILE)
    w_row = lax.broadcasted_iota(jnp.int32, (WIN_TILE, 1), 0)
    dist = qpos - (ws + w_row)
    bias_w = both(jnp.where((dist >= 0) & (dist <= WINDOW), 0.0, NEG))
    kw_tile = kw_ref[0, 0, 0, pl.ds(ws, WIN_TILE), :]
    vwt_tile = jnp.concatenate([vwt_ref[0, 0, :, pl.ds(ws, WIN_TILE)], jnp.ones((ONES_ROWS, WIN_TILE), MXU_DTYPE)],
                               axis=0)
    c_scores = [_dot_nt(kc, q_pairs[j]) for j in range(n_pairs)]
    w_scores = [_dot_nt(kw_tile, q_pairs[j]) for j in range(n_pairs)]
    ks_first = ks_ref[0, 0, 0, 0:SEL_TILE, :]
    for j in range(n_pairs):
        s_sc[j] = _dot_nt(ks_first, q_pairs[j])
    p_sum = jnp.zeros((ncp, QBLK), F32)
    c_probs, w_probs = [], []
    for j in range(n_pairs):
        s = c_scores[j] + bias_c
        e = jnp.exp2(s - jnp.max(s, axis=0, keepdims=True)) * keep_c
        den = jnp.sum(e, axis=0, keepdims=True)
        p = e * (1.0 / jnp.where(den > 0.0, den, 1.0))
        c_probs.append(p.astype(MXU_DTYPE))
        p_sum = p_sum + p[:, :QBLK] + p[:, QBLK:]
    for j in range(n_pairs):
        s = w_scores[j] + bias_w
        w_probs.append(jnp.exp2(s - jnp.max(s, axis=0, keepdims=True)).astype(MXU_DTYPE))
    o_c = [_dot(vct, p) for p in c_probs]
    o_w = [_dot(vwt_tile, p) for p in w_probs]
    o_w = [o[:DH] * (1.0 / o[DH:DH + 1]) for o in o_w]

    s_idx = lax.broadcasted_iota(jnp.int32, (ns, 1), 0)
    ov_t = _overlap(lax.broadcasted_iota(jnp.int32, (1, ncp), 1), s_idx).astype(MXU_DTYPE)
    imp_t = sum(_dot(ov_t, part) for part in _split3(p_sum))
    blk_q = qpos // SEL_BLK
    forced = (s_idx == 0) | (s_idx == blk_q) | (s_idx == blk_q - 1)
    valid = s_idx * SEL_BLK <= qpos
    score = jnp.where(valid, imp_t + FORCE_BONUS * forced.astype(F32), NEG)
    sel_t = _topk_mask(score, s_idx.astype(F32), min(N_SEL, ns), 0)
    selb_sc[...] = jnp.where(valid & (sel_t > 0.5), 0.0, NEG)

    def sel_bias(j, carry):
        k0 = pl.multiple_of(j * KEY_TILE, KEY_TILE)
        blocks = [jnp.broadcast_to(selb_sc[pl.ds(2 * j + i, 1), :], (SEL_BLK, QBLK)) for i in range(2)]
        bias_sc[pl.ds(k0, KEY_TILE), :] = jnp.where(k0 + k_row <= qpos, jnp.concatenate(blocks, axis=0), NEG)
        return carry

    n_tiles = (st + QBLK + SEL_TILE - 1) // SEL_TILE
    lax.fori_loop(0, n_tiles * (SEL_TILE // KEY_TILE), sel_bias, 0)

    m_sc[...] = jnp.full(m_sc.shape, RUNNING_MAX_INIT, F32)
    acc_sc[...] = jnp.zeros(acc_sc.shape, F32)
    alpha_sc[...] = jnp.ones(alpha_sc.shape, F32)
    p_sc[...] = jnp.zeros(p_sc.shape, MXU_DTYPE)
    ones_rows = jnp.ones((ONES_ROWS, SEL_TILE), MXU_DTYPE)

    def key_start(i):
        return pl.multiple_of(i * SEL_TILE, SEL_TILE)

    def fold(i):
        vt_tile = jnp.concatenate([vst_ref[0, 0, :, pl.ds(key_start(i), SEL_TILE)], ones_rows], axis=0)
        for j in range(n_pairs):
            acc_sc[j] = alpha_sc[j] * acc_sc[j] + _dot(vt_tile, p_sc[j])

    def step(i, carry):
        fold(jnp.maximum(i - 1, 0))
        bias = both(bias_sc[pl.ds(key_start(i), SEL_TILE), :])
        k_ahead = ks_ref[0, 0, 0, pl.ds(key_start(jnp.minimum(i + 1, n_tiles - 1)), SEL_TILE), :]
        for j in range(n_pairs):
            ahead = _dot_nt(k_ahead, q_ref[0, 2 * j:2 * j + 2].reshape(2 * QBLK, DH))
            s = s_sc[j] + bias
            m_old = m_sc[j]
            m_new = jnp.maximum(m_old, jnp.max(s, axis=0, keepdims=True))
            p_sc[j] = jnp.exp2(s - m_new).astype(MXU_DTYPE)
            alpha_sc[j] = jnp.exp2(m_old - m_new)
            m_sc[j] = m_new
            s_sc[j] = ahead
        return carry

    lax.fori_loop(0, n_tiles, step, 0)
    fold(n_tiles - 1)
    o_s = [acc_sc[j, :DH] * (1.0 / acc_sc[j, DH:DH + 1]) for j in range(n_pairs)]
    gc, gs, gw = gc_ref[0], gs_ref[0], gw_ref[0]
    for h in range(GQ):
        j, cols = h // 2, slice((h % 2) * QBLK, (h % 2 + 1) * QBLK)
        ot_sc[h * DH:(h + 1) * DH, :] = (gc[h:h + 1] * o_c[j][:, cols] + gs[h:h + 1] * o_s[j][:, cols]
                                         + gw[h:h + 1] * o_w[j][:, cols])
    o_ref[0] = ot_sc[...].T


def _nsa_attn_prompt(q_hm, kc_hm, vc_t, khm, v_t, gates_t):
    nb, _, t, _ = q_hm.shape
    ncp = kc_hm.shape[2]
    ns = t // SEL_BLK
    k_spec = lambda j: pl.BlockSpec((1, 1, 1, t, DH), lambda b, h, i: (j, b, h, 0, 0))
    vt_spec = lambda j: pl.BlockSpec((1, 1, DH, t), lambda b, h, i: (j, b, h, 0))
    gate_spec = lambda br: pl.BlockSpec((1, GQ, QBLK), lambda b, h, i: (b, br * KVH + h, i))
    kern = functools.partial(_nsa_attn_prompt_kernel, seq_len=t)
    return pl.pallas_call(
        kern,
        grid=(nb, KVH, t // QBLK),
        in_specs=[pl.BlockSpec((1, GQ, QBLK, DH), lambda b, h, i: (b, h, i, 0)),
                  pl.BlockSpec((1, 1, ncp, DH), lambda b, h, i: (b, h, 0, 0)),
                  pl.BlockSpec((1, DH, ncp), lambda b, h, i: (b, h, 0)),
                  k_spec(0), vt_spec(0), k_spec(1), vt_spec(1),
                  gate_spec(0), gate_spec(1), gate_spec(2)],
        out_specs=pl.BlockSpec((1, QBLK, GQ * DH), lambda b, h, i: (b, i, h)),
        out_shape=jax.ShapeDtypeStruct((nb, t, NSA_W), F32),
        scratch_shapes=[pltpu.VMEM((ns, QBLK), F32), pltpu.VMEM((t, QBLK), F32),
                        pltpu.VMEM((GQ * DH, QBLK), F32),
                        pltpu.VMEM((GQ // 2, 1, 2 * QBLK), F32), pltpu.VMEM((GQ // 2, 1, 2 * QBLK), F32),
                        pltpu.VMEM((GQ // 2, DH + ONES_ROWS, 2 * QBLK), F32),
                        pltpu.VMEM((GQ // 2, SEL_TILE, 2 * QBLK), MXU_DTYPE),
                        pltpu.VMEM((GQ // 2, SEL_TILE, 2 * QBLK), F32)],
        compiler_params=_params(3),
        name="nsa_attn_prompt",
    )(q_hm, kc_hm, vc_t, khm, v_t, khm, v_t, gates_t, gates_t, gates_t)


def _block_diag_fold(x):
    r_kv = lax.broadcasted_iota(jnp.int32, (HQ, KV_W), 0) // GQ
    l_kv = lax.broadcasted_iota(jnp.int32, (HQ, KV_W), 1) // DH
    x = jnp.where(r_kv == l_kv, x, 0.0)
    return sum(x[..., h * DH:(h + 1) * DH] for h in range(KVH))


def _nsa_select_kernel(q_ref, kc_ref, vc_ref, oc_ref, sel_ref, *, past):
    bb, ncp = kc_ref.shape[0], kc_ref.shape[1]
    nsp = sel_ref.shape[1]
    q = q_ref[...]
    s = jnp.einsum("bqd,bkd->bqk", q, kc_ref[...], preferred_element_type=F32)
    c_idx = lax.broadcasted_iota(jnp.int32, (1, 1, ncp), 2)
    p_c = _masked_softmax(s, c_idx * CMP_STRIDE + (CMP_LEN - 1) <= past)
    o_c = jnp.einsum("bqk,bkd->bqd", p_c.astype(MXU_DTYPE), vc_ref[...], preferred_element_type=F32)
    oc_ref[...] = _block_diag_fold(o_c)
    ov = _overlap(lax.broadcasted_iota(jnp.int32, (ncp, 1), 0),
                  lax.broadcasted_iota(jnp.int32, (1, nsp), 1)).astype(MXU_DTYPE)
    imp_rows = sum(_dot(part, ov) for part in _split3(p_c.reshape(bb * HQ, ncp)))
    imp = jnp.sum(imp_rows.reshape(bb * KVH, GQ, nsp), axis=1)
    s_idx = lax.broadcasted_iota(jnp.int32, (1, nsp), 1)
    blk_q = past // SEL_BLK
    forced = (s_idx == 0) | (s_idx == blk_q) | (s_idx == blk_q - 1)
    valid = s_idx * SEL_BLK <= past
    score = jnp.where(valid, imp + FORCE_BONUS * forced.astype(F32), NEG)
    n_blocks = past // SEL_BLK + 1
    sel = _topk_mask(score, s_idx.astype(F32), min(N_SEL, n_blocks), 1)
    sel_ref[...] = jnp.where(valid, sel, 0.0)


def _nsa_select(q_bd, kc_tm, vc_tm, past, bb):
    nb, ncp, _ = kc_tm.shape
    nsp = LANES
    assert past // SEL_BLK + 1 <= nsp
    kern = functools.partial(_nsa_select_kernel, past=past)
    blk = lambda shape: pl.BlockSpec(shape, lambda i: (i, 0, 0))
    return pl.pallas_call(
        kern,
        grid=(nb // bb,),
        in_specs=[blk((bb, HQ, KV_W)), blk((bb, ncp, KV_W)), blk((bb, ncp, KV_W))],
        out_specs=[blk((bb, HQ, DH)), pl.BlockSpec((bb * KVH, nsp), lambda i: (i, 0))],
        out_shape=[jax.ShapeDtypeStruct((nb, HQ, DH), F32), jax.ShapeDtypeStruct((nb * KVH, nsp), F32)],
        compiler_params=_params(1),
        name="nsa_select_decode",
    )(q_bd, kc_tm, vc_tm)


def _nsa_attn_decode_kernel(*refs, n_pages, n_prev):
    q_ref, oc_ref, sel_ref, new_ref, newc_ref, g_ref = refs[1:7]
    k_pages = refs[7:7 + n_pages]
    v_pages = refs[7 + n_pages:7 + 2 * n_pages]
    pos = 7 + 2 * n_pages
    kwin_ref, vwin_ref = refs[pos:pos + 2]
    o_ref, kwin_out, vwin_out = refs[pos + 2 + n_prev:pos + 5 + n_prev]
    q = q_ref[0]
    qf = q.astype(F32)
    sel4 = sel_ref[0]
    sel_rows = jnp.concatenate([jnp.broadcast_to(sel4[k:k + 1], (GQ, sel4.shape[1])) for k in range(KVH)], axis=0)
    new = new_ref[0]
    first_half = lax.broadcasted_iota(jnp.int32, (1, PAGE_SIZE), 1) < SEL_BLK

    scores, masks = [], []
    for p in range(n_pages):
        scores.append(_dot(q, k_pages[p][0, 0].astype(MXU_DTYPE)))
        masks.append(jnp.where(first_half, sel_rows[:, 2 * p:2 * p + 1], sel_rows[:, 2 * p + 1:2 * p + 2]) > 0.5)
    own = 2 * n_pages
    s_new = jnp.sum(qf * new[0:1, :].astype(MXU_DTYPE).astype(F32), axis=-1, keepdims=True)
    ok_new = sel_rows[:, own:own + 1] > 0.5
    s_new = jnp.where(ok_new, s_new, NEG)
    m = s_new
    for s, ok in zip(scores, masks):
        m = jnp.maximum(m, jnp.max(jnp.where(ok, s, NEG), axis=-1, keepdims=True))
    e_new = jnp.where(ok_new, jnp.exp2(s_new - m), 0.0)
    den = e_new
    acc = e_new * new[1:2, :].astype(MXU_DTYPE).astype(F32)
    for p, (s, ok) in enumerate(zip(scores, masks)):
        e = jnp.where(ok, jnp.exp2(s - m), 0.0)
        den = den + jnp.sum(e, axis=-1, keepdims=True)
        acc = acc + _dot_nt(e.astype(MXU_DTYPE), v_pages[p][0, 0].astype(MXU_DTYPE))
    o_s = _block_diag_fold(acc * (1.0 / den))

    kwin, vwin = kwin_ref[0, 0], vwin_ref[0, 0]
    s_w = _dot(q, kwin.astype(MXU_DTYPE))
    s_wn = jnp.sum(qf * new[2:3, :].astype(MXU_DTYPE).astype(F32), axis=-1, keepdims=True)
    m_w = jnp.maximum(jnp.max(s_w, axis=-1, keepdims=True), s_wn)
    e_w, e_wn = jnp.exp2(s_w - m_w), jnp.exp2(s_wn - m_w)
    den_w = jnp.sum(e_w, axis=-1, keepdims=True) + e_wn
    acc_w = _dot_nt(e_w.astype(MXU_DTYPE), vwin.astype(MXU_DTYPE)) + e_wn * new[3:4, :].astype(MXU_DTYPE).astype(F32)
    o_w = _block_diag_fold(acc_w * (1.0 / den_w))

    g = g_ref[0]
    o_ref[0] = g[:, 0:1] * oc_ref[0] + g[:, 1:2] * o_s + g[:, 2:3] * o_w

    wb = kwin.shape[1]
    last = lax.broadcasted_iota(jnp.int32, (1, wb), 1) == wb - 1
    kwin_out[0, 0] = jnp.where(last, newc_ref[0, 0], pltpu.roll(kwin, wb - 1, 1))
    vwin_out[0, 0] = jnp.where(last, newc_ref[0, 1], pltpu.roll(vwin, wb - 1, 1))


def _nsa_attn_decode(page_table, q_bd, o_c, sel, new_rows, new_cols, gates, pool_k, pool_v, win_k, win_v,
                     li, prev):
    nb, n_pages = page_table.shape
    n_layers, _, _, wb = win_k.shape
    nsp = sel.shape[-1]
    per_b = lambda shape: pl.BlockSpec((1,) + shape, lambda b, pt: (b,) + (0,) * len(shape))
    page = lambda p: pl.BlockSpec((1, 1, KV_W, PAGE_SIZE), lambda b, pt: (li, pt[b, p], 0, 0))
    win = pl.BlockSpec((1, 1, KV_W, wb), lambda b, pt: (li, b, 0, 0))
    kern = functools.partial(_nsa_attn_decode_kernel, n_pages=n_pages, n_prev=len(prev))
    n_in = 7 + 2 * n_pages + 2
    win_shape = jax.ShapeDtypeStruct((n_layers, nb, KV_W, wb), F32)
    return pl.pallas_call(
        kern,
        grid_spec=pltpu.PrefetchScalarGridSpec(
            num_scalar_prefetch=1,
            grid=(nb,),
            in_specs=([per_b((HQ, KV_W)), per_b((HQ, DH)), per_b((KVH, nsp)), per_b((4, KV_W)),
                       per_b((2, KV_W, 1)), per_b((HQ, 3))]
                      + [page(p) for p in range(n_pages)] + [page(p) for p in range(n_pages)]
                      + [win, win] + _alias_specs(prev)),
            out_specs=[per_b((HQ, DH)), win, win],
        ),
        out_shape=[jax.ShapeDtypeStruct((nb, HQ, DH), F32), win_shape, win_shape],
        input_output_aliases={n_in + j: 1 + j for j in range(len(prev))},
        compiler_params=_params(1),
        name="nsa_attn_decode",
    )(page_table, q_bd, o_c, sel.reshape(nb, KVH, nsp), new_rows, new_cols, gates,
      *([pool_k] * n_pages), *([pool_v] * n_pages), win_k, win_v, *prev)


def _outproj_ln_kernel(x_ref, o_ref, z_ref, w_ref, g_ref, b_ref, y_ref, *, alpha):
    z = z_ref[...]
    a = o_ref[...] * (z * (1.0 / (1.0 + jnp.exp(-z))))
    h = alpha * x_ref[...] + _dot(a.astype(MXU_DTYPE), w_ref[...])
    d = h - jnp.mean(h, axis=-1, keepdims=True)
    var = jnp.mean(d * d, axis=-1, keepdims=True)
    y_ref[...] = d * lax.rsqrt(var + LN_EPS) * g_ref[...] + b_ref[...]


def _outproj_ln(x2d, o2d, z2d, w, ln_g, ln_b, alpha, tm):
    m, d = x2d.shape
    wdt = o2d.shape[1]
    row = lambda n: pl.BlockSpec((tm, n), lambda i: (i, 0))
    fixed = lambda shape: pl.BlockSpec(shape, lambda i: (0, 0))
    return pl.pallas_call(
        functools.partial(_outproj_ln_kernel, alpha=alpha),
        grid=(m // tm,),
        in_specs=[row(d), row(wdt), row(wdt), fixed((wdt, d)), fixed((1, d)), fixed((1, d))],
        out_specs=row(d),
        out_shape=jax.ShapeDtypeStruct((m, d), F32),
        compiler_params=_params(1),
        name="outproj_ln",
    )(x2d, o2d, z2d, w, ln_g.reshape(1, d), ln_b.reshape(1, d))


def _ret_inproj_kernel(x_ref, w_ref, c_ref, sa_ref, sb_ref, q_ref, k_ref, v_ref, z_ref, *, rdk):
    xb = x_ref[...].astype(MXU_DTYPE)
    c, sa, sb = c_ref[...], sa_ref[...], sb_ref[...]
    qk_w = RH * rdk
    col = 0
    for dst, scale in ((q_ref, 1.0), (k_ref, rdk ** -0.5)):
        for ch in range(qk_w // 512):
            a = _rotate(_dot(xb, w_ref[:, col:col + 512]), c, sa, sb, 1)
            dst[:, ch * 512:(ch + 1) * 512] = a * scale
            col += 512
    for dst in (v_ref, z_ref):
        for ch in range(2 * qk_w // 512):
            dst[:, ch * 512:(ch + 1) * 512] = _dot(xb, w_ref[:, col:col + 512])
            col += 512


def _ret_inproj(x2d, w, tabs, t, tm):
    m, d = x2d.shape
    nt = t // tm
    rdk = d // RH
    row = lambda n: pl.BlockSpec((tm, n), lambda i: (i, 0))
    tab = pl.BlockSpec((tm, rdk), lambda i: (i % nt, 0))
    return pl.pallas_call(
        functools.partial(_ret_inproj_kernel, rdk=rdk),
        grid=(m // tm,),
        in_specs=[row(d), pl.BlockSpec(w.shape, lambda i: (0, 0)), tab, tab, tab],
        out_specs=[row(d), row(d), row(2 * d), row(2 * d)],
        out_shape=[jax.ShapeDtypeStruct((m, d), F32), jax.ShapeDtypeStruct((m, d), F32),
                   jax.ShapeDtypeStruct((m, 2 * d), F32), jax.ShapeDtypeStruct((m, 2 * d), F32)],
        compiler_params=_params(1),
        name="ret_inproj",
    )(x2d, w, *tabs)


def _group_norm(o, gain):
    d = o - jnp.mean(o, axis=-1, keepdims=True)
    return d * lax.rsqrt(jnp.mean(d * d, axis=-1, keepdims=True) + LN_EPS) * gain


def _ret_prompt_kernel(q_ref, k_ref, v_ref, dm_ref, dec_ref, gn_ref, o_ref, s_ref, s_sc, *, rdk, rdv):
    ci = pl.program_id(1)

    @pl.when(ci == 0)
    def _():
        s_sc[...] = jnp.zeros(s_sc.shape, F32)

    heads = range(RH)
    q = [q_ref[0, :, h * rdk:(h + 1) * rdk] for h in heads]
    k = [k_ref[0, :, h * rdk:(h + 1) * rdk] for h in heads]
    v = [v_ref[0, :, h * rdv:(h + 1) * rdv].astype(MXU_DTYPE) for h in heads]
    dec = [dec_ref[h] for h in heads]
    s_old = [s_sc[h] for h in heads]
    inner = [_dot_nt(q[h].astype(MXU_DTYPE), k[h].astype(MXU_DTYPE)) for h in heads]
    cross = [_dot((q[h] * dec[h][:, 0:1]).astype(MXU_DTYPE), s_old[h].astype(MXU_DTYPE)) for h in heads]
    outer = [_dot_tn((k[h] * dec[h][:, 1:2]).astype(MXU_DTYPE), v[h]) for h in heads]
    intra = [_dot((inner[h] * dm_ref[h]).astype(MXU_DTYPE), v[h]) for h in heads]
    for h in heads:
        s_sc[h] = s_old[h] * dec[h][0:1, 2:3] + outer[h]
        o_ref[0, :, h * rdv:(h + 1) * rdv] = _group_norm(intra[h] + cross[h], gn_ref[:, h * rdv:(h + 1) * rdv])

    @pl.when(ci == pl.num_programs(1) - 1)
    def _():
        s_ref[0] = s_sc[...]


def _ret_prompt(q, k, v, dmask, dec, gn_g):
    nb, t, qk_w = q.shape
    rdk = qk_w // RH
    rdv = v.shape[2] // RH
    c = dmask.shape[1]
    tok = lambda n: pl.BlockSpec((1, c, n), lambda b, i: (b, i, 0))
    full = lambda a: pl.BlockSpec(a.shape, lambda b, i: (0,) * a.ndim)
    gn_g = gn_g.reshape(1, RH * rdv)
    return pl.pallas_call(
        functools.partial(_ret_prompt_kernel, rdk=rdk, rdv=rdv),
        grid=(nb, t // c),
        in_specs=[tok(qk_w), tok(qk_w), tok(RH * rdv), full(dmask), full(dec), full(gn_g)],
        out_specs=[tok(RH * rdv), pl.BlockSpec((1, RH, rdk, rdv), lambda b, i: (b, 0, 0, 0))],
        out_shape=[jax.ShapeDtypeStruct((nb, t, RH * rdv), F32),
                   jax.ShapeDtypeStruct((nb, RH, rdk, rdv), F32)],
        scratch_shapes=[pltpu.VMEM((RH, rdk, rdv), F32)],
        compiler_params=_params(2),
        name="ret_prompt",
    )(q, k, v, dmask, dec, gn_g)


def _ret_decode_kernel(dec_ref, q_ref, k_ref, v_ref, s0_ref, gn_ref, *rest, rdk, rdv):
    o_ref, s_ref = rest[-2:]
    pad = LANES
    first_row = lax.broadcasted_iota(jnp.int32, (pad, 1), 0) == 0
    for h in range(RH):
        q = q_ref[0, :, h * rdk:(h + 1) * rdk]
        k = k_ref[0, :, h * rdk:(h + 1) * rdk]
        v = v_ref[0, :, h * rdv:(h + 1) * rdv]
        dmask, q_dec, k_dec, c_dec = dec_ref[h, 0], dec_ref[h, 1], dec_ref[h, 2], dec_ref[h, 3]
        s0 = s0_ref[0, 0, h]
        k_pad = jnp.where(first_row, jnp.broadcast_to(k * k_dec, (pad, rdk)), 0.0).astype(MXU_DTYPE)
        v_pad = jnp.broadcast_to(v, (pad, rdv)).astype(MXU_DTYPE)
        s_ref[0, 0, h] = s0 * c_dec + _dot_tn(k_pad, v_pad)
        qk = jnp.sum(q.astype(MXU_DTYPE).astype(F32) * k.astype(MXU_DTYPE).astype(F32), axis=-1, keepdims=True)
        inner = (qk * dmask).astype(MXU_DTYPE).astype(F32)
        q_pad = jnp.broadcast_to(q * q_dec, (8, rdk)).astype(MXU_DTYPE)
        o = inner * v.astype(MXU_DTYPE).astype(F32) + _dot(q_pad, s0.astype(MXU_DTYPE))[0:1, :]
        o_ref[0, :, h * rdv:(h + 1) * rdv] = _group_norm(o, gn_ref[:, h * rdv:(h + 1) * rdv])


def _ret_decode(q, k, v, state, li, dec, gn_g, prev):
    nb, qk_w = q.shape
    rdk, rdv = qk_w // RH, v.shape[1] // RH
    vec = lambda n: pl.BlockSpec((1, 1, n), lambda b: (b, 0, 0))
    st_spec = pl.BlockSpec((1, 1, RH, rdk, rdv), lambda b: (li, b, 0, 0, 0))
    return pl.pallas_call(
        functools.partial(_ret_decode_kernel, rdk=rdk, rdv=rdv),
        grid=(nb,),
        in_specs=[pl.BlockSpec(memory_space=pltpu.SMEM), vec(qk_w), vec(qk_w), vec(RH * rdv), st_spec,
                  pl.BlockSpec((1, RH * rdv), lambda b: (0, 0))] + _alias_specs(prev),
        out_specs=[vec(RH * rdv), st_spec],
        out_shape=[jax.ShapeDtypeStruct((nb, 1, RH * rdv), F32), jax.ShapeDtypeStruct(state.shape, F32)],
        input_output_aliases={6 + j: 1 + j for j in range(len(prev))},
        compiler_params=_params(1),
        name="ret_decode",
    )(dec, q.reshape(nb, 1, qk_w), k.reshape(nb, 1, qk_w), v.reshape(nb, 1, RH * rdv), state,
      gn_g.reshape(1, RH * rdv), *prev)


def _rope_tables(pos):
    half = ROT_DIM // 2
    inv = ROPE_THETA ** (-jnp.arange(half, dtype=F32) / half)
    ang = pos.astype(F32)[:, None] * inv[None, :]
    lane = jnp.arange(LANES) % DH
    cos, sin = jnp.cos(ang)[:, lane % half], jnp.sin(ang)[:, lane % half]
    c = jnp.where(lane < ROT_DIM, cos, 1.0)
    sa = jnp.where((lane >= half) & (lane < ROT_DIM), sin, 0.0)
    sb = jnp.where(lane < half, -sin, 0.0)
    return c, sa, sb


def _xpos_tables(pos, rdk):
    half = rdk // 2
    inv = 1.0 / (XPOS_BASE ** jnp.linspace(0.0, 1.0, half, dtype=F32))
    ang = pos.astype(F32)[:, None] * inv[None, :]
    lane = jnp.arange(rdk)
    cos, sin = jnp.cos(ang)[:, lane // 2], jnp.sin(ang)[:, lane // 2]
    sa = jnp.where(lane % 2 == 1, sin, 0.0)
    sb = jnp.where(lane % 2 == 0, -sin, 0.0)
    return cos, sa, sb


def _decay_tables(c):
    log_g = jnp.log1p(-jnp.power(2.0, -5.0 - jnp.arange(RH, dtype=F32)))
    i = jnp.arange(c, dtype=F32)
    diff = i[:, None] - i[None, :]
    dmask = jnp.where(diff >= 0, jnp.exp(jnp.maximum(diff, 0.0)[None] * log_g[:, None, None]), 0.0)
    q_dec = jnp.exp((i + 1.0)[None, :] * log_g[:, None])
    k_dec = jnp.exp((c - 1.0 - i)[None, :] * log_g[:, None])
    c_dec = jnp.exp(c * log_g)
    return dmask, q_dec, k_dec, c_dec


def _nsa_w_in_layout(w):
    kv_end = NSA_W + 6 * KV_W
    gates = jnp.pad(w[:, kv_end:kv_end + 3 * HQ], ((0, 0), (0, GATE_PAD - 3 * HQ)))
    return jnp.concatenate([w[:, :kv_end], gates, w[:, kv_end + 3 * HQ:]], axis=1).astype(MXU_DTYPE)


def _channel_major(cache):
    return jnp.moveaxis(cache, -3, -1).reshape(cache.shape[:-3] + (KV_W, cache.shape[-3]))


def _token_major(slab):
    return jnp.moveaxis(slab.reshape(slab.shape[:-2] + (KVH, DH, slab.shape[-1])), -1, -3)


def _row_tile(m):
    for tm in (256, 128):
        if m % tm == 0:
            return tm
    raise ValueError(f"row count {m} is not a multiple of {LANES}")


def _nsa_layer(xp, xs, li, n_layers, page_table, pools, wins, w_in, w_out, cmp_k, cmp_v, ln_g, ln_b, alpha,
               prev_p, prev_s, prev_win):
    nb, t, d = xp.shape
    nd = xs.shape[0]
    past = page_table.shape[1] * PAGE_SIZE
    assert xs.shape[1] == 1 and t % SEL_TILE == 0 and t >= WIN_TILE
    w_in = _nsa_w_in_layout(w_in)
    w_out = w_out.astype(MXU_DTYPE)

    xp2 = xp.reshape(nb * t, d)
    tm = _row_tile(t)
    q_hm, kv, khm, v_t, g_t, z, *cache_p = _nsa_inproj(xp2, w_in, _rope_tables(jnp.arange(t)), nb, t, tm,
                                                        li, n_layers, prev_p)
    kv = kv.reshape(6, nb, t, KV_W)
    kc_hm = _compress_prompt(kv[0], cmp_k, "head_major")
    vc_t = _compress_prompt(kv[1], cmp_v, "channel_major")
    o = _nsa_attn_prompt(q_hm, kc_hm, vc_t, khm, v_t, g_t)
    yp = _outproj_ln(xp2, o.reshape(nb * t, NSA_W), z, w_out, ln_g, ln_b, alpha, tm).reshape(nb, t, d)

    xs2 = xs.reshape(nd, d)
    tms = _row_tile(nd)
    pos_s = jnp.full((nd,), past, jnp.int32)
    q_s, kv_s, _, _, g_s, z_s, *cache_s = _nsa_inproj(xs2, w_in, _rope_tables(pos_s), 1, nd, tms,
                                                      li, n_layers, prev_s)
    q_tok = q_s[0].transpose(1, 0, 2).astype(F32).reshape(nd, KVH, GQ, 1, DH)
    q_bd = (q_tok * jnp.eye(KVH, dtype=F32)[None, :, None, :, None]).reshape(nd, HQ, KV_W).astype(MXU_DTYPE)
    kc_tm = _compress_paged(pools[0], li, page_table, cmp_k)
    vc_tm = _compress_paged(pools[1], li, page_table, cmp_v)
    o_c, sel = _nsa_select(q_bd, kc_tm, vc_tm, past, min(16, nd))
    new_rows = jnp.stack([kv_s[2], kv_s[3], kv_s[4], kv_s[5]], axis=1)
    new_cols = jnp.stack([kv_s[4], kv_s[5]], axis=1)[..., None]
    gates_s = g_s[0, :3 * HQ].reshape(3, HQ, nd).transpose(2, 1, 0)
    o_s, *win_new = _nsa_attn_decode(page_table, q_bd, o_c, sel, new_rows, new_cols, gates_s,
                                     pools[2], pools[3], wins[0], wins[1], li, prev_win)
    ys = _outproj_ln(xs2, o_s.reshape(nd, NSA_W), z_s, w_out, ln_g, ln_b, alpha, tms).reshape(nd, 1, d)
    return yp, ys, cache_p, cache_s, win_new


def _ret_layer(xp, xs, li, state, past, w_in, gn_g, w_out, ln_g, ln_b, alpha, prev_state):
    nb, t, d = xp.shape
    nd = xs.shape[0]
    rdk = d // RH
    assert xs.shape[1] == 1 and t % RET_CHUNK == 0
    w_in = w_in.astype(MXU_DTYPE)
    w_out = w_out.astype(MXU_DTYPE)

    xp2 = xp.reshape(nb * t, d)
    tm = _row_tile(t)
    q, k, v, z = _ret_inproj(xp2, w_in, _xpos_tables(jnp.arange(t), rdk), t, tm)
    dmask, q_dec, k_dec, c_dec = _decay_tables(RET_CHUNK)
    dec = jnp.stack([q_dec, k_dec, jnp.broadcast_to(c_dec[:, None], q_dec.shape)]
                    + [jnp.zeros_like(q_dec)] * 5, axis=-1)
    o, sp = _ret_prompt(q.reshape(nb, t, d), k.reshape(nb, t, d), v.reshape(nb, t, 2 * d), dmask, dec, gn_g)
    yp = _outproj_ln(xp2, o.reshape(nb * t, 2 * d), z, w_out, ln_g, ln_b, alpha, tm).reshape(nb, t, d)

    xs2 = xs.reshape(nd, d)
    tms = _row_tile(nd)
    q, k, v, z = _ret_inproj(xs2, w_in, _xpos_tables(jnp.full((nd,), past, jnp.int32), rdk), nd, tms)
    dmask, q_dec, k_dec, c_dec = _decay_tables(1)
    dec = jnp.stack([dmask[:, 0, 0], q_dec[:, 0], k_dec[:, 0], c_dec], axis=-1)
    o, ss = _ret_decode(q, k, v, state, li, dec, gn_g, prev_state)
    ys = _outproj_ln(xs2, o.reshape(nd, 2 * d), z, w_out, ln_g, ln_b, alpha, tms).reshape(nd, 1, d)
    return yp, ys, sp, ss


def kernel(x_prompt, x_sample, cache_k_cmp, cache_v_cmp, cache_k_sel, cache_v_sel, cache_k_win, cache_v_win,
           state_ret, page_table, nsa_w_in, nsa_w_out, nsa_pe_k, nsa_w1_k, nsa_w2_k, nsa_pe_v, nsa_w1_v,
           nsa_w2_v, ret_w_in, ret_gn_g, ret_w_out, ln_g, ln_b):
    depth = ln_g.shape[0]
    alpha = (2.0 * depth) ** 0.25
    n_nsa = nsa_w_in.shape[0]
    past = page_table.shape[1] * PAGE_SIZE
    t = x_prompt.shape[1]
    pools = [_channel_major(c) for c in (cache_k_cmp, cache_v_cmp, cache_k_sel, cache_v_sel)]
    wins = [_channel_major(c) for c in (cache_k_win, cache_v_win)]
    xp, xs = x_prompt, x_sample
    cache_p, cache_s, win_new, ret_p, ret_s = [], [], [], [], []
    for i in range(depth):
        li = i // 2
        if i % 2 == 0:
            cmp_k = _compress_weights(nsa_pe_k[li], nsa_w1_k[li], nsa_w2_k[li])
            cmp_v = _compress_weights(nsa_pe_v[li], nsa_w1_v[li], nsa_w2_v[li])
            xp, xs, cache_p, cache_s, win_new = _nsa_layer(
                xp, xs, li, n_nsa, page_table, pools, wins, nsa_w_in[li], nsa_w_out[li], cmp_k, cmp_v,
                ln_g[i], ln_b[i], alpha, cache_p, cache_s, win_new)
        else:
            xp, xs, sp, ss = _ret_layer(xp, xs, li, state_ret, past, ret_w_in[li], ret_gn_g[li], ret_w_out[li],
                                        ln_g[i], ln_b[i], alpha, ret_s)
            ret_p.append(sp)
            ret_s = [ss]
    wb = min(WINDOW, t)
    out_p = [_token_major(c) for c in cache_p[:4]] + [_token_major(c[..., t - wb:]) for c in cache_p[4:]]
    out_s = [jnp.moveaxis(_token_major(c), 2, 1) for c in cache_s[:4]] + [_token_major(w) for w in win_new]
    return (xp, xs,
            out_p[0], out_s[0], out_p[1], out_s[1], out_p[2], out_s[2], out_p[3], out_s[3],
            out_p[4], out_s[4], out_p[5], out_s[5],
            jnp.stack(ret_p), ret_s[0])
```

```python
import functools

import jax
import jax.numpy as jnp
from jax import lax
from jax.experimental import pallas as pl
from jax.experimental.pallas import tpu as pltpu

F32 = jnp.float32
MXU_DTYPE = jnp.bfloat16

HQ, KVH, DH = 32, 4, 64
GQ = HQ // KVH
NSA_W, KV_W = HQ * DH, KVH * DH
ROT_DIM, ROPE_THETA = DH // 4, 500000.0
CMP_LEN, CMP_STRIDE, CMP_HID = 32, 16, 2 * DH
SEL_BLK, N_SEL, WINDOW, QBLK = 64, 16, 512, 128
FORCE_BONUS, NEG = 1000.0, -1.0e30
RH, RET_CHUNK, XPOS_BASE = 4, 128, 10000.0
PAGE_SIZE, LN_EPS = 128, 1e-5

LANES = 128
VMEM_LIMIT_BYTES = 56 * 1024 * 1024
CMP_GROUP = 2048
GATE_PAD = LANES
KEY_TILE = 2 * SEL_BLK
SEL_TILE = 4 * KEY_TILE
BLOCKS_PER_TILE = SEL_TILE // SEL_BLK
WIN_TILE = WINDOW + QBLK
ONES_ROWS = 16
LOG2E = 1.4426950408889634
REMOVED = -3.0e38
RUNNING_MAX_INIT = -1.0e29


def _params(n_axes):
    return pltpu.CompilerParams(dimension_semantics=("arbitrary",) * n_axes,
                                vmem_limit_bytes=VMEM_LIMIT_BYTES)


def _dot(a, b):
    return jnp.dot(a, b, preferred_element_type=F32)


def _dot_nt(a, b):
    return lax.dot_general(a, b, (((1,), (1,)), ((), ())), preferred_element_type=F32)


def _dot_tn(a, b):
    return lax.dot_general(a, b, (((0,), (0,)), ((), ())), preferred_element_type=F32)


def _silu(z):
    return z * (1.0 / (1.0 + jnp.exp(-z)))


def _lane_tile(t, reps):
    return t if reps == 1 else jnp.concatenate([t] * reps, axis=1)


def _rotate(a, c, sa, sb, shift):
    n = a.shape[1]
    reps = n // c.shape[1]
    c, sa, sb = _lane_tile(c, reps), _lane_tile(sa, reps), _lane_tile(sb, reps)
    return a * c + pltpu.roll(a, shift, 1) * sa + pltpu.roll(a, n - shift, 1) * sb


def _masked_softmax(s, mask):
    s = jnp.where(mask, s, NEG)
    m = jnp.max(s, axis=-1, keepdims=True)
    e = jnp.where(mask, jnp.exp2(s - m), 0.0)
    d = jnp.sum(e, axis=-1, keepdims=True)
    return e * (1.0 / jnp.where(d > 0.0, d, 1.0))


def _split3(x):
    hi = x.astype(MXU_DTYPE)
    r = x - hi.astype(F32)
    mid = r.astype(MXU_DTYPE)
    lo = (r - mid.astype(F32)).astype(MXU_DTYPE)
    return hi, mid, lo


def _topk_mask(score, idx, n_pick, axis):
    big = jnp.float32(score.shape[axis])
    sel = jnp.zeros(score.shape, F32)
    for _ in range(n_pick):
        m = jnp.max(score, axis=axis, keepdims=True)
        first = jnp.min(jnp.where(score == m, idx, big), axis=axis, keepdims=True)
        pick = idx == first
        sel = jnp.where(pick, 1.0, sel)
        score = jnp.where(pick, REMOVED, score)
    return sel


def _overlap(c_idx, s_idx):
    cs, ss = c_idx * CMP_STRIDE, s_idx * SEL_BLK
    ov = jnp.minimum(cs + CMP_LEN, ss + SEL_BLK) - jnp.maximum(cs, ss)
    return jnp.maximum(ov, 0).astype(F32) * (1.0 / CMP_LEN)


def _alias_specs(prev):
    return [pl.BlockSpec(memory_space=pl.ANY)] * len(prev)


def _nsa_inproj_kernel(x_ref, w_ref, c_ref, sa_ref, sb_ref, *rest):
    q_ref, kv_ref, ks_ref, kw_ref, vt_ref, gt_ref, z_ref = rest[-13:-6]
    cache_refs = rest[-6:]
    xb = x_ref[...].astype(MXU_DTYPE)
    c, sa, sb = c_ref[...], sa_ref[...], sb_ref[...]
    half = ROT_DIM // 2
    tm = x_ref.shape[0]
    token = pl.program_id(0) * tm + lax.broadcasted_iota(jnp.int32, (tm, DH), 0)
    blk_onehot = ((token // SEL_BLK) % BLOCKS_PER_TILE
                  == lax.broadcasted_iota(jnp.int32, (tm, DH), 1)).astype(F32)
    col = 0
    for ch in range(NSA_W // 512):
        a = _dot(xb, w_ref[:, col:col + 512])
        a = _rotate(a, c, sa, sb, half) * (DH ** -0.5 * LOG2E)
        for h in range(512 // DH):
            q_ref[0, ch * (512 // DH) + h] = a[:, h * DH:(h + 1) * DH].astype(q_ref.dtype)
        col += 512
    for j in range(6):
        a = _dot(xb, w_ref[:, col:col + KV_W])
        if j % 2 == 0:
            a = _rotate(a, c, sa, sb, half)
        kv_ref[j] = a
        a_t = a.T
        cache_refs[j][0, 0] = a_t
        if j == 2:
            for h in range(KVH):
                ks_ref[0, h] = jnp.concatenate([a[:, h * DH:(h + 1) * DH], blk_onehot], axis=1).astype(ks_ref.dtype)
        if j == 4:
            for h in range(KVH):
                kw_ref[0, h] = a[:, h * DH:(h + 1) * DH].astype(kw_ref.dtype)
        if j in (3, 5):
            vt_ref[j // 2 - 1, 0] = a_t.astype(vt_ref.dtype)
        col += KV_W
    a = _dot(xb, w_ref[:, col:col + GATE_PAD])
    gt_ref[0] = (1.0 / (1.0 + jnp.exp(-a))).T
    col += GATE_PAD
    for ch in range(NSA_W // 512):
        z_ref[:, ch * 512:(ch + 1) * 512] = _silu(_dot(xb, w_ref[:, col:col + 512])).astype(z_ref.dtype)
        col += 512


def _nsa_inproj(x2d, w, tabs, nb, t, tm, li, n_layers, prev):
    m, d = x2d.shape
    nt = t // tm
    n_cols = w.shape[1]
    row = lambda i: (i, 0)
    tab = pl.BlockSpec((tm, LANES), lambda i: (i % nt, 0))
    cache_spec = pl.BlockSpec((1, 1, KV_W, tm), lambda i: (li, i // nt, 0, i % nt))
    cache_shape = jax.ShapeDtypeStruct((n_layers, nb, KV_W, t), F32)
    n_in = 5
    return pl.pallas_call(
        _nsa_inproj_kernel,
        grid=(m // tm,),
        in_specs=[pl.BlockSpec((tm, d), row), pl.BlockSpec((d, n_cols), lambda i: (0, 0)), tab, tab, tab]
        + _alias_specs(prev),
        out_specs=[
            pl.BlockSpec((1, HQ, tm, DH), lambda i: (i // nt, 0, i % nt, 0)),
            pl.BlockSpec((6, tm, KV_W), lambda i: (0, i, 0)),
            pl.BlockSpec((1, KVH, tm, 2 * DH), lambda i: (i // nt, 0, i % nt, 0)),
            pl.BlockSpec((1, KVH, tm, DH), lambda i: (i // nt, 0, i % nt, 0)),
            pl.BlockSpec((2, 1, KV_W, tm), lambda i: (0, i // nt, 0, i % nt)),
            pl.BlockSpec((1, GATE_PAD, tm), lambda i: (i // nt, 0, i % nt)),
            pl.BlockSpec((tm, NSA_W), row),
        ] + [cache_spec] * 6,
        out_shape=[
            jax.ShapeDtypeStruct((nb, HQ, t, DH), MXU_DTYPE),
            jax.ShapeDtypeStruct((6, m, KV_W), F32),
            jax.ShapeDtypeStruct((nb, KVH, t, 2 * DH), MXU_DTYPE),
            jax.ShapeDtypeStruct((nb, KVH, t, DH), MXU_DTYPE),
            jax.ShapeDtypeStruct((2, nb, KV_W, t), MXU_DTYPE),
            jax.ShapeDtypeStruct((nb, GATE_PAD, t), F32),
            jax.ShapeDtypeStruct((m, NSA_W), MXU_DTYPE),
        ] + [cache_shape] * 6,
        input_output_aliases={n_in + j: 7 + j for j in range(len(prev))},
        compiler_params=_params(1),
        name="nsa_inproj",
    )(x2d, w, *tabs, *prev)


def _compress_kernel(*refs, n_prefetch, n_pages, layout, n_chunks):
    split = KV_W // LANES
    pos = n_prefetch
    if n_pages:
        page_refs = refs[pos:pos + n_pages]
        pos += n_pages
    else:
        tok_ref, halo_ref = refs[pos:pos + 2]
        pos += 2
    wpair_ref, pe_ref, w1_ref, w2_ref, out_ref, r_ref = refs[pos:pos + 6]
    r_ref[:, n_chunks:, :] = jnp.zeros((split, 8, CMP_STRIDE * LANES), F32)
    if n_pages:
        tok_sc = refs[pos + 6]
        for p, ref in enumerate(page_refs):
            page = ref[0, 0].T
            for s in range(split):
                tok_sc[s, p * PAGE_SIZE:(p + 1) * PAGE_SIZE, :] = page[:, s * LANES:(s + 1) * LANES]
        for s in range(split):
            for l in range(CMP_STRIDE):
                r_ref[s, 0:n_chunks, l * LANES:(l + 1) * LANES] = tok_sc[s, pl.ds(l, n_chunks, stride=CMP_STRIDE), :]
    else:
        for s in range(split):
            for l in range(CMP_STRIDE):
                lanes = slice(l * LANES, (l + 1) * LANES)
                j = l * split + s
                r_ref[s, 0:n_chunks, lanes] = tok_ref[0, pl.ds(j, n_chunks, stride=CMP_STRIDE * split), :]
                r_ref[s, n_chunks:n_chunks + 1, lanes] = halo_ref[0, j:j + 1, :]
    heads_per_tile = LANES // DH
    ab = [_dot(r_ref[s].astype(MXU_DTYPE), wpair_ref[...]) for s in range(split)]
    pe8 = jnp.broadcast_to(pe_ref[...], (8, CMP_LEN * DH)).astype(MXU_DTYPE)
    bias = _dot(pe8, w1_ref[...])[0:1, :]
    w2 = w2_ref[...]
    toks = []
    for h in range(KVH):
        base = (h % heads_per_tile) * 2 * CMP_HID
        first_half = ab[h // heads_per_tile][0:n_chunks, base:base + CMP_HID]
        second_half = ab[h // heads_per_tile][1:n_chunks + 1, base + CMP_HID:base + 2 * CMP_HID]
        u = first_half + second_half + bias
        hid = 0.5 * u * (1.0 + jnp.tanh(0.7978845608028654 * (u + 0.044715 * (u * u * u))))
        toks.append(_dot(hid.astype(MXU_DTYPE), w2))
    if layout == "head_major":
        for h in range(KVH):
            out_ref[0, h] = toks[h].astype(out_ref.dtype)
    elif layout == "token_major":
        out_ref[0] = jnp.concatenate(toks, axis=1).astype(out_ref.dtype)
    else:
        out_ref[0] = jnp.concatenate(toks, axis=1).T.astype(out_ref.dtype)


def _compress_weights(pe, w1, w2):
    heads = LANES // DH
    w1r = w1.reshape(2, CMP_STRIDE, DH, CMP_HID)
    eye = jnp.eye(heads, dtype=w1.dtype)
    wpair = jnp.einsum("sldn,hg->lhdgsn", w1r, eye).reshape(CMP_STRIDE * LANES, heads * 2 * CMP_HID)
    return (wpair.astype(MXU_DTYPE), pe.reshape(1, CMP_LEN * DH), w1.astype(MXU_DTYPE), w2.astype(MXU_DTYPE))


def _compress_prompt(k_tok, weights, layout):
    nb, t, _ = k_tok.shape
    gt = min(CMP_GROUP, t)
    n_chunks = gt // CMP_STRIDE
    ng = t // gt
    last_halo = t // CMP_STRIDE - 1
    wpair, pe, w1, w2 = weights
    split = KV_W // LANES
    k_tok = k_tok.reshape(nb, t * split, LANES)
    full = lambda a: pl.BlockSpec(a.shape, lambda b, g: (0,) * a.ndim)
    kern = functools.partial(_compress_kernel, n_prefetch=0, n_pages=0, layout=layout, n_chunks=n_chunks)
    if layout == "head_major":
        out_spec = pl.BlockSpec((1, KVH, n_chunks, DH), lambda b, g: (b, 0, g, 0))
        out_shape = jax.ShapeDtypeStruct((nb, KVH, t // CMP_STRIDE, DH), MXU_DTYPE)
    else:
        out_spec = pl.BlockSpec((1, KV_W, n_chunks), lambda b, g: (b, 0, g))
        out_shape = jax.ShapeDtypeStruct((nb, KV_W, t // CMP_STRIDE), MXU_DTYPE)
    return pl.pallas_call(
        kern,
        grid=(nb, ng),
        in_specs=[pl.BlockSpec((1, gt * split, LANES), lambda b, g: (b, g, 0)),
                  pl.BlockSpec((1, CMP_STRIDE * split, LANES),
                               lambda b, g: (b, jnp.minimum((g + 1) * n_chunks, last_halo), 0)),
                  full(wpair), full(pe), full(w1), full(w2)],
        out_specs=out_spec,
        out_shape=out_shape,
        scratch_shapes=[pltpu.VMEM((split, n_chunks + 8, CMP_STRIDE * LANES), F32)],
        compiler_params=_params(2),
        name="nsa_compress_prompt",
    )(k_tok, k_tok, wpair, pe, w1, w2)


def _compress_paged(pool, li, page_table, weights):
    nb, n_pages = page_table.shape
    n_chunks = n_pages * PAGE_SIZE // CMP_STRIDE
    wpair, pe, w1, w2 = weights
    split = KV_W // LANES
    full = lambda a: pl.BlockSpec(a.shape, lambda b, pt: (0,) * a.ndim)
    page = lambda p: pl.BlockSpec((1, 1, KV_W, PAGE_SIZE), lambda b, pt: (li, pt[b, p], 0, 0))
    kern = functools.partial(_compress_kernel, n_prefetch=1, n_pages=n_pages, layout="token_major",
                             n_chunks=n_chunks)
    return pl.pallas_call(
        kern,
        grid_spec=pltpu.PrefetchScalarGridSpec(
            num_scalar_prefetch=1,
            grid=(nb,),
            in_specs=[page(p) for p in range(n_pages)] + [full(wpair), full(pe), full(w1), full(w2)],
            out_specs=pl.BlockSpec((1, n_chunks, KV_W), lambda b, pt: (b, 0, 0)),
            scratch_shapes=[pltpu.VMEM((split, n_chunks + 8, CMP_STRIDE * LANES), F32),
                            pltpu.VMEM((split, n_pages * PAGE_SIZE, LANES), F32)],
        ),
        out_shape=jax.ShapeDtypeStruct((nb, n_chunks, KV_W), MXU_DTYPE),
        compiler_params=_params(1),
        name="nsa_compress_paged",
    )(page_table, *([pool] * n_pages), wpair, pe, w1, w2)


def _nsa_attn_prompt_kernel(q_ref, kc_ref, vct_ref, ks_ref, vst_ref, kw_ref, vwt_ref, gc_ref, gs_ref, gw_ref,
                            o_ref, ot_sc, m_sc, alpha_sc, acc_sc, p_sc, s_sc, lanes_sc, *, seq_len):
    st = pl.program_id(2) * QBLK
    ncp = kc_ref.shape[2]
    ns = seq_len // SEL_BLK
    n_pairs = GQ // 2
    qpos = st + lax.broadcasted_iota(jnp.int32, (1, QBLK), 1)
    q_pairs = [q_ref[0, 2 * j:2 * j + 2].reshape(2 * QBLK, DH) for j in range(n_pairs)]
    both = lambda b: jnp.concatenate([b, b], axis=1)

    c_row = lax.broadcasted_iota(jnp.int32, (ncp, 1), 0)
    keep_c = (c_row * CMP_STRIDE + (CMP_LEN - 1) <= qpos).astype(F32)
    bias_c = both((1.0 - keep_c) * NEG)
    keep_c = both(keep_c)
    kc, vct = kc_ref[0, 0], vct_ref[0]
    ws = pl.multiple_of(jnp.maximum(st - WINDOW, 0), KEY_TILE)
    w_row = lax.broadcasted_iota(jnp.int32, (WIN_TILE, 1), 0)
    dist = qpos - (ws + w_row)
    bias_w = both(jnp.where((dist >= 0) & (dist <= WINDOW), 0.0, NEG))
    kw_tile = kw_ref[0, 0, pl.ds(ws, WIN_TILE), :]
    vwt_tile = jnp.concatenate([vwt_ref[0, 0, :, pl.ds(ws, WIN_TILE)], jnp.ones((ONES_ROWS, WIN_TILE), MXU_DTYPE)],
                               axis=0)
    c_scores = [_dot_nt(kc, q_pairs[j]) for j in range(n_pairs)]
    w_scores = [_dot_nt(kw_tile, q_pairs[j]) for j in range(n_pairs)]
    ks_first = ks_ref[0, 0, 0:SEL_TILE, 0:DH]
    for j in range(n_pairs):
        s_sc[j] = _dot_nt(ks_first, q_pairs[j])
    p_sum = jnp.zeros((ncp, QBLK), F32)
    c_probs, w_probs = [], []
    for j in range(n_pairs):
        s = c_scores[j] + bias_c
        e = jnp.exp2(s - jnp.max(s, axis=0, keepdims=True)) * keep_c
        den = jnp.sum(e, axis=0, keepdims=True)
        p = e * (1.0 / jnp.where(den > 0.0, den, 1.0))
        c_probs.append(p.astype(MXU_DTYPE))
        p_sum = p_sum + p[:, :QBLK] + p[:, QBLK:]
    for j in range(n_pairs):
        s = w_scores[j] + bias_w
        w_probs.append(jnp.exp2(s - jnp.max(s, axis=0, keepdims=True)).astype(MXU_DTYPE))
    o_c = [_dot(vct, p) for p in c_probs]
    o_w = [_dot(vwt_tile, p) for p in w_probs]
    o_w = [o[:DH] * (1.0 / o[DH:DH + 1]) for o in o_w]

    s_idx = lax.broadcasted_iota(jnp.int32, (ns, 1), 0)
    ov_t = _overlap(lax.broadcasted_iota(jnp.int32, (1, ncp), 1), s_idx).astype(MXU_DTYPE)
    imp_t = sum(_dot(ov_t, part) for part in _split3(p_sum))
    blk_q = qpos // SEL_BLK
    forced = (s_idx == 0) | (s_idx == blk_q) | (s_idx == blk_q - 1)
    valid = s_idx * SEL_BLK <= qpos
    score = jnp.where(valid, imp_t + FORCE_BONUS * forced.astype(F32), NEG)
    sel_t = _topk_mask(score, s_idx.astype(F32), min(N_SEL, ns), 0)
    sel_bias = jnp.where(valid & (sel_t > 0.5), 0.0, NEG)
    sel_bias_q = sel_bias.T.astype(MXU_DTYPE)

    n_tiles = (st + QBLK + SEL_TILE - 1) // SEL_TILE
    m_sc[...] = jnp.full(m_sc.shape, RUNNING_MAX_INIT, F32)
    acc_sc[...] = jnp.zeros(acc_sc.shape, F32)
    alpha_sc[...] = jnp.ones(alpha_sc.shape, F32)
    p_sc[...] = jnp.zeros(p_sc.shape, MXU_DTYPE)
    ones_rows = jnp.ones((ONES_ROWS, SEL_TILE), MXU_DTYPE)
    first_bias = both(jnp.concatenate(
        [jnp.broadcast_to(sel_bias[b:b + 1, :], (SEL_BLK, QBLK)) for b in range(BLOCKS_PER_TILE)], axis=0))
    for j in range(n_pairs):
        s_sc[j] = s_sc[j] + first_bias
    blk_row = lax.broadcasted_iota(jnp.int32, (ns, DH), 0)
    blk_lane = lax.broadcasted_iota(jnp.int32, (ns, DH), 1)

    def key_start(i):
        return pl.multiple_of(i * SEL_TILE, SEL_TILE)

    def fold(i):
        vt_tile = jnp.concatenate([vst_ref[0, 0, :, pl.ds(key_start(i), SEL_TILE)], ones_rows], axis=0)
        for j in range(n_pairs):
            acc_sc[j] = alpha_sc[j] * acc_sc[j] + _dot(vt_tile, p_sc[j])

    def bias_lanes(i):
        pick = ((blk_row - i * BLOCKS_PER_TILE == blk_lane) & (blk_lane < BLOCKS_PER_TILE)).astype(MXU_DTYPE)
        return _dot(sel_bias_q, pick).astype(MXU_DTYPE)

    def masked_queries(lanes):
        return [jnp.concatenate([jnp.concatenate([q_ref[0, 2 * j + u], lanes], axis=1) for u in range(2)], axis=0)
                for j in range(n_pairs)]

    def softmax(j, s):
        m_old = m_sc[j]
        m_new = jnp.maximum(m_old, jnp.max(s, axis=0, keepdims=True))
        p_sc[j] = jnp.exp2(s - m_new).astype(MXU_DTYPE)
        alpha_sc[j] = jnp.exp2(m_old - m_new)
        m_sc[j] = m_new

    lanes_sc[...] = bias_lanes(1)

    def step(i, carry):
        fold(jnp.maximum(i - 1, 0))
        k_ahead = ks_ref[0, 0, pl.ds(key_start(i + 1), SEL_TILE), :]
        q_ahead = masked_queries(lanes_sc[...])
        for j in range(n_pairs):
            ahead = _dot_nt(k_ahead, q_ahead[j])
            softmax(j, s_sc[j])
            s_sc[j] = ahead
        lanes_sc[...] = bias_lanes(i + 2)
        return carry

    lax.fori_loop(0, n_tiles - 1, step, 0)
    fold(jnp.maximum(n_tiles - 2, 0))
    k_row = (n_tiles - 1) * SEL_TILE + lax.broadcasted_iota(jnp.int32, (SEL_TILE, 1), 0)
    causal = both(jnp.where(k_row <= qpos, 0.0, NEG))
    for j in range(n_pairs):
        softmax(j, s_sc[j] + causal)
    fold(n_tiles - 1)
    o_s = [acc_sc[j, :DH] * (1.0 / acc_sc[j, DH:DH + 1]) for j in range(n_pairs)]
    gc, gs, gw = gc_ref[0], gs_ref[0], gw_ref[0]
    for h in range(GQ):
        j, cols = h // 2, slice((h % 2) * QBLK, (h % 2 + 1) * QBLK)
        ot_sc[h * DH:(h + 1) * DH, :] = (gc[h:h + 1] * o_c[j][:, cols] + gs[h:h + 1] * o_s[j][:, cols]
                                         + gw[h:h + 1] * o_w[j][:, cols])
    o_ref[0] = ot_sc[...].T.astype(o_ref.dtype)


def _nsa_attn_prompt(q_hm, kc_hm, vc_t, ks_aug, kw_hm, v_t, gates_t):
    nb, _, t, _ = q_hm.shape
    ncp = kc_hm.shape[2]
    k_spec = lambda width: pl.BlockSpec((1, 1, t, width), lambda b, h, i: (b, h, 0, 0))
    vt_spec = lambda j: pl.BlockSpec((1, 1, DH, t), lambda b, h, i: (j, b, h, 0))
    gate_spec = lambda br: pl.BlockSpec((1, GQ, QBLK), lambda b, h, i: (b, br * KVH + h, i))
    kern = functools.partial(_nsa_attn_prompt_kernel, seq_len=t)
    return pl.pallas_call(
        kern,
        grid=(nb, KVH, t // QBLK),
        in_specs=[pl.BlockSpec((1, GQ, QBLK, DH), lambda b, h, i: (b, h, i, 0)),
                  pl.BlockSpec((1, 1, ncp, DH), lambda b, h, i: (b, h, 0, 0)),
                  pl.BlockSpec((1, DH, ncp), lambda b, h, i: (b, h, 0)),
                  k_spec(2 * DH), vt_spec(0), k_spec(DH), vt_spec(1),
                  gate_spec(0), gate_spec(1), gate_spec(2)],
        out_specs=pl.BlockSpec((1, QBLK, GQ * DH), lambda b, h, i: (b, i, h)),
        out_shape=jax.ShapeDtypeStruct((nb, t, NSA_W), MXU_DTYPE),
        scratch_shapes=[pltpu.VMEM((GQ * DH, QBLK), F32),
                        pltpu.VMEM((GQ // 2, 1, 2 * QBLK), F32), pltpu.VMEM((GQ // 2, 1, 2 * QBLK), F32),
                        pltpu.VMEM((GQ // 2, DH + ONES_ROWS, 2 * QBLK), F32),
                        pltpu.VMEM((GQ // 2, SEL_TILE, 2 * QBLK), MXU_DTYPE),
                        pltpu.VMEM((GQ // 2, SEL_TILE, 2 * QBLK), F32),
                        pltpu.VMEM((QBLK, DH), MXU_DTYPE)],
        compiler_params=_params(3),
        name="nsa_attn_prompt",
    )(q_hm, kc_hm, vc_t, ks_aug, v_t, kw_hm, v_t, gates_t, gates_t, gates_t)


def _block_diag_fold(x):
    r_kv = lax.broadcasted_iota(jnp.int32, (HQ, KV_W), 0) // GQ
    l_kv = lax.broadcasted_iota(jnp.int32, (HQ, KV_W), 1) // DH
    x = jnp.where(r_kv == l_kv, x, 0.0)
    return sum(x[..., h * DH:(h + 1) * DH] for h in range(KVH))


def _nsa_select_kernel(q_ref, kc_ref, vc_ref, oc_ref, sel_ref, *, past):
    bb, ncp = kc_ref.shape[0], kc_ref.shape[1]
    nsp = sel_ref.shape[1]
    q = q_ref[...]
    s = jnp.einsum("bqd,bkd->bqk", q, kc_ref[...], preferred_element_type=F32)
    c_idx = lax.broadcasted_iota(jnp.int32, (1, 1, ncp), 2)
    p_c = _masked_softmax(s, c_idx * CMP_STRIDE + (CMP_LEN - 1) <= past)
    o_c = jnp.einsum("bqk,bkd->bqd", p_c.astype(MXU_DTYPE), vc_ref[...], preferred_element_type=F32)
    oc_ref[...] = _block_diag_fold(o_c)
    ov = _overlap(lax.broadcasted_iota(jnp.int32, (ncp, 1), 0),
                  lax.broadcasted_iota(jnp.int32, (1, nsp), 1)).astype(MXU_DTYPE)
    imp_rows = sum(_dot(part, ov) for part in _split3(p_c.reshape(bb * HQ, ncp)))
    imp = jnp.sum(imp_rows.reshape(bb * KVH, GQ, nsp), axis=1)
    s_idx = lax.broadcasted_iota(jnp.int32, (1, nsp), 1)
    blk_q = past // SEL_BLK
    forced = (s_idx == 0) | (s_idx == blk_q) | (s_idx == blk_q - 1)
    valid = s_idx * SEL_BLK <= past
    score = jnp.where(valid, imp + FORCE_BONUS * forced.astype(F32), NEG)
    n_blocks = past // SEL_BLK + 1
    sel = _topk_mask(score, s_idx.astype(F32), min(N_SEL, n_blocks), 1)
    sel_ref[...] = jnp.where(valid, sel, 0.0)


def _nsa_select(q_bd, kc_tm, vc_tm, past, bb):
    nb, ncp, _ = kc_tm.shape
    nsp = LANES
    assert past // SEL_BLK + 1 <= nsp
    kern = functools.partial(_nsa_select_kernel, past=past)
    blk = lambda shape: pl.BlockSpec(shape, lambda i: (i, 0, 0))
    return pl.pallas_call(
        kern,
        grid=(nb // bb,),
        in_specs=[blk((bb, HQ, KV_W)), blk((bb, ncp, KV_W)), blk((bb, ncp, KV_W))],
        out_specs=[blk((bb, HQ, DH)), pl.BlockSpec((bb * KVH, nsp), lambda i: (i, 0))],
        out_shape=[jax.ShapeDtypeStruct((nb, HQ, DH), F32), jax.ShapeDtypeStruct((nb * KVH, nsp), F32)],
        compiler_params=_params(1),
        name="nsa_select_decode",
    )(q_bd, kc_tm, vc_tm)


def _nsa_attn_decode_kernel(*refs, n_pages, n_prev):
    q_ref, oc_ref, sel_ref, new_ref, newc_ref, g_ref = refs[1:7]
    k_pages = refs[7:7 + n_pages]
    v_pages = refs[7 + n_pages:7 + 2 * n_pages]
    pos = 7 + 2 * n_pages
    kwin_ref, vwin_ref = refs[pos:pos + 2]
    o_ref, kwin_out, vwin_out = refs[pos + 2 + n_prev:pos + 5 + n_prev]
    q = q_ref[0]
    qf = q.astype(F32)
    sel4 = sel_ref[0]
    sel_rows = jnp.concatenate([jnp.broadcast_to(sel4[k:k + 1], (GQ, sel4.shape[1])) for k in range(KVH)], axis=0)
    new = new_ref[0]
    first_half = lax.broadcasted_iota(jnp.int32, (1, PAGE_SIZE), 1) < SEL_BLK

    scores, masks = [], []
    for p in range(n_pages):
        scores.append(_dot(q, k_pages[p][0, 0].astype(MXU_DTYPE)))
        masks.append(jnp.where(first_half, sel_rows[:, 2 * p:2 * p + 1], sel_rows[:, 2 * p + 1:2 * p + 2]) > 0.5)
    own = 2 * n_pages
    s_new = jnp.sum(qf * new[0:1, :].astype(MXU_DTYPE).astype(F32), axis=-1, keepdims=True)
    ok_new = sel_rows[:, own:own + 1] > 0.5
    s_new = jnp.where(ok_new, s_new, NEG)
    m = s_new
    for s, ok in zip(scores, masks):
        m = jnp.maximum(m, jnp.max(jnp.where(ok, s, NEG), axis=-1, keepdims=True))
    e_new = jnp.where(ok_new, jnp.exp2(s_new - m), 0.0)
    den = e_new
    acc = e_new * new[1:2, :].astype(MXU_DTYPE).astype(F32)
    for p, (s, ok) in enumerate(zip(scores, masks)):
        e = jnp.where(ok, jnp.exp2(s - m), 0.0)
        den = den + jnp.sum(e, axis=-1, keepdims=True)
        acc = acc + _dot_nt(e.astype(MXU_DTYPE), v_pages[p][0, 0].astype(MXU_DTYPE))
    o_s = _block_diag_fold(acc * (1.0 / den))

    kwin, vwin = kwin_ref[0, 0], vwin_ref[0, 0]
    s_w = _dot(q, kwin.astype(MXU_DTYPE))
    s_wn = jnp.sum(qf * new[2:3, :].astype(MXU_DTYPE).astype(F32), axis=-1, keepdims=True)
    m_w = jnp.maximum(jnp.max(s_w, axis=-1, keepdims=True), s_wn)
    e_w, e_wn = jnp.exp2(s_w - m_w), jnp.exp2(s_wn - m_w)
    den_w = jnp.sum(e_w, axis=-1, keepdims=True) + e_wn
    acc_w = _dot_nt(e_w.astype(MXU_DTYPE), vwin.astype(MXU_DTYPE)) + e_wn * new[3:4, :].astype(MXU_DTYPE).astype(F32)
    o_w = _block_diag_fold(acc_w * (1.0 / den_w))

    g = g_ref[0]
    o_ref[0] = g[:, 0:1] * oc_ref[0] + g[:, 1:2] * o_s + g[:, 2:3] * o_w

    wb = kwin.shape[1]
    last = lax.broadcasted_iota(jnp.int32, (1, wb), 1) == wb - 1
    kwin_out[0, 0] = jnp.where(last, newc_ref[0, 0], pltpu.roll(kwin, wb - 1, 1))
    vwin_out[0, 0] = jnp.where(last, newc_ref[0, 1], pltpu.roll(vwin, wb - 1, 1))


def _nsa_attn_decode(page_table, q_bd, o_c, sel, new_rows, new_cols, gates, pool_k, pool_v, win_k, win_v,
                     li, prev):
    nb, n_pages = page_table.shape
    n_layers, _, _, wb = win_k.shape
    nsp = sel.shape[-1]
    per_b = lambda shape: pl.BlockSpec((1,) + shape, lambda b, pt: (b,) + (0,) * len(shape))
    page = lambda p: pl.BlockSpec((1, 1, KV_W, PAGE_SIZE), lambda b, pt: (li, pt[b, p], 0, 0))
    win = pl.BlockSpec((1, 1, KV_W, wb), lambda b, pt: (li, b, 0, 0))
    kern = functools.partial(_nsa_attn_decode_kernel, n_pages=n_pages, n_prev=len(prev))
    n_in = 7 + 2 * n_pages + 2
    win_shape = jax.ShapeDtypeStruct((n_layers, nb, KV_W, wb), F32)
    return pl.pallas_call(
        kern,
        grid_spec=pltpu.PrefetchScalarGridSpec(
            num_scalar_prefetch=1,
            grid=(nb,),
            in_specs=([per_b((HQ, KV_W)), per_b((HQ, DH)), per_b((KVH, nsp)), per_b((4, KV_W)),
                       per_b((2, KV_W, 1)), per_b((HQ, 3))]
                      + [page(p) for p in range(n_pages)] + [page(p) for p in range(n_pages)]
                      + [win, win] + _alias_specs(prev)),
            out_specs=[per_b((HQ, DH)), win, win],
        ),
        out_shape=[jax.ShapeDtypeStruct((nb, HQ, DH), F32), win_shape, win_shape],
        input_output_aliases={n_in + j: 1 + j for j in range(len(prev))},
        compiler_params=_params(1),
        name="nsa_attn_decode",
    )(page_table, q_bd, o_c, sel.reshape(nb, KVH, nsp), new_rows, new_cols, gates,
      *([pool_k] * n_pages), *([pool_v] * n_pages), win_k, win_v, *prev)


def _outproj_ln_kernel(x_ref, o_ref, z_ref, w_ref, g_ref, b_ref, y_ref, *, alpha):
    a = o_ref[...].astype(F32) * z_ref[...].astype(F32)
    h = alpha * x_ref[...] + _dot(a.astype(MXU_DTYPE), w_ref[...])
    d = h - jnp.mean(h, axis=-1, keepdims=True)
    var = jnp.mean(d * d, axis=-1, keepdims=True)
    y_ref[...] = d * lax.rsqrt(var + LN_EPS) * g_ref[...] + b_ref[...]


def _outproj_ln(x2d, o2d, z2d, w, ln_g, ln_b, alpha, tm):
    m, d = x2d.shape
    wdt = o2d.shape[1]
    row = lambda n: pl.BlockSpec((tm, n), lambda i: (i, 0))
    fixed = lambda shape: pl.BlockSpec(shape, lambda i: (0, 0))
    return pl.pallas_call(
        functools.partial(_outproj_ln_kernel, alpha=alpha),
        grid=(m // tm,),
        in_specs=[row(d), row(wdt), row(wdt), fixed((wdt, d)), fixed((1, d)), fixed((1, d))],
        out_specs=row(d),
        out_shape=jax.ShapeDtypeStruct((m, d), F32),
        compiler_params=_params(1),
        name="outproj_ln",
    )(x2d, o2d, z2d, w, ln_g.reshape(1, d), ln_b.reshape(1, d))


def _ret_inproj_kernel(x_ref, w_ref, c_ref, sa_ref, sb_ref, q_ref, k_ref, v_ref, z_ref, *, rdk):
    xb = x_ref[...].astype(MXU_DTYPE)
    c, sa, sb = c_ref[...], sa_ref[...], sb_ref[...]
    qk_w = RH * rdk
    col = 0
    for dst, scale in ((q_ref, 1.0), (k_ref, rdk ** -0.5)):
        for ch in range(qk_w // 512):
            a = _rotate(_dot(xb, w_ref[:, col:col + 512]), c, sa, sb, 1)
            dst[:, ch * 512:(ch + 1) * 512] = a * scale
            col += 512
    for dst, act in ((v_ref, lambda a: a), (z_ref, _silu)):
        for ch in range(2 * qk_w // 512):
            dst[:, ch * 512:(ch + 1) * 512] = act(_dot(xb, w_ref[:, col:col + 512])).astype(dst.dtype)
            col += 512


def _ret_inproj(x2d, w, tabs, t, tm):
    m, d = x2d.shape
    nt = t // tm
    rdk = d // RH
    row = lambda n: pl.BlockSpec((tm, n), lambda i: (i, 0))
    tab = pl.BlockSpec((tm, rdk), lambda i: (i % nt, 0))
    return pl.pallas_call(
        functools.partial(_ret_inproj_kernel, rdk=rdk),
        grid=(m // tm,),
        in_specs=[row(d), pl.BlockSpec(w.shape, lambda i: (0, 0)), tab, tab, tab],
        out_specs=[row(d), row(d), row(2 * d), row(2 * d)],
        out_shape=[jax.ShapeDtypeStruct((m, d), F32), jax.ShapeDtypeStruct((m, d), F32),
                   jax.ShapeDtypeStruct((m, 2 * d), F32), jax.ShapeDtypeStruct((m, 2 * d), MXU_DTYPE)],
        compiler_params=_params(1),
        name="ret_inproj",
    )(x2d, w, *tabs)


def _group_norm(o, gain):
    d = o - jnp.mean(o, axis=-1, keepdims=True)
    return d * lax.rsqrt(jnp.mean(d * d, axis=-1, keepdims=True) + LN_EPS) * gain


def _ret_prompt_kernel(q_ref, k_ref, v_ref, dm_ref, dec_ref, gn_ref, o_ref, s_ref, s_sc, *, rdk, rdv):
    ci = pl.program_id(1)

    @pl.when(ci == 0)
    def _():
        s_sc[...] = jnp.zeros(s_sc.shape, F32)

    heads = range(RH)
    q = [q_ref[0, :, h * rdk:(h + 1) * rdk] for h in heads]
    k = [k_ref[0, :, h * rdk:(h + 1) * rdk] for h in heads]
    v = [v_ref[0, :, h * rdv:(h + 1) * rdv].astype(MXU_DTYPE) for h in heads]
    dec = [dec_ref[h] for h in heads]
    s_old = [s_sc[h] for h in heads]
    inner = [_dot_nt(q[h].astype(MXU_DTYPE), k[h].astype(MXU_DTYPE)) for h in heads]
    cross = [_dot((q[h] * dec[h][:, 0:1]).astype(MXU_DTYPE), s_old[h].astype(MXU_DTYPE)) for h in heads]
    outer = [_dot_tn((k[h] * dec[h][:, 1:2]).astype(MXU_DTYPE), v[h]) for h in heads]
    intra = [_dot((inner[h] * dm_ref[h]).astype(MXU_DTYPE), v[h]) for h in heads]
    for h in heads:
        s_sc[h] = s_old[h] * dec[h][0:1, 2:3] + outer[h]
        o_ref[0, :, h * rdv:(h + 1) * rdv] = _group_norm(
            intra[h] + cross[h], gn_ref[:, h * rdv:(h + 1) * rdv]).astype(o_ref.dtype)

    @pl.when(ci == pl.num_programs(1) - 1)
    def _():
        s_ref[0] = s_sc[...]


def _ret_prompt(q, k, v, dmask, dec, gn_g):
    nb, t, qk_w = q.shape
    rdk = qk_w // RH
    rdv = v.shape[2] // RH
    c = dmask.shape[1]
    tok = lambda n: pl.BlockSpec((1, c, n), lambda b, i: (b, i, 0))
    full = lambda a: pl.BlockSpec(a.shape, lambda b, i: (0,) * a.ndim)
    gn_g = gn_g.reshape(1, RH * rdv)
    return pl.pallas_call(
        functools.partial(_ret_prompt_kernel, rdk=rdk, rdv=rdv),
        grid=(nb, t // c),
        in_specs=[tok(qk_w), tok(qk_w), tok(RH * rdv), full(dmask), full(dec), full(gn_g)],
        out_specs=[tok(RH * rdv), pl.BlockSpec((1, RH, rdk, rdv), lambda b, i: (b, 0, 0, 0))],
        out_shape=[jax.ShapeDtypeStruct((nb, t, RH * rdv), MXU_DTYPE),
                   jax.ShapeDtypeStruct((nb, RH, rdk, rdv), F32)],
        scratch_shapes=[pltpu.VMEM((RH, rdk, rdv), F32)],
        compiler_params=_params(2),
        name="ret_prompt",
    )(q, k, v, dmask, dec, gn_g)


def _ret_decode_kernel(dec_ref, q_ref, k_ref, v_ref, s0_ref, gn_ref, *rest, rdk, rdv):
    o_ref, s_ref = rest[-2:]
    pad = LANES
    first_row = lax.broadcasted_iota(jnp.int32, (pad, 1), 0) == 0
    for h in range(RH):
        q = q_ref[0, :, h * rdk:(h + 1) * rdk]
        k = k_ref[0, :, h * rdk:(h + 1) * rdk]
        v = v_ref[0, :, h * rdv:(h + 1) * rdv]
        dmask, q_dec, k_dec, c_dec = dec_ref[h, 0], dec_ref[h, 1], dec_ref[h, 2], dec_ref[h, 3]
        s0 = s0_ref[0, 0, h]
        k_pad = jnp.where(first_row, jnp.broadcast_to(k * k_dec, (pad, rdk)), 0.0).astype(MXU_DTYPE)
        v_pad = jnp.broadcast_to(v, (pad, rdv)).astype(MXU_DTYPE)
        s_ref[0, 0, h] = s0 * c_dec + _dot_tn(k_pad, v_pad)
        qk = jnp.sum(q.astype(MXU_DTYPE).astype(F32) * k.astype(MXU_DTYPE).astype(F32), axis=-1, keepdims=True)
        inner = (qk * dmask).astype(MXU_DTYPE).astype(F32)
        q_pad = jnp.broadcast_to(q * q_dec, (8, rdk)).astype(MXU_DTYPE)
        o = inner * v.astype(MXU_DTYPE).astype(F32) + _dot(q_pad, s0.astype(MXU_DTYPE))[0:1, :]
        o_ref[0, :, h * rdv:(h + 1) * rdv] = _group_norm(o, gn_ref[:, h * rdv:(h + 1) * rdv])


def _ret_decode(q, k, v, state, li, dec, gn_g, prev):
    nb, qk_w = q.shape
    rdk, rdv = qk_w // RH, v.shape[1] // RH
    vec = lambda n: pl.BlockSpec((1, 1, n), lambda b: (b, 0, 0))
    st_spec = pl.BlockSpec((1, 1, RH, rdk, rdv), lambda b: (li, b, 0, 0, 0))
    return pl.pallas_call(
        functools.partial(_ret_decode_kernel, rdk=rdk, rdv=rdv),
        grid=(nb,),
        in_specs=[pl.BlockSpec(memory_space=pltpu.SMEM), vec(qk_w), vec(qk_w), vec(RH * rdv), st_spec,
                  pl.BlockSpec((1, RH * rdv), lambda b: (0, 0))] + _alias_specs(prev),
        out_specs=[vec(RH * rdv), st_spec],
        out_shape=[jax.ShapeDtypeStruct((nb, 1, RH * rdv), F32), jax.ShapeDtypeStruct(state.shape, F32)],
        input_output_aliases={6 + j: 1 + j for j in range(len(prev))},
        compiler_params=_params(1),
        name="ret_decode",
    )(dec, q.reshape(nb, 1, qk_w), k.reshape(nb, 1, qk_w), v.reshape(nb, 1, RH * rdv), state,
      gn_g.reshape(1, RH * rdv), *prev)


def _rope_tables(pos):
    half = ROT_DIM // 2
    inv = ROPE_THETA ** (-jnp.arange(half, dtype=F32) / half)
    ang = pos.astype(F32)[:, None] * inv[None, :]
    lane = jnp.arange(LANES) % DH
    cos, sin = jnp.cos(ang)[:, lane % half], jnp.sin(ang)[:, lane % half]
    c = jnp.where(lane < ROT_DIM, cos, 1.0)
    sa = jnp.where((lane >= half) & (lane < ROT_DIM), sin, 0.0)
    sb = jnp.where(lane < half, -sin, 0.0)
    return c, sa, sb


def _xpos_tables(pos, rdk):
    half = rdk // 2
    inv = 1.0 / (XPOS_BASE ** jnp.linspace(0.0, 1.0, half, dtype=F32))
    ang = pos.astype(F32)[:, None] * inv[None, :]
    lane = jnp.arange(rdk)
    cos, sin = jnp.cos(ang)[:, lane // 2], jnp.sin(ang)[:, lane // 2]
    sa = jnp.where(lane % 2 == 1, sin, 0.0)
    sb = jnp.where(lane % 2 == 0, -sin, 0.0)
    return cos, sa, sb


def _decay_tables(c):
    log_g = jnp.log1p(-jnp.power(2.0, -5.0 - jnp.arange(RH, dtype=F32)))
    i = jnp.arange(c, dtype=F32)
    diff = i[:, None] - i[None, :]
    dmask = jnp.where(diff >= 0, jnp.exp(jnp.maximum(diff, 0.0)[None] * log_g[:, None, None]), 0.0)
    q_dec = jnp.exp((i + 1.0)[None, :] * log_g[:, None])
    k_dec = jnp.exp((c - 1.0 - i)[None, :] * log_g[:, None])
    c_dec = jnp.exp(c * log_g)
    return dmask, q_dec, k_dec, c_dec


def _nsa_w_in_layout(w):
    kv_end = NSA_W + 6 * KV_W
    gates = jnp.pad(w[:, kv_end:kv_end + 3 * HQ], ((0, 0), (0, GATE_PAD - 3 * HQ)))
    return jnp.concatenate([w[:, :kv_end], gates, w[:, kv_end + 3 * HQ:]], axis=1).astype(MXU_DTYPE)


def _channel_major(cache):
    return jnp.moveaxis(cache, -3, -1).reshape(cache.shape[:-3] + (KV_W, cache.shape[-3]))


def _token_major(slab):
    return jnp.moveaxis(slab.reshape(slab.shape[:-2] + (KVH, DH, slab.shape[-1])), -1, -3)


def _row_tile(m):
    for tm in (256, 128):
        if m % tm == 0:
            return tm
    raise ValueError(f"row count {m} is not a multiple of {LANES}")


def _nsa_layer(xp, xs, li, n_layers, page_table, pools, wins, w_in, w_out, cmp_k, cmp_v, ln_g, ln_b, alpha,
               prev_p, prev_s, prev_win):
    nb, t, d = xp.shape
    nd = xs.shape[0]
    past = page_table.shape[1] * PAGE_SIZE
    assert xs.shape[1] == 1 and t % SEL_TILE == 0 and t >= WIN_TILE
    w_in = _nsa_w_in_layout(w_in)
    w_out = w_out.astype(MXU_DTYPE)

    xp2 = xp.reshape(nb * t, d)
    tm = _row_tile(t)
    q_hm, kv, ks_aug, kw_hm, v_t, g_t, z, *cache_p = _nsa_inproj(xp2, w_in, _rope_tables(jnp.arange(t)), nb, t, tm,
                                                        li, n_layers, prev_p)
    kv = kv.reshape(6, nb, t, KV_W)
    kc_hm = _compress_prompt(kv[0], cmp_k, "head_major")
    vc_t = _compress_prompt(kv[1], cmp_v, "channel_major")
    o = _nsa_attn_prompt(q_hm, kc_hm, vc_t, ks_aug, kw_hm, v_t, g_t)
    yp = _outproj_ln(xp2, o.reshape(nb * t, NSA_W), z, w_out, ln_g, ln_b, alpha, tm).reshape(nb, t, d)

    xs2 = xs.reshape(nd, d)
    tms = _row_tile(nd)
    pos_s = jnp.full((nd,), past, jnp.int32)
    q_s, kv_s, _, _, _, g_s, z_s, *cache_s = _nsa_inproj(xs2, w_in, _rope_tables(pos_s), 1, nd, tms,
                                                      li, n_layers, prev_s)
    q_tok = q_s[0].transpose(1, 0, 2).astype(F32).reshape(nd, KVH, GQ, 1, DH)
    q_bd = (q_tok * jnp.eye(KVH, dtype=F32)[None, :, None, :, None]).reshape(nd, HQ, KV_W).astype(MXU_DTYPE)
    kc_tm = _compress_paged(pools[0], li, page_table, cmp_k)
    vc_tm = _compress_paged(pools[1], li, page_table, cmp_v)
    o_c, sel = _nsa_select(q_bd, kc_tm, vc_tm, past, min(16, nd))
    new_rows = jnp.stack([kv_s[2], kv_s[3], kv_s[4], kv_s[5]], axis=1)
    new_cols = jnp.stack([kv_s[4], kv_s[5]], axis=1)[..., None]
    gates_s = g_s[0, :3 * HQ].reshape(3, HQ, nd).transpose(2, 1, 0)
    o_s, *win_new = _nsa_attn_decode(page_table, q_bd, o_c, sel, new_rows, new_cols, gates_s,
                                     pools[2], pools[3], wins[0], wins[1], li, prev_win)
    ys = _outproj_ln(xs2, o_s.reshape(nd, NSA_W), z_s, w_out, ln_g, ln_b, alpha, tms).reshape(nd, 1, d)
    return yp, ys, cache_p, cache_s, win_new


def _ret_layer(xp, xs, li, state, past, w_in, gn_g, w_out, ln_g, ln_b, alpha, prev_state):
    nb, t, d = xp.shape
    nd = xs.shape[0]
    rdk = d // RH
    assert xs.shape[1] == 1 and t % RET_CHUNK == 0
    w_in = w_in.astype(MXU_DTYPE)
    w_out = w_out.astype(MXU_DTYPE)

    xp2 = xp.reshape(nb * t, d)
    tm = _row_tile(t)
    q, k, v, z = _ret_inproj(xp2, w_in, _xpos_tables(jnp.arange(t), rdk), t, tm)
    dmask, q_dec, k_dec, c_dec = _decay_tables(RET_CHUNK)
    dec = jnp.stack([q_dec, k_dec, jnp.broadcast_to(c_dec[:, None], q_dec.shape)]
                    + [jnp.zeros_like(q_dec)] * 5, axis=-1)
    o, sp = _ret_prompt(q.reshape(nb, t, d), k.reshape(nb, t, d), v.reshape(nb, t, 2 * d), dmask, dec, gn_g)
    yp = _outproj_ln(xp2, o.reshape(nb * t, 2 * d), z, w_out, ln_g, ln_b, alpha, tm).reshape(nb, t, d)

    xs2 = xs.reshape(nd, d)
    tms = _row_tile(nd)
    q, k, v, z = _ret_inproj(xs2, w_in, _xpos_tables(jnp.full((nd,), past, jnp.int32), rdk), nd, tms)
    dmask, q_dec, k_dec, c_dec = _decay_tables(1)
    dec = jnp.stack([dmask[:, 0, 0], q_dec[:, 0], k_dec[:, 0], c_dec], axis=-1)
    o, ss = _ret_decode(q, k, v, state, li, dec, gn_g, prev_state)
    ys = _outproj_ln(xs2, o.reshape(nd, 2 * d), z, w_out, ln_g, ln_b, alpha, tms).reshape(nd, 1, d)
    return yp, ys, sp, ss


def kernel(x_prompt, x_sample, cache_k_cmp, cache_v_cmp, cache_k_sel, cache_v_sel, cache_k_win, cache_v_win,
           state_ret, page_table, nsa_w_in, nsa_w_out, nsa_pe_k, nsa_w1_k, nsa_w2_k, nsa_pe_v, nsa_w1_v,
           nsa_w2_v, ret_w_in, ret_gn_g, ret_w_out, ln_g, ln_b):
    depth = ln_g.shape[0]
    alpha = (2.0 * depth) ** 0.25
    n_nsa = nsa_w_in.shape[0]
    past = page_table.shape[1] * PAGE_SIZE
    t = x_prompt.shape[1]
    pools = [_channel_major(c) for c in (cache_k_cmp, cache_v_cmp, cache_k_sel, cache_v_sel)]
    wins = [_channel_major(c) for c in (cache_k_win, cache_v_win)]
    xp, xs = x_prompt, x_sample
    cache_p, cache_s, win_new, ret_p, ret_s = [], [], [], [], []
    for i in range(depth):
        li = i // 2
        if i % 2 == 0:
            cmp_k = _compress_weights(nsa_pe_k[li], nsa_w1_k[li], nsa_w2_k[li])
            cmp_v = _compress_weights(nsa_pe_v[li], nsa_w1_v[li], nsa_w2_v[li])
            xp, xs, cache_p, cache_s, win_new = _nsa_layer(
                xp, xs, li, n_nsa, page_table, pools, wins, nsa_w_in[li], nsa_w_out[li], cmp_k, cmp_v,
                ln_g[i], ln_b[i], alpha, cache_p, cache_s, win_new)
        else:
            xp, xs, sp, ss = _ret_layer(xp, xs, li, state_ret, past, ret_w_in[li], ret_gn_g[li], ret_w_out[li],
                                        ln_g[i], ln_b[i], alpha, ret_s)
            ret_p.append(sp)
            ret_s = [ss]
    wb = min(WINDOW, t)
    out_p = [_token_major(c) for c in cache_p[:4]] + [_token_major(c[..., t - wb:]) for c in cache_p[4:]]
    out_s = [jnp.moveaxis(_token_major(c), 2, 1) for c in cache_s[:4]] + [_token_major(w) for w in win_new]
    return (xp, xs,
            out_p[0], out_s[0], out_p[1], out_s[1], out_p[2], out_s[2], out_p[3], out_s[3],
            out_p[4], out_s[4], out_p[5], out_s[5],
            jnp.stack(ret_p), ret_s[0])
```

```python
import functools

import jax
import jax.numpy as jnp
from jax import lax
from jax.experimental import pallas as pl
from jax.experimental.pallas import tpu as pltpu

F32 = jnp.float32
MXU_DTYPE = jnp.bfloat16

HQ, KVH, DH = 32, 4, 64
GQ = HQ // KVH
NSA_W, KV_W = HQ * DH, KVH * DH
ROT_DIM, ROPE_THETA = DH // 4, 500000.0
CMP_LEN, CMP_STRIDE, CMP_HID = 32, 16, 2 * DH
SEL_BLK, N_SEL, WINDOW, QBLK = 64, 16, 512, 128
FORCE_BONUS, NEG = 1000.0, -1.0e30
RH, RET_CHUNK, XPOS_BASE = 4, 128, 10000.0
PAGE_SIZE, LN_EPS = 128, 1e-5

LANES = 128
VMEM_LIMIT_BYTES = 56 * 1024 * 1024
CMP_GROUP = 2048
GATE_PAD = LANES
KEY_TILE = 2 * SEL_BLK
SEL_TILE = 4 * KEY_TILE
BLOCKS_PER_TILE = SEL_TILE // SEL_BLK
WIN_TILE = WINDOW + QBLK
ONES_ROWS = 16
LOG2E = 1.4426950408889634
REMOVED = -3.0e38
RUNNING_MAX_INIT = -1.0e29


def _params(n_axes):
    return pltpu.CompilerParams(dimension_semantics=("arbitrary",) * n_axes,
                                vmem_limit_bytes=VMEM_LIMIT_BYTES)


def _dot(a, b):
    return jnp.dot(a, b, preferred_element_type=F32)


def _dot_nt(a, b):
    return lax.dot_general(a, b, (((1,), (1,)), ((), ())), preferred_element_type=F32)


def _dot_tn(a, b):
    return lax.dot_general(a, b, (((0,), (0,)), ((), ())), preferred_element_type=F32)


def _silu(z):
    return z * (1.0 / (1.0 + jnp.exp(-z)))


def _lane_tile(t, reps):
    return t if reps == 1 else jnp.concatenate([t] * reps, axis=1)


def _rotate(a, c, sa, sb, shift):
    n = a.shape[1]
    reps = n // c.shape[1]
    c, sa, sb = _lane_tile(c, reps), _lane_tile(sa, reps), _lane_tile(sb, reps)
    return a * c + pltpu.roll(a, shift, 1) * sa + pltpu.roll(a, n - shift, 1) * sb


def _masked_softmax(s, mask):
    s = jnp.where(mask, s, NEG)
    m = jnp.max(s, axis=-1, keepdims=True)
    e = jnp.where(mask, jnp.exp2(s - m), 0.0)
    d = jnp.sum(e, axis=-1, keepdims=True)
    return e * (1.0 / jnp.where(d > 0.0, d, 1.0))


def _split3(x):
    hi = x.astype(MXU_DTYPE)
    r = x - hi.astype(F32)
    mid = r.astype(MXU_DTYPE)
    lo = (r - mid.astype(F32)).astype(MXU_DTYPE)
    return hi, mid, lo


def _topk_mask(score, idx, n_pick, axis):
    big = jnp.float32(score.shape[axis])
    for _ in range(n_pick):
        m = jnp.max(score, axis=axis, keepdims=True)
        first = jnp.min(jnp.where(score == m, idx, big), axis=axis, keepdims=True)
        score = jnp.where(idx == first, REMOVED, score)
    return (score == REMOVED).astype(F32)


def _overlap(c_idx, s_idx):
    cs, ss = c_idx * CMP_STRIDE, s_idx * SEL_BLK
    ov = jnp.minimum(cs + CMP_LEN, ss + SEL_BLK) - jnp.maximum(cs, ss)
    return jnp.maximum(ov, 0).astype(F32) * (1.0 / CMP_LEN)


def _alias_specs(prev):
    return [pl.BlockSpec(memory_space=pl.ANY)] * len(prev)


def _nsa_inproj_kernel(x_ref, w_ref, c_ref, sa_ref, sb_ref, *rest):
    q_ref, kv_ref, ks_ref, kw_ref, vt_ref, gt_ref, z_ref = rest[-13:-6]
    cache_refs = rest[-6:]
    xb = x_ref[...].astype(MXU_DTYPE)
    c, sa, sb = c_ref[...], sa_ref[...], sb_ref[...]
    half = ROT_DIM // 2
    tm = x_ref.shape[0]
    token = pl.program_id(0) * tm + lax.broadcasted_iota(jnp.int32, (tm, DH), 0)
    blk_onehot = ((token // SEL_BLK) % BLOCKS_PER_TILE
                  == lax.broadcasted_iota(jnp.int32, (tm, DH), 1)).astype(F32)
    col = 0
    for ch in range(NSA_W // 512):
        a = _dot(xb, w_ref[:, col:col + 512])
        a = _rotate(a, c, sa, sb, half) * (DH ** -0.5 * LOG2E)
        for h in range(512 // DH):
            q_ref[0, ch * (512 // DH) + h] = a[:, h * DH:(h + 1) * DH].astype(q_ref.dtype)
        col += 512
    for j in range(6):
        a = _dot(xb, w_ref[:, col:col + KV_W])
        if j % 2 == 0:
            a = _rotate(a, c, sa, sb, half)
        kv_ref[j] = a
        a_t = a.T
        cache_refs[j][0, 0] = a_t
        if j == 2:
            for h in range(KVH):
                ks_ref[0, h] = jnp.concatenate([a[:, h * DH:(h + 1) * DH], blk_onehot], axis=1).astype(ks_ref.dtype)
        if j == 4:
            for h in range(KVH):
                kw_ref[0, h] = a[:, h * DH:(h + 1) * DH].astype(kw_ref.dtype)
        if j in (3, 5):
            vt_ref[j // 2 - 1, 0] = a_t.astype(vt_ref.dtype)
        col += KV_W
    a = _dot(xb, w_ref[:, col:col + GATE_PAD])
    gt_ref[0] = (1.0 / (1.0 + jnp.exp(-a))).T
    col += GATE_PAD
    for ch in range(NSA_W // 512):
        z_ref[:, ch * 512:(ch + 1) * 512] = _silu(_dot(xb, w_ref[:, col:col + 512])).astype(z_ref.dtype)
        col += 512


def _nsa_inproj(x2d, w, tabs, nb, t, tm, li, n_layers, prev):
    m, d = x2d.shape
    nt = t // tm
    n_cols = w.shape[1]
    row = lambda i: (i, 0)
    tab = pl.BlockSpec((tm, LANES), lambda i: (i % nt, 0))
    cache_spec = pl.BlockSpec((1, 1, KV_W, tm), lambda i: (li, i // nt, 0, i % nt))
    cache_shape = jax.ShapeDtypeStruct((n_layers, nb, KV_W, t), F32)
    n_in = 5
    return pl.pallas_call(
        _nsa_inproj_kernel,
        grid=(m // tm,),
        in_specs=[pl.BlockSpec((tm, d), row), pl.BlockSpec((d, n_cols), lambda i: (0, 0)), tab, tab, tab]
        + _alias_specs(prev),
        out_specs=[
            pl.BlockSpec((1, HQ, tm, DH), lambda i: (i // nt, 0, i % nt, 0)),
            pl.BlockSpec((6, tm, KV_W), lambda i: (0, i, 0)),
            pl.BlockSpec((1, KVH, tm, 2 * DH), lambda i: (i // nt, 0, i % nt, 0)),
            pl.BlockSpec((1, KVH, tm, DH), lambda i: (i // nt, 0, i % nt, 0)),
            pl.BlockSpec((2, 1, KV_W, tm), lambda i: (0, i // nt, 0, i % nt)),
            pl.BlockSpec((1, GATE_PAD, tm), lambda i: (i // nt, 0, i % nt)),
            pl.BlockSpec((tm, NSA_W), row),
        ] + [cache_spec] * 6,
        out_shape=[
            jax.ShapeDtypeStruct((nb, HQ, t, DH), MXU_DTYPE),
            jax.ShapeDtypeStruct((6, m, KV_W), F32),
            jax.ShapeDtypeStruct((nb, KVH, t, 2 * DH), MXU_DTYPE),
            jax.ShapeDtypeStruct((nb, KVH, t, DH), MXU_DTYPE),
            jax.ShapeDtypeStruct((2, nb, KV_W, t), MXU_DTYPE),
            jax.ShapeDtypeStruct((nb, GATE_PAD, t), F32),
            jax.ShapeDtypeStruct((m, NSA_W), MXU_DTYPE),
        ] + [cache_shape] * 6,
        input_output_aliases={n_in + j: 7 + j for j in range(len(prev))},
        compiler_params=_params(1),
        name="nsa_inproj",
    )(x2d, w, *tabs, *prev)


def _compress_kernel(*refs, n_prefetch, n_pages, layout, n_chunks):
    split = KV_W // LANES
    pos = n_prefetch
    if n_pages:
        page_refs = refs[pos:pos + n_pages]
        pos += n_pages
    else:
        tok_ref, halo_ref = refs[pos:pos + 2]
        pos += 2
    wpair_ref, pe_ref, w1_ref, w2_ref, out_ref, r_ref = refs[pos:pos + 6]
    r_ref[:, n_chunks:, :] = jnp.zeros((split, 8, CMP_STRIDE * LANES), F32)
    if n_pages:
        tok_sc = refs[pos + 6]
        for p, ref in enumerate(page_refs):
            page = ref[0, 0].T
            for s in range(split):
                tok_sc[s, p * PAGE_SIZE:(p + 1) * PAGE_SIZE, :] = page[:, s * LANES:(s + 1) * LANES]
        for s in range(split):
            for l in range(CMP_STRIDE):
                r_ref[s, 0:n_chunks, l * LANES:(l + 1) * LANES] = tok_sc[s, pl.ds(l, n_chunks, stride=CMP_STRIDE), :]
    else:
        for s in range(split):
            for l in range(CMP_STRIDE):
                lanes = slice(l * LANES, (l + 1) * LANES)
                j = l * split + s
                r_ref[s, 0:n_chunks, lanes] = tok_ref[0, pl.ds(j, n_chunks, stride=CMP_STRIDE * split), :]
                r_ref[s, n_chunks:n_chunks + 1, lanes] = halo_ref[0, j:j + 1, :]
    heads_per_tile = LANES // DH
    ab = [_dot(r_ref[s].astype(MXU_DTYPE), wpair_ref[...]) for s in range(split)]
    pe8 = jnp.broadcast_to(pe_ref[...], (8, CMP_LEN * DH)).astype(MXU_DTYPE)
    bias = _dot(pe8, w1_ref[...])[0:1, :]
    w2 = w2_ref[...]
    toks = []
    for h in range(KVH):
        base = (h % heads_per_tile) * 2 * CMP_HID
        first_half = ab[h // heads_per_tile][0:n_chunks, base:base + CMP_HID]
        second_half = ab[h // heads_per_tile][1:n_chunks + 1, base + CMP_HID:base + 2 * CMP_HID]
        u = first_half + second_half + bias
        hid = 0.5 * u * (1.0 + jnp.tanh(0.7978845608028654 * (u + 0.044715 * (u * u * u))))
        toks.append(_dot(hid.astype(MXU_DTYPE), w2))
    if layout == "head_major":
        for h in range(KVH):
            out_ref[0, h] = toks[h].astype(out_ref.dtype)
    elif layout == "token_major":
        out_ref[0] = jnp.concatenate(toks, axis=1).astype(out_ref.dtype)
    else:
        out_ref[0] = jnp.concatenate(toks, axis=1).T.astype(out_ref.dtype)


def _compress_weights(pe, w1, w2):
    heads = LANES // DH
    w1r = w1.reshape(2, CMP_STRIDE, DH, CMP_HID)
    eye = jnp.eye(heads, dtype=w1.dtype)
    wpair = jnp.einsum("sldn,hg->lhdgsn", w1r, eye).reshape(CMP_STRIDE * LANES, heads * 2 * CMP_HID)
    return (wpair.astype(MXU_DTYPE), pe.reshape(1, CMP_LEN * DH), w1.astype(MXU_DTYPE), w2.astype(MXU_DTYPE))


def _compress_prompt(k_tok, weights, layout):
    nb, t, _ = k_tok.shape
    gt = min(CMP_GROUP, t)
    n_chunks = gt // CMP_STRIDE
    ng = t // gt
    last_halo = t // CMP_STRIDE - 1
    wpair, pe, w1, w2 = weights
    split = KV_W // LANES
    k_tok = k_tok.reshape(nb, t * split, LANES)
    full = lambda a: pl.BlockSpec(a.shape, lambda b, g: (0,) * a.ndim)
    kern = functools.partial(_compress_kernel, n_prefetch=0, n_pages=0, layout=layout, n_chunks=n_chunks)
    if layout == "head_major":
        out_spec = pl.BlockSpec((1, KVH, n_chunks, DH), lambda b, g: (b, 0, g, 0))
        out_shape = jax.ShapeDtypeStruct((nb, KVH, t // CMP_STRIDE, DH), MXU_DTYPE)
    else:
        out_spec = pl.BlockSpec((1, KV_W, n_chunks), lambda b, g: (b, 0, g))
        out_shape = jax.ShapeDtypeStruct((nb, KV_W, t // CMP_STRIDE), MXU_DTYPE)
    return pl.pallas_call(
        kern,
        grid=(nb, ng),
        in_specs=[pl.BlockSpec((1, gt * split, LANES), lambda b, g: (b, g, 0)),
                  pl.BlockSpec((1, CMP_STRIDE * split, LANES),
                               lambda b, g: (b, jnp.minimum((g + 1) * n_chunks, last_halo), 0)),
                  full(wpair), full(pe), full(w1), full(w2)],
        out_specs=out_spec,
        out_shape=out_shape,
        scratch_shapes=[pltpu.VMEM((split, n_chunks + 8, CMP_STRIDE * LANES), F32)],
        compiler_params=_params(2),
        name="nsa_compress_prompt",
    )(k_tok, k_tok, wpair, pe, w1, w2)


def _compress_paged_kernel(*refs, n_pages, n_chunks):
    n_in = n_pages + 4
    ins = refs[1:1 + 2 * n_in]
    outs = refs[1 + 2 * n_in:3 + 2 * n_in]
    scratch = refs[3 + 2 * n_in:]
    for s in range(2):
        _compress_kernel(*ins[s * n_in:(s + 1) * n_in], outs[s], *scratch[2 * s:2 * s + 2],
                         n_prefetch=0, n_pages=n_pages, layout="token_major", n_chunks=n_chunks)


def _compress_paged(pool_k, pool_v, li, page_table, weights_k, weights_v):
    nb, n_pages = page_table.shape
    n_chunks = n_pages * PAGE_SIZE // CMP_STRIDE
    split = KV_W // LANES
    full = lambda a: pl.BlockSpec(a.shape, lambda b, pt: (0,) * a.ndim)
    page = lambda p: pl.BlockSpec((1, 1, KV_W, PAGE_SIZE), lambda b, pt: (li, pt[b, p], 0, 0))
    stream_specs = lambda w: [page(p) for p in range(n_pages)] + [full(a) for a in w]
    out_spec = pl.BlockSpec((1, n_chunks, KV_W), lambda b, pt: (b, 0, 0))
    out_shape = jax.ShapeDtypeStruct((nb, n_chunks, KV_W), MXU_DTYPE)
    stream_scratch = [pltpu.VMEM((split, n_chunks + 8, CMP_STRIDE * LANES), F32),
                      pltpu.VMEM((split, n_pages * PAGE_SIZE, LANES), F32)]
    return pl.pallas_call(
        functools.partial(_compress_paged_kernel, n_pages=n_pages, n_chunks=n_chunks),
        grid_spec=pltpu.PrefetchScalarGridSpec(
            num_scalar_prefetch=1,
            grid=(nb,),
            in_specs=stream_specs(weights_k) + stream_specs(weights_v),
            out_specs=[out_spec, out_spec],
            scratch_shapes=stream_scratch + stream_scratch,
        ),
        out_shape=[out_shape, out_shape],
        compiler_params=_params(1),
        name="nsa_compress_paged",
    )(page_table, *([pool_k] * n_pages), *weights_k, *([pool_v] * n_pages), *weights_v)


def _nsa_attn_prompt_kernel(q_ref, kc_ref, vct_ref, ks_ref, vst_ref, kw_ref, vwt_ref, gc_ref, gs_ref, gw_ref,
                            o_ref, ot_sc, m_sc, alpha_sc, acc_sc, p_sc, s_sc, lanes_sc, *, seq_len):
    st = pl.program_id(2) * QBLK
    ncp = kc_ref.shape[2]
    ns = seq_len // SEL_BLK
    n_pairs = GQ // 2
    qpos = st + lax.broadcasted_iota(jnp.int32, (1, QBLK), 1)
    q_pairs = [q_ref[0, 2 * j:2 * j + 2].reshape(2 * QBLK, DH) for j in range(n_pairs)]
    both = lambda b: jnp.concatenate([b, b], axis=1)

    c_row = lax.broadcasted_iota(jnp.int32, (ncp, 1), 0)
    bias_c = both(jnp.where(c_row * CMP_STRIDE + (CMP_LEN - 1) <= qpos, 0.0, NEG))
    sees_any = both(qpos >= CMP_LEN - 1)
    kc, vct = kc_ref[0, 0], vct_ref[0]
    ws = pl.multiple_of(jnp.maximum(st - WINDOW, 0), KEY_TILE)
    w_row = lax.broadcasted_iota(jnp.int32, (WIN_TILE, 1), 0)
    dist = qpos - (ws + w_row)
    bias_w = both(jnp.where((dist >= 0) & (dist <= WINDOW), 0.0, NEG))
    kw_tile = kw_ref[0, 0, pl.ds(ws, WIN_TILE), :]
    vwt_tile = jnp.concatenate([vwt_ref[0, 0, :, pl.ds(ws, WIN_TILE)], jnp.ones((ONES_ROWS, WIN_TILE), MXU_DTYPE)],
                               axis=0)
    c_scores = [_dot_nt(kc, q_pairs[j]) for j in range(n_pairs)]
    w_scores = [_dot_nt(kw_tile, q_pairs[j]) for j in range(n_pairs)]
    ks_first = ks_ref[0, 0, 0:SEL_TILE, 0:DH]
    for j in range(n_pairs):
        s_sc[j] = _dot_nt(ks_first, q_pairs[j])
    p_sum = jnp.zeros((ncp, QBLK), F32)
    c_probs, w_probs = [], []
    for j in range(n_pairs):
        s = c_scores[j] + bias_c
        e = jnp.exp2(s - jnp.max(s, axis=0, keepdims=True))
        den = jnp.sum(e, axis=0, keepdims=True)
        p = e * jnp.where(sees_any, 1.0 / den, 0.0)
        c_probs.append(p.astype(MXU_DTYPE))
        p_sum = p_sum + p[:, :QBLK] + p[:, QBLK:]
    for j in range(n_pairs):
        s = w_scores[j] + bias_w
        w_probs.append(jnp.exp2(s - jnp.max(s, axis=0, keepdims=True)).astype(MXU_DTYPE))
    o_c = [_dot(vct, p) for p in c_probs]
    o_w = [_dot(vwt_tile, p) for p in w_probs]
    o_w = [o[:DH] * (1.0 / o[DH:DH + 1]) for o in o_w]

    s_idx = lax.broadcasted_iota(jnp.int32, (ns, 1), 0)
    ov_t = _overlap(lax.broadcasted_iota(jnp.int32, (1, ncp), 1), s_idx).astype(MXU_DTYPE)
    imp_t = sum(_dot(ov_t, part) for part in _split3(p_sum))
    blk_q = qpos // SEL_BLK
    forced = (s_idx == 0) | (s_idx == blk_q) | (s_idx == blk_q - 1)
    valid = s_idx * SEL_BLK <= qpos
    score = jnp.where(valid, imp_t + FORCE_BONUS * forced.astype(F32), NEG)
    sel_t = _topk_mask(score, s_idx.astype(F32), min(N_SEL, ns), 0)
    sel_bias = jnp.where(valid & (sel_t > 0.5), 0.0, NEG)
    sel_bias_q = sel_bias.T.astype(MXU_DTYPE)

    n_tiles = (st + QBLK + SEL_TILE - 1) // SEL_TILE
    m_sc[...] = jnp.full(m_sc.shape, RUNNING_MAX_INIT, F32)
    acc_sc[...] = jnp.zeros(acc_sc.shape, F32)
    alpha_sc[...] = jnp.ones(alpha_sc.shape, F32)
    p_sc[...] = jnp.zeros(p_sc.shape, MXU_DTYPE)
    ones_rows = jnp.ones((ONES_ROWS, SEL_TILE), MXU_DTYPE)
    first_bias = both(jnp.concatenate(
        [jnp.broadcast_to(sel_bias[b:b + 1, :], (SEL_BLK, QBLK)) for b in range(BLOCKS_PER_TILE)], axis=0))
    for j in range(n_pairs):
        s_sc[j] = s_sc[j] + first_bias
    blk_row = lax.broadcasted_iota(jnp.int32, (ns, DH), 0)
    blk_lane = lax.broadcasted_iota(jnp.int32, (ns, DH), 1)

    def key_start(i):
        return pl.multiple_of(i * SEL_TILE, SEL_TILE)

    def fold(i):
        vt_tile = jnp.concatenate([vst_ref[0, 0, :, pl.ds(key_start(i), SEL_TILE)], ones_rows], axis=0)
        for j in range(n_pairs):
            acc_sc[j] = alpha_sc[j] * acc_sc[j] + _dot(vt_tile, p_sc[j])

    def bias_lanes(i):
        pick = ((blk_row - i * BLOCKS_PER_TILE == blk_lane) & (blk_lane < BLOCKS_PER_TILE)).astype(MXU_DTYPE)
        return _dot(sel_bias_q, pick).astype(MXU_DTYPE)

    def masked_queries(lanes):
        return [jnp.concatenate([jnp.concatenate([q_ref[0, 2 * j + u], lanes], axis=1) for u in range(2)], axis=0)
                for j in range(n_pairs)]

    def softmax(j, s):
        m_old = m_sc[j]
        m_new = jnp.maximum(m_old, jnp.max(s, axis=0, keepdims=True))
        p_sc[j] = jnp.exp2(s - m_new).astype(MXU_DTYPE)
        alpha_sc[j] = jnp.exp2(m_old - m_new)
        m_sc[j] = m_new

    lanes_sc[...] = bias_lanes(1)

    def step(i, carry):
        fold(jnp.maximum(i - 1, 0))
        k_ahead = ks_ref[0, 0, pl.ds(key_start(i + 1), SEL_TILE), :]
        q_ahead = masked_queries(lanes_sc[...])
        for j in range(n_pairs):
            ahead = _dot_nt(k_ahead, q_ahead[j])
            softmax(j, s_sc[j])
            s_sc[j] = ahead
        lanes_sc[...] = bias_lanes(i + 2)
        return carry

    lax.fori_loop(0, n_tiles - 1, step, 0)
    fold(jnp.maximum(n_tiles - 2, 0))
    k_row = (n_tiles - 1) * SEL_TILE + lax.broadcasted_iota(jnp.int32, (SEL_TILE, 1), 0)
    causal = both(jnp.where(k_row <= qpos, 0.0, NEG))
    for j in range(n_pairs):
        softmax(j, s_sc[j] + causal)
    fold(n_tiles - 1)
    o_s = [acc_sc[j, :DH] * (1.0 / acc_sc[j, DH:DH + 1]) for j in range(n_pairs)]
    gc, gs, gw = gc_ref[0], gs_ref[0], gw_ref[0]
    for h in range(GQ):
        j, cols = h // 2, slice((h % 2) * QBLK, (h % 2 + 1) * QBLK)
        ot_sc[h * DH:(h + 1) * DH, :] = (gc[h:h + 1] * o_c[j][:, cols] + gs[h:h + 1] * o_s[j][:, cols]
                                         + gw[h:h + 1] * o_w[j][:, cols])
    o_ref[0] = ot_sc[...].T.astype(o_ref.dtype)


def _nsa_attn_prompt(q_hm, kc_hm, vc_t, ks_aug, kw_hm, v_t, gates_t):
    nb, _, t, _ = q_hm.shape
    ncp = kc_hm.shape[2]
    k_spec = lambda width: pl.BlockSpec((1, 1, t, width), lambda b, h, i: (b, h, 0, 0))
    vt_spec = lambda j: pl.BlockSpec((1, 1, DH, t), lambda b, h, i: (j, b, h, 0))
    gate_spec = lambda br: pl.BlockSpec((1, GQ, QBLK), lambda b, h, i: (b, br * KVH + h, i))
    kern = functools.partial(_nsa_attn_prompt_kernel, seq_len=t)
    return pl.pallas_call(
        kern,
        grid=(nb, KVH, t // QBLK),
        in_specs=[pl.BlockSpec((1, GQ, QBLK, DH), lambda b, h, i: (b, h, i, 0)),
                  pl.BlockSpec((1, 1, ncp, DH), lambda b, h, i: (b, h, 0, 0)),
                  pl.BlockSpec((1, DH, ncp), lambda b, h, i: (b, h, 0)),
                  k_spec(2 * DH), vt_spec(0), k_spec(DH), vt_spec(1),
                  gate_spec(0), gate_spec(1), gate_spec(2)],
        out_specs=pl.BlockSpec((1, QBLK, GQ * DH), lambda b, h, i: (b, i, h)),
        out_shape=jax.ShapeDtypeStruct((nb, t, NSA_W), MXU_DTYPE),
        scratch_shapes=[pltpu.VMEM((GQ * DH, QBLK), F32),
                        pltpu.VMEM((GQ // 2, 1, 2 * QBLK), F32), pltpu.VMEM((GQ // 2, 1, 2 * QBLK), F32),
                        pltpu.VMEM((GQ // 2, DH + ONES_ROWS, 2 * QBLK), F32),
                        pltpu.VMEM((GQ // 2, SEL_TILE, 2 * QBLK), MXU_DTYPE),
                        pltpu.VMEM((GQ // 2, SEL_TILE, 2 * QBLK), F32),
                        pltpu.VMEM((QBLK, DH), MXU_DTYPE)],
        compiler_params=_params(3),
        name="nsa_attn_prompt",
    )(q_hm, kc_hm, vc_t, ks_aug, v_t, kw_hm, v_t, gates_t, gates_t, gates_t)


def _block_diag_fold(x):
    r_kv = lax.broadcasted_iota(jnp.int32, (HQ, KV_W), 0) // GQ
    l_kv = lax.broadcasted_iota(jnp.int32, (HQ, KV_W), 1) // DH
    x = jnp.where(r_kv == l_kv, x, 0.0)
    return sum(x[..., h * DH:(h + 1) * DH] for h in range(KVH))


def _nsa_select_kernel(q_ref, kc_ref, vc_ref, oc_ref, sel_ref, *, past):
    bb, ncp = kc_ref.shape[0], kc_ref.shape[1]
    nsp = sel_ref.shape[1]
    q = q_ref[...]
    s = jnp.einsum("bqd,bkd->bqk", q, kc_ref[...], preferred_element_type=F32)
    c_idx = lax.broadcasted_iota(jnp.int32, (1, 1, ncp), 2)
    p_c = _masked_softmax(s, c_idx * CMP_STRIDE + (CMP_LEN - 1) <= past)
    o_c = jnp.einsum("bqk,bkd->bqd", p_c.astype(MXU_DTYPE), vc_ref[...], preferred_element_type=F32)
    oc_ref[...] = _block_diag_fold(o_c)
    ov = _overlap(lax.broadcasted_iota(jnp.int32, (ncp, 1), 0),
                  lax.broadcasted_iota(jnp.int32, (1, nsp), 1)).astype(MXU_DTYPE)
    imp_rows = sum(_dot(part, ov) for part in _split3(p_c.reshape(bb * HQ, ncp)))
    imp = jnp.sum(imp_rows.reshape(bb * KVH, GQ, nsp), axis=1)
    s_idx = lax.broadcasted_iota(jnp.int32, (1, nsp), 1)
    blk_q = past // SEL_BLK
    forced = (s_idx == 0) | (s_idx == blk_q) | (s_idx == blk_q - 1)
    valid = s_idx * SEL_BLK <= past
    score = jnp.where(valid, imp + FORCE_BONUS * forced.astype(F32), NEG)
    n_blocks = past // SEL_BLK + 1
    sel = _topk_mask(score, s_idx.astype(F32), min(N_SEL, n_blocks), 1)
    sel_ref[...] = jnp.where(valid, sel, 0.0)


def _nsa_select(q_bd, kc_tm, vc_tm, past, bb):
    nb, ncp, _ = kc_tm.shape
    nsp = LANES
    assert past // SEL_BLK + 1 <= nsp
    kern = functools.partial(_nsa_select_kernel, past=past)
    blk = lambda shape: pl.BlockSpec(shape, lambda i: (i, 0, 0))
    return pl.pallas_call(
        kern,
        grid=(nb // bb,),
        in_specs=[blk((bb, HQ, KV_W)), blk((bb, ncp, KV_W)), blk((bb, ncp, KV_W))],
        out_specs=[blk((bb, HQ, DH)), pl.BlockSpec((bb * KVH, nsp), lambda i: (i, 0))],
        out_shape=[jax.ShapeDtypeStruct((nb, HQ, DH), F32), jax.ShapeDtypeStruct((nb * KVH, nsp), F32)],
        compiler_params=_params(1),
        name="nsa_select_decode",
    )(q_bd, kc_tm, vc_tm)


def _nsa_attn_decode_kernel(*refs, n_pages, n_prev):
    q_ref, oc_ref, sel_ref, new_ref, newc_ref, g_ref = refs[1:7]
    k_pages = refs[7:7 + n_pages]
    v_pages = refs[7 + n_pages:7 + 2 * n_pages]
    pos = 7 + 2 * n_pages
    kwin_ref, vwin_ref = refs[pos:pos + 2]
    o_ref, kwin_out, vwin_out = refs[pos + 2 + n_prev:pos + 5 + n_prev]
    q = q_ref[0]
    qf = q.astype(F32)
    sel4 = sel_ref[0]
    sel_rows = jnp.concatenate([jnp.broadcast_to(sel4[k:k + 1], (GQ, sel4.shape[1])) for k in range(KVH)], axis=0)
    new = new_ref[0]
    first_half = lax.broadcasted_iota(jnp.int32, (1, PAGE_SIZE), 1) < SEL_BLK

    scores, masks = [], []
    for p in range(n_pages):
        scores.append(_dot(q, k_pages[p][0, 0].astype(MXU_DTYPE)))
        masks.append(jnp.where(first_half, sel_rows[:, 2 * p:2 * p + 1], sel_rows[:, 2 * p + 1:2 * p + 2]) > 0.5)
    own = 2 * n_pages
    s_new = jnp.sum(qf * new[0:1, :].astype(MXU_DTYPE).astype(F32), axis=-1, keepdims=True)
    ok_new = sel_rows[:, own:own + 1] > 0.5
    s_new = jnp.where(ok_new, s_new, NEG)
    m = s_new
    for s, ok in zip(scores, masks):
        m = jnp.maximum(m, jnp.max(jnp.where(ok, s, NEG), axis=-1, keepdims=True))
    e_new = jnp.where(ok_new, jnp.exp2(s_new - m), 0.0)
    den = e_new
    acc = e_new * new[1:2, :].astype(MXU_DTYPE).astype(F32)
    for p, (s, ok) in enumerate(zip(scores, masks)):
        e = jnp.where(ok, jnp.exp2(s - m), 0.0)
        den = den + jnp.sum(e, axis=-1, keepdims=True)
        acc = acc + _dot_nt(e.astype(MXU_DTYPE), v_pages[p][0, 0].astype(MXU_DTYPE))
    o_s = _block_diag_fold(acc * (1.0 / den))

    kwin, vwin = kwin_ref[0, 0], vwin_ref[0, 0]
    s_w = _dot(q, kwin.astype(MXU_DTYPE))
    s_wn = jnp.sum(qf * new[2:3, :].astype(MXU_DTYPE).astype(F32), axis=-1, keepdims=True)
    m_w = jnp.maximum(jnp.max(s_w, axis=-1, keepdims=True), s_wn)
    e_w, e_wn = jnp.exp2(s_w - m_w), jnp.exp2(s_wn - m_w)
    den_w = jnp.sum(e_w, axis=-1, keepdims=True) + e_wn
    acc_w = _dot_nt(e_w.astype(MXU_DTYPE), vwin.astype(MXU_DTYPE)) + e_wn * new[3:4, :].astype(MXU_DTYPE).astype(F32)
    o_w = _block_diag_fold(acc_w * (1.0 / den_w))

    g = g_ref[0]
    o_ref[0] = g[:, 0:1] * oc_ref[0] + g[:, 1:2] * o_s + g[:, 2:3] * o_w

    wb = kwin.shape[1]
    last = lax.broadcasted_iota(jnp.int32, (1, wb), 1) == wb - 1
    kwin_out[0, 0] = jnp.where(last, newc_ref[0, 0], pltpu.roll(kwin, wb - 1, 1))
    vwin_out[0, 0] = jnp.where(last, newc_ref[0, 1], pltpu.roll(vwin, wb - 1, 1))


def _nsa_attn_decode(page_table, q_bd, o_c, sel, new_rows, new_cols, gates, pool_k, pool_v, win_k, win_v,
                     li, prev):
    nb, n_pages = page_table.shape
    n_layers, _, _, wb = win_k.shape
    nsp = sel.shape[-1]
    per_b = lambda shape: pl.BlockSpec((1,) + shape, lambda b, pt: (b,) + (0,) * len(shape))
    page = lambda p: pl.BlockSpec((1, 1, KV_W, PAGE_SIZE), lambda b, pt: (li, pt[b, p], 0, 0))
    win = pl.BlockSpec((1, 1, KV_W, wb), lambda b, pt: (li, b, 0, 0))
    kern = functools.partial(_nsa_attn_decode_kernel, n_pages=n_pages, n_prev=len(prev))
    n_in = 7 + 2 * n_pages + 2
    win_shape = jax.ShapeDtypeStruct((n_layers, nb, KV_W, wb), F32)
    return pl.pallas_call(
        kern,
        grid_spec=pltpu.PrefetchScalarGridSpec(
            num_scalar_prefetch=1,
            grid=(nb,),
            in_specs=([per_b((HQ, KV_W)), per_b((HQ, DH)), per_b((KVH, nsp)), per_b((4, KV_W)),
                       per_b((2, KV_W, 1)), per_b((HQ, 3))]
                      + [page(p) for p in range(n_pages)] + [page(p) for p in range(n_pages)]
                      + [win, win] + _alias_specs(prev)),
            out_specs=[per_b((HQ, DH)), win, win],
        ),
        out_shape=[jax.ShapeDtypeStruct((nb, HQ, DH), F32), win_shape, win_shape],
        input_output_aliases={n_in + j: 1 + j for j in range(len(prev))},
        compiler_params=_params(1),
        name="nsa_attn_decode",
    )(page_table, q_bd, o_c, sel.reshape(nb, KVH, nsp), new_rows, new_cols, gates,
      *([pool_k] * n_pages), *([pool_v] * n_pages), win_k, win_v, *prev)


def _outproj_ln_kernel(x_ref, o_ref, z_ref, w_ref, g_ref, b_ref, y_ref, *, alpha):
    a = o_ref[...].astype(F32) * z_ref[...].astype(F32)
    h = alpha * x_ref[...] + _dot(a.astype(MXU_DTYPE), w_ref[...])
    d = h - jnp.mean(h, axis=-1, keepdims=True)
    var = jnp.mean(d * d, axis=-1, keepdims=True)
    y_ref[...] = d * lax.rsqrt(var + LN_EPS) * g_ref[...] + b_ref[...]


def _outproj_ln(x2d, o2d, z2d, w, ln_g, ln_b, alpha, tm):
    m, d = x2d.shape
    wdt = o2d.shape[1]
    row = lambda n: pl.BlockSpec((tm, n), lambda i: (i, 0))
    fixed = lambda shape: pl.BlockSpec(shape, lambda i: (0, 0))
    return pl.pallas_call(
        functools.partial(_outproj_ln_kernel, alpha=alpha),
        grid=(m // tm,),
        in_specs=[row(d), row(wdt), row(wdt), fixed((wdt, d)), fixed((1, d)), fixed((1, d))],
        out_specs=row(d),
        out_shape=jax.ShapeDtypeStruct((m, d), F32),
        compiler_params=_params(1),
        name="outproj_ln",
    )(x2d, o2d, z2d, w, ln_g.reshape(1, d), ln_b.reshape(1, d))


def _ret_inproj_kernel(x_ref, w_ref, c_ref, sa_ref, sb_ref, q_ref, k_ref, v_ref, z_ref, *, rdk):
    xb = x_ref[...].astype(MXU_DTYPE)
    c, sa, sb = c_ref[...], sa_ref[...], sb_ref[...]
    qk_w = RH * rdk
    col = 0
    for dst, scale in ((q_ref, 1.0), (k_ref, rdk ** -0.5)):
        for ch in range(qk_w // 512):
            a = _rotate(_dot(xb, w_ref[:, col:col + 512]), c, sa, sb, 1)
            dst[:, ch * 512:(ch + 1) * 512] = a * scale
            col += 512
    for dst, act in ((v_ref, lambda a: a), (z_ref, _silu)):
        for ch in range(2 * qk_w // 512):
            dst[:, ch * 512:(ch + 1) * 512] = act(_dot(xb, w_ref[:, col:col + 512])).astype(dst.dtype)
            col += 512


def _ret_inproj(x2d, w, tabs, t, tm):
    m, d = x2d.shape
    nt = t // tm
    rdk = d // RH
    row = lambda n: pl.BlockSpec((tm, n), lambda i: (i, 0))
    tab = pl.BlockSpec((tm, rdk), lambda i: (i % nt, 0))
    return pl.pallas_call(
        functools.partial(_ret_inproj_kernel, rdk=rdk),
        grid=(m // tm,),
        in_specs=[row(d), pl.BlockSpec(w.shape, lambda i: (0, 0)), tab, tab, tab],
        out_specs=[row(d), row(d), row(2 * d), row(2 * d)],
        out_shape=[jax.ShapeDtypeStruct((m, d), F32), jax.ShapeDtypeStruct((m, d), F32),
                   jax.ShapeDtypeStruct((m, 2 * d), F32), jax.ShapeDtypeStruct((m, 2 * d), MXU_DTYPE)],
        compiler_params=_params(1),
        name="ret_inproj",
    )(x2d, w, *tabs)


def _group_norm(o, gain):
    d = o - jnp.mean(o, axis=-1, keepdims=True)
    return d * lax.rsqrt(jnp.mean(d * d, axis=-1, keepdims=True) + LN_EPS) * gain


def _ret_prompt_kernel(q_ref, k_ref, v_ref, dm_ref, dec_ref, gn_ref, o_ref, s_ref, s_sc, *, rdk, rdv):
    ci = pl.program_id(1)

    @pl.when(ci == 0)
    def _():
        s_sc[...] = jnp.zeros(s_sc.shape, F32)

    heads = range(RH)
    q = [q_ref[0, :, h * rdk:(h + 1) * rdk] for h in heads]
    k = [k_ref[0, :, h * rdk:(h + 1) * rdk] for h in heads]
    v = [v_ref[0, :, h * rdv:(h + 1) * rdv].astype(MXU_DTYPE) for h in heads]
    dec = [dec_ref[h] for h in heads]
    s_old = [s_sc[h] for h in heads]
    inner = [_dot_nt(q[h].astype(MXU_DTYPE), k[h].astype(MXU_DTYPE)) for h in heads]
    cross = [_dot((q[h] * dec[h][:, 0:1]).astype(MXU_DTYPE), s_old[h].astype(MXU_DTYPE)) for h in heads]
    outer = [_dot_tn((k[h] * dec[h][:, 1:2]).astype(MXU_DTYPE), v[h]) for h in heads]
    intra = [_dot((inner[h] * dm_ref[h]).astype(MXU_DTYPE), v[h]) for h in heads]
    for h in heads:
        s_sc[h] = s_old[h] * dec[h][0:1, 2:3] + outer[h]
        o_ref[0, :, h * rdv:(h + 1) * rdv] = _group_norm(
            intra[h] + cross[h], gn_ref[:, h * rdv:(h + 1) * rdv]).astype(o_ref.dtype)

    @pl.when(ci == pl.num_programs(1) - 1)
    def _():
        s_ref[0] = s_sc[...]


def _ret_prompt(q, k, v, dmask, dec, gn_g):
    nb, t, qk_w = q.shape
    rdk = qk_w // RH
    rdv = v.shape[2] // RH
    c = dmask.shape[1]
    tok = lambda n: pl.BlockSpec((1, c, n), lambda b, i: (b, i, 0))
    full = lambda a: pl.BlockSpec(a.shape, lambda b, i: (0,) * a.ndim)
    gn_g = gn_g.reshape(1, RH * rdv)
    return pl.pallas_call(
        functools.partial(_ret_prompt_kernel, rdk=rdk, rdv=rdv),
        grid=(nb, t // c),
        in_specs=[tok(qk_w), tok(qk_w), tok(RH * rdv), full(dmask), full(dec), full(gn_g)],
        out_specs=[tok(RH * rdv), pl.BlockSpec((1, RH, rdk, rdv), lambda b, i: (b, 0, 0, 0))],
        out_shape=[jax.ShapeDtypeStruct((nb, t, RH * rdv), MXU_DTYPE),
                   jax.ShapeDtypeStruct((nb, RH, rdk, rdv), F32)],
        scratch_shapes=[pltpu.VMEM((RH, rdk, rdv), F32)],
        compiler_params=_params(2),
        name="ret_prompt",
    )(q, k, v, dmask, dec, gn_g)


def _ret_decode_kernel(dec_ref, q_ref, k_ref, v_ref, s0_ref, gn_ref, *rest, rdk, rdv):
    o_ref, s_ref = rest[-2:]
    pad = LANES
    first_row = lax.broadcasted_iota(jnp.int32, (pad, 1), 0) == 0
    for h in range(RH):
        q = q_ref[0, :, h * rdk:(h + 1) * rdk]
        k = k_ref[0, :, h * rdk:(h + 1) * rdk]
        v = v_ref[0, :, h * rdv:(h + 1) * rdv]
        dmask, q_dec, k_dec, c_dec = dec_ref[h, 0], dec_ref[h, 1], dec_ref[h, 2], dec_ref[h, 3]
        s0 = s0_ref[0, 0, h]
        k_pad = jnp.where(first_row, jnp.broadcast_to(k * k_dec, (pad, rdk)), 0.0).astype(MXU_DTYPE)
        v_pad = jnp.broadcast_to(v, (pad, rdv)).astype(MXU_DTYPE)
        s_ref[0, 0, h] = s0 * c_dec + _dot_tn(k_pad, v_pad)
        qk = jnp.sum(q.astype(MXU_DTYPE).astype(F32) * k.astype(MXU_DTYPE).astype(F32), axis=-1, keepdims=True)
        inner = (qk * dmask).astype(MXU_DTYPE).astype(F32)
        q_pad = jnp.broadcast_to(q * q_dec, (8, rdk)).astype(MXU_DTYPE)
        o = inner * v.astype(MXU_DTYPE).astype(F32) + _dot(q_pad, s0.astype(MXU_DTYPE))[0:1, :]
        o_ref[0, :, h * rdv:(h + 1) * rdv] = _group_norm(o, gn_ref[:, h * rdv:(h + 1) * rdv])


def _ret_decode(q, k, v, state, li, dec, gn_g, prev):
    nb, qk_w = q.shape
    rdk, rdv = qk_w // RH, v.shape[1] // RH
    vec = lambda n: pl.BlockSpec((1, 1, n), lambda b: (b, 0, 0))
    st_spec = pl.BlockSpec((1, 1, RH, rdk, rdv), lambda b: (li, b, 0, 0, 0))
    return pl.pallas_call(
        functools.partial(_ret_decode_kernel, rdk=rdk, rdv=rdv),
        grid=(nb,),
        in_specs=[pl.BlockSpec(memory_space=pltpu.SMEM), vec(qk_w), vec(qk_w), vec(RH * rdv), st_spec,
                  pl.BlockSpec((1, RH * rdv), lambda b: (0, 0))] + _alias_specs(prev),
        out_specs=[vec(RH * rdv), st_spec],
        out_shape=[jax.ShapeDtypeStruct((nb, 1, RH * rdv), F32), jax.ShapeDtypeStruct(state.shape, F32)],
        input_output_aliases={6 + j: 1 + j for j in range(len(prev))},
        compiler_params=_params(1),
        name="ret_decode",
    )(dec, q.reshape(nb, 1, qk_w), k.reshape(nb, 1, qk_w), v.reshape(nb, 1, RH * rdv), state,
      gn_g.reshape(1, RH * rdv), *prev)


def _rope_tables(pos):
    half = ROT_DIM // 2
    inv = ROPE_THETA ** (-jnp.arange(half, dtype=F32) / half)
    ang = pos.astype(F32)[:, None] * inv[None, :]
    lane = jnp.arange(LANES) % DH
    cos, sin = jnp.cos(ang)[:, lane % half], jnp.sin(ang)[:, lane % half]
    c = jnp.where(lane < ROT_DIM, cos, 1.0)
    sa = jnp.where((lane >= half) & (lane < ROT_DIM), sin, 0.0)
    sb = jnp.where(lane < half, -sin, 0.0)
    return c, sa, sb


def _xpos_tables(pos, rdk):
    half = rdk // 2
    inv = 1.0 / (XPOS_BASE ** jnp.linspace(0.0, 1.0, half, dtype=F32))
    ang = pos.astype(F32)[:, None] * inv[None, :]
    lane = jnp.arange(rdk)
    cos, sin = jnp.cos(ang)[:, lane // 2], jnp.sin(ang)[:, lane // 2]
    sa = jnp.where(lane % 2 == 1, sin, 0.0)
    sb = jnp.where(lane % 2 == 0, -sin, 0.0)
    return cos, sa, sb


def _decay_tables(c):
    log_g = jnp.log1p(-jnp.power(2.0, -5.0 - jnp.arange(RH, dtype=F32)))
    i = jnp.arange(c, dtype=F32)
    diff = i[:, None] - i[None, :]
    dmask = jnp.where(diff >= 0, jnp.exp(jnp.maximum(diff, 0.0)[None] * log_g[:, None, None]), 0.0)
    q_dec = jnp.exp((i + 1.0)[None, :] * log_g[:, None])
    k_dec = jnp.exp((c - 1.0 - i)[None, :] * log_g[:, None])
    c_dec = jnp.exp(c * log_g)
    return dmask, q_dec, k_dec, c_dec


def _nsa_w_in_layout(w):
    kv_end = NSA_W + 6 * KV_W
    gates = jnp.pad(w[:, kv_end:kv_end + 3 * HQ], ((0, 0), (0, GATE_PAD - 3 * HQ)))
    return jnp.concatenate([w[:, :kv_end], gates, w[:, kv_end + 3 * HQ:]], axis=1).astype(MXU_DTYPE)


def _channel_major(cache):
    return jnp.moveaxis(cache, -3, -1).reshape(cache.shape[:-3] + (KV_W, cache.shape[-3]))


def _token_major(slab):
    return jnp.moveaxis(slab.reshape(slab.shape[:-2] + (KVH, DH, slab.shape[-1])), -1, -3)


def _row_tile(m):
    for tm in (256, 128):
        if m % tm == 0:
            return tm
    raise ValueError(f"row count {m} is not a multiple of {LANES}")


def _nsa_layer(xp, xs, li, n_layers, page_table, pools, wins, w_in, w_out, cmp_k, cmp_v, ln_g, ln_b, alpha,
               prev_p, prev_s, prev_win):
    nb, t, d = xp.shape
    nd = xs.shape[0]
    past = page_table.shape[1] * PAGE_SIZE
    assert xs.shape[1] == 1 and t % SEL_TILE == 0 and t >= WIN_TILE
    w_in = _nsa_w_in_layout(w_in)
    w_out = w_out.astype(MXU_DTYPE)

    xp2 = xp.reshape(nb * t, d)
    tm = _row_tile(t)
    q_hm, kv, ks_aug, kw_hm, v_t, g_t, z, *cache_p = _nsa_inproj(xp2, w_in, _rope_tables(jnp.arange(t)), nb, t, tm,
                                                        li, n_layers, prev_p)
    kv = kv.reshape(6, nb, t, KV_W)
    kc_hm = _compress_prompt(kv[0], cmp_k, "head_major")
    vc_t = _compress_prompt(kv[1], cmp_v, "channel_major")
    o = _nsa_attn_prompt(q_hm, kc_hm, vc_t, ks_aug, kw_hm, v_t, g_t)
    yp = _outproj_ln(xp2, o.reshape(nb * t, NSA_W), z, w_out, ln_g, ln_b, alpha, tm).reshape(nb, t, d)

    xs2 = xs.reshape(nd, d)
    tms = _row_tile(nd)
    pos_s = jnp.full((nd,), past, jnp.int32)
    q_s, kv_s, _, _, _, g_s, z_s, *cache_s = _nsa_inproj(xs2, w_in, _rope_tables(pos_s), 1, nd, tms,
                                                      li, n_layers, prev_s)
    q_tok = q_s[0].transpose(1, 0, 2).astype(F32).reshape(nd, KVH, GQ, 1, DH)
    q_bd = (q_tok * jnp.eye(KVH, dtype=F32)[None, :, None, :, None]).reshape(nd, HQ, KV_W).astype(MXU_DTYPE)
    kc_tm, vc_tm = _compress_paged(pools[0], pools[1], li, page_table, cmp_k, cmp_v)
    o_c, sel = _nsa_select(q_bd, kc_tm, vc_tm, past, min(16, nd))
    new_rows = jnp.stack([kv_s[2], kv_s[3], kv_s[4], kv_s[5]], axis=1)
    new_cols = jnp.stack([kv_s[4], kv_s[5]], axis=1)[..., None]
    gates_s = g_s[0, :3 * HQ].reshape(3, HQ, nd).transpose(2, 1, 0)
    o_s, *win_new = _nsa_attn_decode(page_table, q_bd, o_c, sel, new_rows, new_cols, gates_s,
                                     pools[2], pools[3], wins[0], wins[1], li, prev_win)
    ys = _outproj_ln(xs2, o_s.reshape(nd, NSA_W), z_s, w_out, ln_g, ln_b, alpha, tms).reshape(nd, 1, d)
    return yp, ys, cache_p, cache_s, win_new


def _ret_layer(xp, xs, li, state, past, w_in, gn_g, w_out, ln_g, ln_b, alpha, prev_state):
    nb, t, d = xp.shape
    nd = xs.shape[0]
    rdk = d // RH
    assert xs.shape[1] == 1 and t % RET_CHUNK == 0
    w_in = w_in.astype(MXU_DTYPE)
    w_out = w_out.astype(MXU_DTYPE)

    xp2 = xp.reshape(nb * t, d)
    tm = _row_tile(t)
    q, k, v, z = _ret_inproj(xp2, w_in, _xpos_tables(jnp.arange(t), rdk), t, tm)
    dmask, q_dec, k_dec, c_dec = _decay_tables(RET_CHUNK)
    dec = jnp.stack([q_dec, k_dec, jnp.broadcast_to(c_dec[:, None], q_dec.shape)]
                    + [jnp.zeros_like(q_dec)] * 5, axis=-1)
    o, sp = _ret_prompt(q.reshape(nb, t, d), k.reshape(nb, t, d), v.reshape(nb, t, 2 * d), dmask, dec, gn_g)
    yp = _outproj_ln(xp2, o.reshape(nb * t, 2 * d), z, w_out, ln_g, ln_b, alpha, tm).reshape(nb, t, d)

    xs2 = xs.reshape(nd, d)
    tms = _row_tile(nd)
    q, k, v, z = _ret_inproj(xs2, w_in, _xpos_tables(jnp.full((nd,), past, jnp.int32), rdk), nd, tms)
    dmask, q_dec, k_dec, c_dec = _decay_tables(1)
    dec = jnp.stack([dmask[:, 0, 0], q_dec[:, 0], k_dec[:, 0], c_dec], axis=-1)
    o, ss = _ret_decode(q, k, v, state, li, dec, gn_g, prev_state)
    ys = _outproj_ln(xs2, o.reshape(nd, 2 * d), z, w_out, ln_g, ln_b, alpha, tms).reshape(nd, 1, d)
    return yp, ys, sp, ss


def kernel(x_prompt, x_sample, cache_k_cmp, cache_v_cmp, cache_k_sel, cache_v_sel, cache_k_win, cache_v_win,
           state_ret, page_table, nsa_w_in, nsa_w_out, nsa_pe_k, nsa_w1_k, nsa_w2_k, nsa_pe_v, nsa_w1_v,
           nsa_w2_v, ret_w_in, ret_gn_g, ret_w_out, ln_g, ln_b):
    depth = ln_g.shape[0]
    alpha = (2.0 * depth) ** 0.25
    n_nsa = nsa_w_in.shape[0]
    past = page_table.shape[1] * PAGE_SIZE
    t = x_prompt.shape[1]
    pools = [_channel_major(c) for c in (cache_k_cmp, cache_v_cmp, cache_k_sel, cache_v_sel)]
    wins = [_channel_major(c) for c in (cache_k_win, cache_v_win)]
    xp, xs = x_prompt, x_sample
    cache_p, cache_s, win_new, ret_p, ret_s = [], [], [], [], []
    for i in range(depth):
        li = i // 2
        if i % 2 == 0:
            cmp_k = _compress_weights(nsa_pe_k[li], nsa_w1_k[li], nsa_w2_k[li])
            cmp_v = _compress_weights(nsa_pe_v[li], nsa_w1_v[li], nsa_w2_v[li])
            xp, xs, cache_p, cache_s, win_new = _nsa_layer(
                xp, xs, li, n_nsa, page_table, pools, wins, nsa_w_in[li], nsa_w_out[li], cmp_k, cmp_v,
                ln_g[i], ln_b[i], alpha, cache_p, cache_s, win_new)
        else:
            xp, xs, sp, ss = _ret_layer(xp, xs, li, state_ret, past, ret_w_in[li], ret_gn_g[li], ret_w_out[li],
                                        ln_g[i], ln_b[i], alpha, ret_s)
            ret_p.append(sp)
            ret_s = [ss]
    wb = min(WINDOW, t)
    out_p = [_token_major(c) for c in cache_p[:4]] + [_token_major(c[..., t - wb:]) for c in cache_p[4:]]
    out_s = [jnp.moveaxis(_token_major(c), 2, 1) for c in cache_s[:4]] + [_token_major(w) for w in win_new]
    return (xp, xs,
            out_p[0], out_s[0], out_p[1], out_s[1], out_p[2], out_s[2], out_p[3], out_s[3],
            out_p[4], out_s[4], out_p[5], out_s[5],
            jnp.stack(ret_p), ret_s[0])
```

```python
import functools

import jax
import jax.numpy as jnp
from jax import lax
from jax.experimental import pallas as pl
from jax.experimental.pallas import tpu as pltpu

F32 = jnp.float32
MXU_DTYPE = jnp.bfloat16

HQ, KVH, DH = 32, 4, 64
GQ = HQ // KVH
NSA_W, KV_W = HQ * DH, KVH * DH
ROT_DIM, ROPE_THETA = DH // 4, 500000.0
CMP_LEN, CMP_STRIDE, CMP_HID = 32, 16, 2 * DH
SEL_BLK, N_SEL, WINDOW, QBLK = 64, 16, 512, 128
FORCE_BONUS, NEG = 1000.0, -1.0e30
RH, RET_CHUNK, XPOS_BASE = 4, 128, 10000.0
PAGE_SIZE, LN_EPS = 128, 1e-5

LANES = 128
VMEM_LIMIT_BYTES = 56 * 1024 * 1024
CMP_GROUP = 2048
GATE_PAD = LANES
KEY_TILE = 2 * SEL_BLK
SEL_TILE = 4 * KEY_TILE
BLOCKS_PER_TILE = SEL_TILE // SEL_BLK
WIN_TILE = WINDOW + QBLK
ONES_ROWS = 16
LOG2E = 1.4426950408889634
REMOVED = -3.0e38
RUNNING_MAX_INIT = -1.0e29


def _params(n_axes):
    return pltpu.CompilerParams(dimension_semantics=("arbitrary",) * n_axes,
                                vmem_limit_bytes=VMEM_LIMIT_BYTES)


def _dot(a, b):
    return jnp.dot(a, b, preferred_element_type=F32)


def _dot_nt(a, b):
    return lax.dot_general(a, b, (((1,), (1,)), ((), ())), preferred_element_type=F32)


def _dot_tn(a, b):
    return lax.dot_general(a, b, (((0,), (0,)), ((), ())), preferred_element_type=F32)


def _silu(z):
    return z * (1.0 / (1.0 + jnp.exp(-z)))


def _lane_tile(t, reps):
    return t if reps == 1 else jnp.concatenate([t] * reps, axis=1)


def _rotate(a, c, sa, sb, shift):
    n = a.shape[1]
    reps = n // c.shape[1]
    c, sa, sb = _lane_tile(c, reps), _lane_tile(sa, reps), _lane_tile(sb, reps)
    return a * c + pltpu.roll(a, shift, 1) * sa + pltpu.roll(a, n - shift, 1) * sb


def _masked_softmax(s, mask):
    s = jnp.where(mask, s, NEG)
    m = jnp.max(s, axis=-1, keepdims=True)
    e = jnp.where(mask, jnp.exp2(s - m), 0.0)
    d = jnp.sum(e, axis=-1, keepdims=True)
    return e * (1.0 / jnp.where(d > 0.0, d, 1.0))


def _split3(x):
    hi = x.astype(MXU_DTYPE)
    r = x - hi.astype(F32)
    mid = r.astype(MXU_DTYPE)
    lo = (r - mid.astype(F32)).astype(MXU_DTYPE)
    return hi, mid, lo


def _topk_mask(score, idx, n_pick, axis):
    big = jnp.float32(score.shape[axis])
    for _ in range(n_pick):
        m = jnp.max(score, axis=axis, keepdims=True)
        first = jnp.min(jnp.where(score == m, idx, big), axis=axis, keepdims=True)
        score = jnp.where(idx == first, REMOVED, score)
    return (score == REMOVED).astype(F32)


def _overlap(c_idx, s_idx):
    cs, ss = c_idx * CMP_STRIDE, s_idx * SEL_BLK
    ov = jnp.minimum(cs + CMP_LEN, ss + SEL_BLK) - jnp.maximum(cs, ss)
    return jnp.maximum(ov, 0).astype(F32) * (1.0 / CMP_LEN)


def _alias_specs(prev):
    return [pl.BlockSpec(memory_space=pl.ANY)] * len(prev)


def _nsa_inproj_kernel(x_ref, w_ref, c_ref, sa_ref, sb_ref, *rest):
    q_ref, kv_ref, ks_ref, kw_ref, vt_ref, gt_ref, z_ref = rest[-13:-6]
    cache_refs = rest[-6:]
    xb = x_ref[...].astype(MXU_DTYPE)
    c, sa, sb = c_ref[...], sa_ref[...], sb_ref[...]
    half = ROT_DIM // 2
    tm = x_ref.shape[0]
    token = pl.program_id(0) * tm + lax.broadcasted_iota(jnp.int32, (tm, DH), 0)
    blk_onehot = ((token // SEL_BLK) % BLOCKS_PER_TILE
                  == lax.broadcasted_iota(jnp.int32, (tm, DH), 1)).astype(F32)
    col = 0
    for ch in range(NSA_W // 512):
        a = _dot(xb, w_ref[:, col:col + 512])
        a = _rotate(a, c, sa, sb, half) * (DH ** -0.5 * LOG2E)
        for h in range(512 // DH):
            q_ref[0, ch * (512 // DH) + h] = a[:, h * DH:(h + 1) * DH].astype(q_ref.dtype)
        col += 512
    for j in range(6):
        a = _dot(xb, w_ref[:, col:col + KV_W])
        if j % 2 == 0:
            a = _rotate(a, c, sa, sb, half)
        kv_ref[j] = a
        a_t = a.T
        cache_refs[j][0, 0] = a_t
        if j == 2:
            for h in range(KVH):
                ks_ref[0, h] = jnp.concatenate([a[:, h * DH:(h + 1) * DH], blk_onehot], axis=1).astype(ks_ref.dtype)
        if j == 4:
            for h in range(KVH):
                kw_ref[0, h] = a[:, h * DH:(h + 1) * DH].astype(kw_ref.dtype)
        if j in (3, 5):
            vt_ref[j // 2 - 1, 0] = a_t.astype(vt_ref.dtype)
        col += KV_W
    a = _dot(xb, w_ref[:, col:col + GATE_PAD])
    gt_ref[0] = (1.0 / (1.0 + jnp.exp(-a))).T
    col += GATE_PAD
    for ch in range(NSA_W // 512):
        z_ref[:, ch * 512:(ch + 1) * 512] = _silu(_dot(xb, w_ref[:, col:col + 512])).astype(z_ref.dtype)
        col += 512


def _nsa_inproj(x2d, w, tabs, nb, t, tm, li, n_layers, prev):
    m, d = x2d.shape
    nt = t // tm
    n_cols = w.shape[1]
    row = lambda i: (i, 0)
    tab = pl.BlockSpec((tm, LANES), lambda i: (i % nt, 0))
    cache_spec = pl.BlockSpec((1, 1, KV_W, tm), lambda i: (li, i // nt, 0, i % nt))
    cache_shape = jax.ShapeDtypeStruct((n_layers, nb, KV_W, t), F32)
    n_in = 5
    return pl.pallas_call(
        _nsa_inproj_kernel,
        grid=(m // tm,),
        in_specs=[pl.BlockSpec((tm, d), row), pl.BlockSpec((d, n_cols), lambda i: (0, 0)), tab, tab, tab]
        + _alias_specs(prev),
        out_specs=[
            pl.BlockSpec((1, HQ, tm, DH), lambda i: (i // nt, 0, i % nt, 0)),
            pl.BlockSpec((6, tm, KV_W), lambda i: (0, i, 0)),
            pl.BlockSpec((1, KVH, tm, 2 * DH), lambda i: (i // nt, 0, i % nt, 0)),
            pl.BlockSpec((1, KVH, tm, DH), lambda i: (i // nt, 0, i % nt, 0)),
            pl.BlockSpec((2, 1, KV_W, tm), lambda i: (0, i // nt, 0, i % nt)),
            pl.BlockSpec((1, GATE_PAD, tm), lambda i: (i // nt, 0, i % nt)),
            pl.BlockSpec((tm, NSA_W), row),
        ] + [cache_spec] * 6,
        out_shape=[
            jax.ShapeDtypeStruct((nb, HQ, t, DH), MXU_DTYPE),
            jax.ShapeDtypeStruct((6, m, KV_W), F32),
            jax.ShapeDtypeStruct((nb, KVH, t, 2 * DH), MXU_DTYPE),
            jax.ShapeDtypeStruct((nb, KVH, t, DH), MXU_DTYPE),
            jax.ShapeDtypeStruct((2, nb, KV_W, t), MXU_DTYPE),
            jax.ShapeDtypeStruct((nb, GATE_PAD, t), F32),
            jax.ShapeDtypeStruct((m, NSA_W), MXU_DTYPE),
        ] + [cache_shape] * 6,
        input_output_aliases={n_in + j: 7 + j for j in range(len(prev))},
        compiler_params=_params(1),
        name="nsa_inproj",
    )(x2d, w, *tabs, *prev)


def _compress_kernel(*refs, n_prefetch, n_pages, layout, n_chunks):
    split = KV_W // LANES
    pos = n_prefetch
    if n_pages:
        page_refs = refs[pos:pos + n_pages]
        pos += n_pages
    else:
        tok_ref, halo_ref = refs[pos:pos + 2]
        pos += 2
    wpair_ref, pe_ref, w1_ref, w2_ref, out_ref, r_ref = refs[pos:pos + 6]
    r_ref[:, n_chunks:, :] = jnp.zeros((split, 8, CMP_STRIDE * LANES), F32)
    if n_pages:
        tok_sc = refs[pos + 6]
        for p, ref in enumerate(page_refs):
            page = ref[0, 0].T
            for s in range(split):
                tok_sc[s, p * PAGE_SIZE:(p + 1) * PAGE_SIZE, :] = page[:, s * LANES:(s + 1) * LANES]
        for s in range(split):
            for l in range(CMP_STRIDE):
                r_ref[s, 0:n_chunks, l * LANES:(l + 1) * LANES] = tok_sc[s, pl.ds(l, n_chunks, stride=CMP_STRIDE), :]
    else:
        for s in range(split):
            for l in range(CMP_STRIDE):
                lanes = slice(l * LANES, (l + 1) * LANES)
                j = l * split + s
                r_ref[s, 0:n_chunks, lanes] = tok_ref[0, pl.ds(j, n_chunks, stride=CMP_STRIDE * split), :]
                r_ref[s, n_chunks:n_chunks + 1, lanes] = halo_ref[0, j:j + 1, :]
    heads_per_tile = LANES // DH
    ab = [_dot(r_ref[s].astype(MXU_DTYPE), wpair_ref[...]) for s in range(split)]
    pe8 = jnp.broadcast_to(pe_ref[...], (8, CMP_LEN * DH)).astype(MXU_DTYPE)
    bias = _dot(pe8, w1_ref[...])[0:1, :]
    w2 = w2_ref[...]
    toks = []
    for h in range(KVH):
        base = (h % heads_per_tile) * 2 * CMP_HID
        first_half = ab[h // heads_per_tile][0:n_chunks, base:base + CMP_HID]
        second_half = ab[h // heads_per_tile][1:n_chunks + 1, base + CMP_HID:base + 2 * CMP_HID]
        u = first_half + second_half + bias
        hid = 0.5 * u * (1.0 + jnp.tanh(0.7978845608028654 * (u + 0.044715 * (u * u * u))))
        toks.append(_dot(hid.astype(MXU_DTYPE), w2))
    if layout == "head_major":
        for h in range(KVH):
            out_ref[0, h] = toks[h].astype(out_ref.dtype)
    elif layout == "token_major":
        out_ref[0] = jnp.concatenate(toks, axis=1).astype(out_ref.dtype)
    else:
        out_ref[0] = jnp.concatenate(toks, axis=1).T.astype(out_ref.dtype)


def _compress_weights(pe, w1, w2):
    heads = LANES // DH
    w1r = w1.reshape(2, CMP_STRIDE, DH, CMP_HID)
    eye = jnp.eye(heads, dtype=w1.dtype)
    wpair = jnp.einsum("sldn,hg->lhdgsn", w1r, eye).reshape(CMP_STRIDE * LANES, heads * 2 * CMP_HID)
    return (wpair.astype(MXU_DTYPE), pe.reshape(1, CMP_LEN * DH), w1.astype(MXU_DTYPE), w2.astype(MXU_DTYPE))


def _compress_prompt(k_tok, weights, layout):
    nb, t, _ = k_tok.shape
    gt = min(CMP_GROUP, t)
    n_chunks = gt // CMP_STRIDE
    ng = t // gt
    last_halo = t // CMP_STRIDE - 1
    wpair, pe, w1, w2 = weights
    split = KV_W // LANES
    k_tok = k_tok.reshape(nb, t * split, LANES)
    full = lambda a: pl.BlockSpec(a.shape, lambda b, g: (0,) * a.ndim)
    kern = functools.partial(_compress_kernel, n_prefetch=0, n_pages=0, layout=layout, n_chunks=n_chunks)
    if layout == "head_major":
        out_spec = pl.BlockSpec((1, KVH, n_chunks, DH), lambda b, g: (b, 0, g, 0))
        out_shape = jax.ShapeDtypeStruct((nb, KVH, t // CMP_STRIDE, DH), MXU_DTYPE)
    else:
        out_spec = pl.BlockSpec((1, KV_W, n_chunks), lambda b, g: (b, 0, g))
        out_shape = jax.ShapeDtypeStruct((nb, KV_W, t // CMP_STRIDE), MXU_DTYPE)
    return pl.pallas_call(
        kern,
        grid=(nb, ng),
        in_specs=[pl.BlockSpec((1, gt * split, LANES), lambda b, g: (b, g, 0)),
                  pl.BlockSpec((1, CMP_STRIDE * split, LANES),
                               lambda b, g: (b, jnp.minimum((g + 1) * n_chunks, last_halo), 0)),
                  full(wpair), full(pe), full(w1), full(w2)],
        out_specs=out_spec,
        out_shape=out_shape,
        scratch_shapes=[pltpu.VMEM((split, n_chunks + 8, CMP_STRIDE * LANES), F32)],
        compiler_params=_params(2),
        name="nsa_compress_prompt",
    )(k_tok, k_tok, wpair, pe, w1, w2)


def _compress_paged_kernel(*refs, n_pages, n_chunks):
    n_in = n_pages + 4
    ins = refs[1:1 + 2 * n_in]
    outs = refs[1 + 2 * n_in:3 + 2 * n_in]
    scratch = refs[3 + 2 * n_in:]
    for s in range(2):
        _compress_kernel(*ins[s * n_in:(s + 1) * n_in], outs[s], *scratch[2 * s:2 * s + 2],
                         n_prefetch=0, n_pages=n_pages, layout="token_major", n_chunks=n_chunks)


def _compress_paged(pool_k, pool_v, li, page_table, weights_k, weights_v):
    nb, n_pages = page_table.shape
    n_chunks = n_pages * PAGE_SIZE // CMP_STRIDE
    split = KV_W // LANES
    full = lambda a: pl.BlockSpec(a.shape, lambda b, pt: (0,) * a.ndim)
    page = lambda p: pl.BlockSpec((1, 1, KV_W, PAGE_SIZE), lambda b, pt: (li, pt[b, p], 0, 0))
    stream_specs = lambda w: [page(p) for p in range(n_pages)] + [full(a) for a in w]
    out_spec = pl.BlockSpec((1, n_chunks, KV_W), lambda b, pt: (b, 0, 0))
    out_shape = jax.ShapeDtypeStruct((nb, n_chunks, KV_W), MXU_DTYPE)
    stream_scratch = [pltpu.VMEM((split, n_chunks + 8, CMP_STRIDE * LANES), F32),
                      pltpu.VMEM((split, n_pages * PAGE_SIZE, LANES), F32)]
    return pl.pallas_call(
        functools.partial(_compress_paged_kernel, n_pages=n_pages, n_chunks=n_chunks),
        grid_spec=pltpu.PrefetchScalarGridSpec(
            num_scalar_prefetch=1,
            grid=(nb,),
            in_specs=stream_specs(weights_k) + stream_specs(weights_v),
            out_specs=[out_spec, out_spec],
            scratch_shapes=stream_scratch + stream_scratch,
        ),
        out_shape=[out_shape, out_shape],
        compiler_params=_params(1),
        name="nsa_compress_paged",
    )(page_table, *([pool_k] * n_pages), *weights_k, *([pool_v] * n_pages), *weights_v)


def _nsa_attn_prompt_kernel(q_ref, kc_ref, vct_ref, ks_ref, vst_ref, kw_ref, vwt_ref, gc_ref, gs_ref, gw_ref,
                            o_ref, ot_sc, m_sc, alpha_sc, acc_sc, p_sc, s_sc, lanes_sc, *, seq_len):
    st = pl.program_id(2) * QBLK
    ncp = kc_ref.shape[2]
    ns = seq_len // SEL_BLK
    n_pairs = GQ // 2
    qpos = st + lax.broadcasted_iota(jnp.int32, (1, QBLK), 1)
    q_pairs = [q_ref[0, 2 * j:2 * j + 2].reshape(2 * QBLK, DH) for j in range(n_pairs)]
    both = lambda b: jnp.concatenate([b, b], axis=1)

    c_row = lax.broadcasted_iota(jnp.int32, (ncp, 1), 0)
    bias_c = both(jnp.where(c_row * CMP_STRIDE + (CMP_LEN - 1) <= qpos, 0.0, NEG))
    sees_any = both(qpos >= CMP_LEN - 1)
    kc, vct = kc_ref[0, 0], vct_ref[0]
    ws = pl.multiple_of(jnp.maximum(st - WINDOW, 0), KEY_TILE)
    w_row = lax.broadcasted_iota(jnp.int32, (WIN_TILE, 1), 0)
    dist = qpos - (ws + w_row)
    bias_w = both(jnp.where((dist >= 0) & (dist <= WINDOW), 0.0, NEG))
    kw_tile = kw_ref[0, 0, pl.ds(ws, WIN_TILE), :]
    vwt_tile = jnp.concatenate([vwt_ref[0, 0, :, pl.ds(ws, WIN_TILE)], jnp.ones((ONES_ROWS, WIN_TILE), MXU_DTYPE)],
                               axis=0)
    c_scores = [_dot_nt(kc, q_pairs[j]) for j in range(n_pairs)]
    w_scores = [_dot_nt(kw_tile, q_pairs[j]) for j in range(n_pairs)]
    ks_first = ks_ref[0, 0, 0:SEL_TILE, 0:DH]
    for j in range(n_pairs):
        s_sc[j] = _dot_nt(ks_first, q_pairs[j])
    p_sum = jnp.zeros((ncp, QBLK), F32)
    c_probs, w_probs = [], []
    for j in range(n_pairs):
        s = c_scores[j] + bias_c
        e = jnp.exp2(s - jnp.max(s, axis=0, keepdims=True))
        den = jnp.sum(e, axis=0, keepdims=True)
        p = e * jnp.where(sees_any, 1.0 / den, 0.0)
        c_probs.append(p.astype(MXU_DTYPE))
        p_sum = p_sum + p[:, :QBLK] + p[:, QBLK:]
    for j in range(n_pairs):
        s = w_scores[j] + bias_w
        w_probs.append(jnp.exp2(s - jnp.max(s, axis=0, keepdims=True)).astype(MXU_DTYPE))
    o_c = [_dot(vct, p) for p in c_probs]
    o_w = [_dot(vwt_tile, p) for p in w_probs]
    o_w = [o[:DH] * (1.0 / o[DH:DH + 1]) for o in o_w]

    s_idx = lax.broadcasted_iota(jnp.int32, (ns, 1), 0)
    ov_t = _overlap(lax.broadcasted_iota(jnp.int32, (1, ncp), 1), s_idx).astype(MXU_DTYPE)
    imp_t = sum(_dot(ov_t, part) for part in _split3(p_sum))
    blk_q = qpos // SEL_BLK
    forced = (s_idx == 0) | (s_idx == blk_q) | (s_idx == blk_q - 1)
    valid = s_idx * SEL_BLK <= qpos
    score = jnp.where(valid, imp_t + FORCE_BONUS * forced.astype(F32), NEG)
    sel_t = _topk_mask(score, s_idx.astype(F32), min(N_SEL, ns), 0)
    sel_bias = jnp.where(valid & (sel_t > 0.5), 0.0, NEG)
    sel_bias_q = sel_bias.T.astype(MXU_DTYPE)

    n_tiles = (st + QBLK + SEL_TILE - 1) // SEL_TILE
    m_sc[...] = jnp.full(m_sc.shape, RUNNING_MAX_INIT, F32)
    acc_sc[...] = jnp.zeros(acc_sc.shape, F32)
    alpha_sc[...] = jnp.ones(alpha_sc.shape, F32)
    p_sc[...] = jnp.zeros(p_sc.shape, MXU_DTYPE)
    ones_rows = jnp.ones((ONES_ROWS, SEL_TILE), MXU_DTYPE)
    first_bias = both(jnp.concatenate(
        [jnp.broadcast_to(sel_bias[b:b + 1, :], (SEL_BLK, QBLK)) for b in range(BLOCKS_PER_TILE)], axis=0))
    for j in range(n_pairs):
        s_sc[j] = s_sc[j] + first_bias
    blk_row = lax.broadcasted_iota(jnp.int32, (ns, DH), 0)
    blk_lane = lax.broadcasted_iota(jnp.int32, (ns, DH), 1)

    def key_start(i):
        return pl.multiple_of(i * SEL_TILE, SEL_TILE)

    def fold(i):
        vt_tile = jnp.concatenate([vst_ref[0, 0, :, pl.ds(key_start(i), SEL_TILE)], ones_rows], axis=0)
        for j in range(n_pairs):
            acc_sc[j] = alpha_sc[j] * acc_sc[j] + _dot(vt_tile, p_sc[j])

    def bias_lanes(i):
        pick = ((blk_row - i * BLOCKS_PER_TILE == blk_lane) & (blk_lane < BLOCKS_PER_TILE)).astype(MXU_DTYPE)
        return _dot(sel_bias_q, pick).astype(MXU_DTYPE)

    def masked_queries(lanes):
        return [jnp.concatenate([jnp.concatenate([q_ref[0, 2 * j + u], lanes], axis=1) for u in range(2)], axis=0)
                for j in range(n_pairs)]

    def softmax(j, s):
        m_old = m_sc[j]
        m_new = jnp.maximum(m_old, jnp.max(s, axis=0, keepdims=True))
        p_sc[j] = jnp.exp2(s - m_new).astype(MXU_DTYPE)
        alpha_sc[j] = jnp.exp2(m_old - m_new)
        m_sc[j] = m_new

    lanes_sc[...] = bias_lanes(1)

    def step(i, carry):
        fold(jnp.maximum(i - 1, 0))
        k_ahead = ks_ref[0, 0, pl.ds(key_start(i + 1), SEL_TILE), :]
        q_ahead = masked_queries(lanes_sc[...])
        for j in range(n_pairs):
            ahead = _dot_nt(k_ahead, q_ahead[j])
            softmax(j, s_sc[j])
            s_sc[j] = ahead
        lanes_sc[...] = bias_lanes(i + 2)
        return carry

    lax.fori_loop(0, n_tiles - 1, step, 0)
    fold(jnp.maximum(n_tiles - 2, 0))
    k_row = (n_tiles - 1) * SEL_TILE + lax.broadcasted_iota(jnp.int32, (SEL_TILE, 1), 0)
    causal = both(jnp.where(k_row <= qpos, 0.0, NEG))
    for j in range(n_pairs):
        softmax(j, s_sc[j] + causal)
    fold(n_tiles - 1)
    o_s = [acc_sc[j, :DH] * (1.0 / acc_sc[j, DH:DH + 1]) for j in range(n_pairs)]
    gc, gs, gw = gc_ref[0], gs_ref[0], gw_ref[0]
    for h in range(GQ):
        j, cols = h // 2, slice((h % 2) * QBLK, (h % 2 + 1) * QBLK)
        ot_sc[h * DH:(h + 1) * DH, :] = (gc[h:h + 1] * o_c[j][:, cols] + gs[h:h + 1] * o_s[j][:, cols]
                                         + gw[h:h + 1] * o_w[j][:, cols])
    o_ref[0] = ot_sc[...].T.astype(o_ref.dtype)


def _nsa_attn_prompt(q_hm, kc_hm, vc_t, ks_aug, kw_hm, v_t, gates_t):
    nb, _, t, _ = q_hm.shape
    ncp = kc_hm.shape[2]
    k_spec = lambda width: pl.BlockSpec((1, 1, t, width), lambda b, h, i: (b, h, 0, 0))
    vt_spec = lambda j: pl.BlockSpec((1, 1, DH, t), lambda b, h, i: (j, b, h, 0))
    gate_spec = lambda br: pl.BlockSpec((1, GQ, QBLK), lambda b, h, i: (b, br * KVH + h, i))
    kern = functools.partial(_nsa_attn_prompt_kernel, seq_len=t)
    return pl.pallas_call(
        kern,
        grid=(nb, KVH, t // QBLK),
        in_specs=[pl.BlockSpec((1, GQ, QBLK, DH), lambda b, h, i: (b, h, i, 0)),
                  pl.BlockSpec((1, 1, ncp, DH), lambda b, h, i: (b, h, 0, 0)),
                  pl.BlockSpec((1, DH, ncp), lambda b, h, i: (b, h, 0)),
                  k_spec(2 * DH), vt_spec(0), k_spec(DH), vt_spec(1),
                  gate_spec(0), gate_spec(1), gate_spec(2)],
        out_specs=pl.BlockSpec((1, QBLK, GQ * DH), lambda b, h, i: (b, i, h)),
        out_shape=jax.ShapeDtypeStruct((nb, t, NSA_W), MXU_DTYPE),
        scratch_shapes=[pltpu.VMEM((GQ * DH, QBLK), F32),
                        pltpu.VMEM((GQ // 2, 1, 2 * QBLK), F32), pltpu.VMEM((GQ // 2, 1, 2 * QBLK), F32),
                        pltpu.VMEM((GQ // 2, DH + ONES_ROWS, 2 * QBLK), F32),
                        pltpu.VMEM((GQ // 2, SEL_TILE, 2 * QBLK), MXU_DTYPE),
                        pltpu.VMEM((GQ // 2, SEL_TILE, 2 * QBLK), F32),
                        pltpu.VMEM((QBLK, DH), MXU_DTYPE)],
        compiler_params=_params(3),
        name="nsa_attn_prompt",
    )(q_hm, kc_hm, vc_t, ks_aug, v_t, kw_hm, v_t, gates_t, gates_t, gates_t)


def _block_diag_fold(x):
    r_kv = lax.broadcasted_iota(jnp.int32, (HQ, KV_W), 0) // GQ
    l_kv = lax.broadcasted_iota(jnp.int32, (HQ, KV_W), 1) // DH
    x = jnp.where(r_kv == l_kv, x, 0.0)
    return sum(x[..., h * DH:(h + 1) * DH] for h in range(KVH))


def _nsa_select_kernel(q_ref, kc_ref, vc_ref, oc_ref, sel_ref, *, past):
    bb, ncp = kc_ref.shape[0], kc_ref.shape[1]
    nsp = sel_ref.shape[1]
    q = q_ref[...]
    s = jnp.einsum("bqd,bkd->bqk", q, kc_ref[...], preferred_element_type=F32)
    c_idx = lax.broadcasted_iota(jnp.int32, (1, 1, ncp), 2)
    p_c = _masked_softmax(s, c_idx * CMP_STRIDE + (CMP_LEN - 1) <= past)
    o_c = jnp.einsum("bqk,bkd->bqd", p_c.astype(MXU_DTYPE), vc_ref[...], preferred_element_type=F32)
    oc_ref[...] = _block_diag_fold(o_c)
    ov = _overlap(lax.broadcasted_iota(jnp.int32, (ncp, 1), 0),
                  lax.broadcasted_iota(jnp.int32, (1, nsp), 1)).astype(MXU_DTYPE)
    imp_rows = sum(_dot(part, ov) for part in _split3(p_c.reshape(bb * HQ, ncp)))
    imp = jnp.sum(imp_rows.reshape(bb * KVH, GQ, nsp), axis=1)
    s_idx = lax.broadcasted_iota(jnp.int32, (1, nsp), 1)
    blk_q = past // SEL_BLK
    forced = (s_idx == 0) | (s_idx == blk_q) | (s_idx == blk_q - 1)
    valid = s_idx * SEL_BLK <= past
    score = jnp.where(valid, imp + FORCE_BONUS * forced.astype(F32), NEG)
    n_blocks = past // SEL_BLK + 1
    sel = _topk_mask(score, s_idx.astype(F32), min(N_SEL, n_blocks), 1)
    sel_ref[...] = jnp.where(valid, sel, 0.0)


def _nsa_select(q_bd, kc_tm, vc_tm, past, bb):
    nb, ncp, _ = kc_tm.shape
    nsp = LANES
    assert past // SEL_BLK + 1 <= nsp
    kern = functools.partial(_nsa_select_kernel, past=past)
    blk = lambda shape: pl.BlockSpec(shape, lambda i: (i, 0, 0))
    return pl.pallas_call(
        kern,
        grid=(nb // bb,),
        in_specs=[blk((bb, HQ, KV_W)), blk((bb, ncp, KV_W)), blk((bb, ncp, KV_W))],
        out_specs=[blk((bb, HQ, DH)), pl.BlockSpec((bb * KVH, nsp), lambda i: (i, 0))],
        out_shape=[jax.ShapeDtypeStruct((nb, HQ, DH), F32), jax.ShapeDtypeStruct((nb * KVH, nsp), F32)],
        compiler_params=_params(1),
        name="nsa_select_decode",
    )(q_bd, kc_tm, vc_tm)


def _nsa_attn_decode_kernel(*refs, n_pages, n_prev):
    q_ref, oc_ref, sel_ref, new_ref, newc_ref, g_ref = refs[1:7]
    k_pages = refs[7:7 + n_pages]
    v_pages = refs[7 + n_pages:7 + 2 * n_pages]
    pos = 7 + 2 * n_pages
    kwin_ref, vwin_ref = refs[pos:pos + 2]
    o_ref, kwin_out, vwin_out = refs[pos + 2 + n_prev:pos + 5 + n_prev]
    q = q_ref[0]
    qf = q.astype(F32)
    sel4 = sel_ref[0]
    sel_rows = jnp.concatenate([jnp.broadcast_to(sel4[k:k + 1], (GQ, sel4.shape[1])) for k in range(KVH)], axis=0)
    new = new_ref[0]
    first_half = lax.broadcasted_iota(jnp.int32, (1, PAGE_SIZE), 1) < SEL_BLK

    scores, masks = [], []
    for p in range(n_pages):
        scores.append(_dot(q, k_pages[p][0, 0].astype(MXU_DTYPE)))
        masks.append(jnp.where(first_half, sel_rows[:, 2 * p:2 * p + 1], sel_rows[:, 2 * p + 1:2 * p + 2]) > 0.5)
    own = 2 * n_pages
    s_new = jnp.sum(qf * new[0:1, :].astype(MXU_DTYPE).astype(F32), axis=-1, keepdims=True)
    ok_new = sel_rows[:, own:own + 1] > 0.5
    s_new = jnp.where(ok_new, s_new, NEG)
    m = s_new
    for s, ok in zip(scores, masks):
        m = jnp.maximum(m, jnp.max(jnp.where(ok, s, NEG), axis=-1, keepdims=True))
    e_new = jnp.where(ok_new, jnp.exp2(s_new - m), 0.0)
    den = e_new
    acc = e_new * new[1:2, :].astype(MXU_DTYPE).astype(F32)
    for p, (s, ok) in enumerate(zip(scores, masks)):
        e = jnp.where(ok, jnp.exp2(s - m), 0.0)
        den = den + jnp.sum(e, axis=-1, keepdims=True)
        acc = acc + _dot_nt(e.astype(MXU_DTYPE), v_pages[p][0, 0].astype(MXU_DTYPE))
    o_s = _block_diag_fold(acc * (1.0 / den))

    kwin, vwin = kwin_ref[0, 0], vwin_ref[0, 0]
    s_w = _dot(q, kwin.astype(MXU_DTYPE))
    s_wn = jnp.sum(qf * new[2:3, :].astype(MXU_DTYPE).astype(F32), axis=-1, keepdims=True)
    m_w = jnp.maximum(jnp.max(s_w, axis=-1, keepdims=True), s_wn)
    e_w, e_wn = jnp.exp2(s_w - m_w), jnp.exp2(s_wn - m_w)
    den_w = jnp.sum(e_w, axis=-1, keepdims=True) + e_wn
    acc_w = _dot_nt(e_w.astype(MXU_DTYPE), vwin.astype(MXU_DTYPE)) + e_wn * new[3:4, :].astype(MXU_DTYPE).astype(F32)
    o_w = _block_diag_fold(acc_w * (1.0 / den_w))

    g = g_ref[0]
    o_ref[0] = g[:, 0:1] * oc_ref[0] + g[:, 1:2] * o_s + g[:, 2:3] * o_w

    wb = kwin.shape[1]
    last = lax.broadcasted_iota(jnp.int32, (1, wb), 1) == wb - 1
    kwin_out[0, 0] = jnp.where(last, newc_ref[0, 0], pltpu.roll(kwin, wb - 1, 1))
    vwin_out[0, 0] = jnp.where(last, newc_ref[0, 1], pltpu.roll(vwin, wb - 1, 1))


def _nsa_attn_decode(page_table, q_bd, o_c, sel, new_rows, new_cols, gates, pool_k, pool_v, win_k, win_v,
                     li, prev):
    nb, n_pages = page_table.shape
    n_layers, _, _, wb = win_k.shape
    nsp = sel.shape[-1]
    per_b = lambda shape: pl.BlockSpec((1,) + shape, lambda b, pt: (b,) + (0,) * len(shape))
    page = lambda p: pl.BlockSpec((1, 1, KV_W, PAGE_SIZE), lambda b, pt: (li, pt[b, p], 0, 0))
    win = pl.BlockSpec((1, 1, KV_W, wb), lambda b, pt: (li, b, 0, 0))
    kern = functools.partial(_nsa_attn_decode_kernel, n_pages=n_pages, n_prev=len(prev))
    n_in = 7 + 2 * n_pages + 2
    win_shape = jax.ShapeDtypeStruct((n_layers, nb, KV_W, wb), F32)
    return pl.pallas_call(
        kern,
        grid_spec=pltpu.PrefetchScalarGridSpec(
            num_scalar_prefetch=1,
            grid=(nb,),
            in_specs=([per_b((HQ, KV_W)), per_b((HQ, DH)), per_b((KVH, nsp)), per_b((4, KV_W)),
                       per_b((2, KV_W, 1)), per_b((HQ, 3))]
                      + [page(p) for p in range(n_pages)] + [page(p) for p in range(n_pages)]
                      + [win, win] + _alias_specs(prev)),
            out_specs=[per_b((HQ, DH)), win, win],
        ),
        out_shape=[jax.ShapeDtypeStruct((nb, HQ, DH), F32), win_shape, win_shape],
        input_output_aliases={n_in + j: 1 + j for j in range(len(prev))},
        compiler_params=_params(1),
        name="nsa_attn_decode",
    )(page_table, q_bd, o_c, sel.reshape(nb, KVH, nsp), new_rows, new_cols, gates,
      *([pool_k] * n_pages), *([pool_v] * n_pages), win_k, win_v, *prev)


def _outproj_ln_kernel(x_ref, o_ref, z_ref, w_ref, g_ref, b_ref, y_ref, *, alpha):
    a = o_ref[...].astype(F32) * z_ref[...].astype(F32)
    h = alpha * x_ref[...] + _dot(a.astype(MXU_DTYPE), w_ref[...])
    d = h - jnp.mean(h, axis=-1, keepdims=True)
    var = jnp.mean(d * d, axis=-1, keepdims=True)
    y_ref[...] = d * lax.rsqrt(var + LN_EPS) * g_ref[...] + b_ref[...]


def _outproj_ln(x2d, o2d, z2d, w, ln_g, ln_b, alpha, tm):
    m, d = x2d.shape
    wdt = o2d.shape[1]
    row = lambda n: pl.BlockSpec((tm, n), lambda i: (i, 0))
    fixed = lambda shape: pl.BlockSpec(shape, lambda i: (0, 0))
    return pl.pallas_call(
        functools.partial(_outproj_ln_kernel, alpha=alpha),
        grid=(m // tm,),
        in_specs=[row(d), row(wdt), row(wdt), fixed((wdt, d)), fixed((1, d)), fixed((1, d))],
        out_specs=row(d),
        out_shape=jax.ShapeDtypeStruct((m, d), F32),
        compiler_params=_params(1),
        name="outproj_ln",
    )(x2d, o2d, z2d, w, ln_g.reshape(1, d), ln_b.reshape(1, d))


def _ret_inproj_kernel(x_ref, w_ref, c_ref, sa_ref, sb_ref, q_ref, k_ref, v_ref, z_ref, *, rdk):
    xb = x_ref[...].astype(MXU_DTYPE)
    c, sa, sb = c_ref[...], sa_ref[...], sb_ref[...]
    qk_w = RH * rdk
    col = 0
    for dst, scale in ((q_ref, 1.0), (k_ref, rdk ** -0.5)):
        for ch in range(qk_w // 512):
            a = _rotate(_dot(xb, w_ref[:, col:col + 512]), c, sa, sb, 1)
            dst[:, ch * 512:(ch + 1) * 512] = a * scale
            col += 512
    for dst, act in ((v_ref, lambda a: a), (z_ref, _silu)):
        for ch in range(2 * qk_w // 512):
            dst[:, ch * 512:(ch + 1) * 512] = act(_dot(xb, w_ref[:, col:col + 512])).astype(dst.dtype)
            col += 512


def _ret_inproj(x2d, w, tabs, t, tm):
    m, d = x2d.shape
    nt = t // tm
    rdk = d // RH
    row = lambda n: pl.BlockSpec((tm, n), lambda i: (i, 0))
    tab = pl.BlockSpec((tm, rdk), lambda i: (i % nt, 0))
    return pl.pallas_call(
        functools.partial(_ret_inproj_kernel, rdk=rdk),
        grid=(m // tm,),
        in_specs=[row(d), pl.BlockSpec(w.shape, lambda i: (0, 0)), tab, tab, tab],
        out_specs=[row(d), row(d), row(2 * d), row(2 * d)],
        out_shape=[jax.ShapeDtypeStruct((m, d), F32), jax.ShapeDtypeStruct((m, d), F32),
                   jax.ShapeDtypeStruct((m, 2 * d), F32), jax.ShapeDtypeStruct((m, 2 * d), MXU_DTYPE)],
        compiler_params=_params(1),
        name="ret_inproj",
    )(x2d, w, *tabs)


def _group_norm(o, gain):
    d = o - jnp.mean(o, axis=-1, keepdims=True)
    return d * lax.rsqrt(jnp.mean(d * d, axis=-1, keepdims=True) + LN_EPS) * gain


def _ret_prompt_kernel(q_ref, k_ref, v_ref, dm_ref, dec_ref, gn_ref, o_ref, s_ref, s_sc, *, rdk, rdv):
    ci = pl.program_id(1)

    @pl.when(ci == 0)
    def _():
        s_sc[...] = jnp.zeros(s_sc.shape, F32)

    heads = range(RH)
    q = [q_ref[0, :, h * rdk:(h + 1) * rdk] for h in heads]
    k = [k_ref[0, :, h * rdk:(h + 1) * rdk] for h in heads]
    v = [v_ref[0, :, h * rdv:(h + 1) * rdv].astype(MXU_DTYPE) for h in heads]
    dec = [dec_ref[h] for h in heads]
    s_old = [s_sc[h] for h in heads]
    inner = [_dot_nt(q[h].astype(MXU_DTYPE), k[h].astype(MXU_DTYPE)) for h in heads]
    cross = [_dot((q[h] * dec[h][:, 0:1]).astype(MXU_DTYPE), s_old[h].astype(MXU_DTYPE)) for h in heads]
    outer = [_dot_tn((k[h] * dec[h][:, 1:2]).astype(MXU_DTYPE), v[h]) for h in heads]
    intra = [_dot((inner[h] * dm_ref[h]).astype(MXU_DTYPE), v[h]) for h in heads]
    for h in heads:
        s_sc[h] = s_old[h] * dec[h][0:1, 2:3] + outer[h]
        o_ref[0, :, h * rdv:(h + 1) * rdv] = _group_norm(
            intra[h] + cross[h], gn_ref[:, h * rdv:(h + 1) * rdv]).astype(o_ref.dtype)

    @pl.when(ci == pl.num_programs(1) - 1)
    def _():
        s_ref[0] = s_sc[...]


def _ret_decode_kernel(dec_ref, q_ref, k_ref, v_ref, s0_ref, gn_ref, o_ref, s_ref, *, rdk, rdv):
    pad = LANES
    first_row = lax.broadcasted_iota(jnp.int32, (pad, 1), 0) == 0
    for r in range(q_ref.shape[0]):
        for h in range(RH):
            q = q_ref[r, :, h * rdk:(h + 1) * rdk]
            k = k_ref[r, :, h * rdk:(h + 1) * rdk]
            v = v_ref[r, :, h * rdv:(h + 1) * rdv]
            dmask, q_dec, k_dec, c_dec = dec_ref[h, 0], dec_ref[h, 1], dec_ref[h, 2], dec_ref[h, 3]
            s0 = s0_ref[0, r, h]
            k_pad = jnp.where(first_row, jnp.broadcast_to(k * k_dec, (pad, rdk)), 0.0).astype(MXU_DTYPE)
            v_pad = jnp.broadcast_to(v, (pad, rdv)).astype(MXU_DTYPE)
            s_ref[0, r, h] = s0 * c_dec + _dot_tn(k_pad, v_pad)
            qk = jnp.sum(q.astype(MXU_DTYPE).astype(F32) * k.astype(MXU_DTYPE).astype(F32), axis=-1, keepdims=True)
            inner = (qk * dmask).astype(MXU_DTYPE).astype(F32)
            q_pad = jnp.broadcast_to(q * q_dec, (8, rdk)).astype(MXU_DTYPE)
            o = inner * v.astype(MXU_DTYPE).astype(F32) + _dot(q_pad, s0.astype(MXU_DTYPE))[0:1, :]
            o_ref[r, :, h * rdv:(h + 1) * rdv] = _group_norm(o, gn_ref[:, h * rdv:(h + 1) * rdv])


def _ret_mixer_kernel(dec_s_ref, q_ref, k_ref, v_ref, dm_ref, dec_ref, gn_ref, qs_ref, ks_ref, vs_ref, s0_ref,
                      *rest, rdk, rdv):
    o_ref, sp_ref, os_ref, ss_ref, s_sc = rest[-5:]
    _ret_prompt_kernel(q_ref, k_ref, v_ref, dm_ref, dec_ref, gn_ref, o_ref, sp_ref, s_sc, rdk=rdk, rdv=rdv)
    _ret_decode_kernel(dec_s_ref, qs_ref, ks_ref, vs_ref, s0_ref, gn_ref, os_ref, ss_ref, rdk=rdk, rdv=rdv)


def _ret_mixer(q, k, v, dmask, dec, qs, ks, vs, state, li, dec_s, gn_g, prev):
    nb, t, qk_w = q.shape
    nd = qs.shape[0]
    rdk = qk_w // RH
    rdv = v.shape[2] // RH
    c = dmask.shape[1]
    nch = t // c
    assert nd % (nb * nch) == 0
    per = nd // (nb * nch)
    tok = lambda n: pl.BlockSpec((1, c, n), lambda b, i: (b, i, 0))
    full = lambda a: pl.BlockSpec(a.shape, lambda b, i: (0,) * a.ndim)
    vec = lambda n: pl.BlockSpec((per, 1, n), lambda b, i: (b * nch + i, 0, 0))
    st_spec = pl.BlockSpec((1, per, RH, rdk, rdv), lambda b, i: (li, b * nch + i, 0, 0, 0))
    gn_g = gn_g.reshape(1, RH * rdv)
    n_in = 11
    return pl.pallas_call(
        functools.partial(_ret_mixer_kernel, rdk=rdk, rdv=rdv),
        grid=(nb, nch),
        in_specs=[pl.BlockSpec(memory_space=pltpu.SMEM), tok(qk_w), tok(qk_w), tok(RH * rdv), full(dmask),
                  full(dec), full(gn_g), vec(qk_w), vec(qk_w), vec(RH * rdv), st_spec] + _alias_specs(prev),
        out_specs=[tok(RH * rdv), pl.BlockSpec((1, RH, rdk, rdv), lambda b, i: (b, 0, 0, 0)),
                   vec(RH * rdv), st_spec],
        out_shape=[jax.ShapeDtypeStruct((nb, t, RH * rdv), MXU_DTYPE),
                   jax.ShapeDtypeStruct((nb, RH, rdk, rdv), F32),
                   jax.ShapeDtypeStruct((nd, 1, RH * rdv), F32), jax.ShapeDtypeStruct(state.shape, F32)],
        input_output_aliases={n_in + j: 3 + j for j in range(len(prev))},
        scratch_shapes=[pltpu.VMEM((RH, rdk, rdv), F32)],
        compiler_params=_params(2),
        name="ret_mixer",
    )(dec_s, q, k, v, dmask, dec, gn_g, qs.reshape(nd, 1, qk_w), ks.reshape(nd, 1, qk_w),
      vs.reshape(nd, 1, RH * rdv), state, *prev)


def _rope_tables(pos):
    half = ROT_DIM // 2
    inv = ROPE_THETA ** (-jnp.arange(half, dtype=F32) / half)
    ang = pos.astype(F32)[:, None] * inv[None, :]
    lane = jnp.arange(LANES) % DH
    cos, sin = jnp.cos(ang)[:, lane % half], jnp.sin(ang)[:, lane % half]
    c = jnp.where(lane < ROT_DIM, cos, 1.0)
    sa = jnp.where((lane >= half) & (lane < ROT_DIM), sin, 0.0)
    sb = jnp.where(lane < half, -sin, 0.0)
    return c, sa, sb


def _xpos_tables(pos, rdk):
    half = rdk // 2
    inv = 1.0 / (XPOS_BASE ** jnp.linspace(0.0, 1.0, half, dtype=F32))
    ang = pos.astype(F32)[:, None] * inv[None, :]
    lane = jnp.arange(rdk)
    cos, sin = jnp.cos(ang)[:, lane // 2], jnp.sin(ang)[:, lane // 2]
    sa = jnp.where(lane % 2 == 1, sin, 0.0)
    sb = jnp.where(lane % 2 == 0, -sin, 0.0)
    return cos, sa, sb


def _decay_tables(c):
    log_g = jnp.log1p(-jnp.power(2.0, -5.0 - jnp.arange(RH, dtype=F32)))
    i = jnp.arange(c, dtype=F32)
    diff = i[:, None] - i[None, :]
    dmask = jnp.where(diff >= 0, jnp.exp(jnp.maximum(diff, 0.0)[None] * log_g[:, None, None]), 0.0)
    q_dec = jnp.exp((i + 1.0)[None, :] * log_g[:, None])
    k_dec = jnp.exp((c - 1.0 - i)[None, :] * log_g[:, None])
    c_dec = jnp.exp(c * log_g)
    return dmask, q_dec, k_dec, c_dec


def _nsa_w_in_layout(w):
    kv_end = NSA_W + 6 * KV_W
    gates = jnp.pad(w[:, kv_end:kv_end + 3 * HQ], ((0, 0), (0, GATE_PAD - 3 * HQ)))
    return jnp.concatenate([w[:, :kv_end], gates, w[:, kv_end + 3 * HQ:]], axis=1).astype(MXU_DTYPE)


def _channel_major(cache):
    return jnp.moveaxis(cache, -3, -1).reshape(cache.shape[:-3] + (KV_W, cache.shape[-3]))


def _token_major(slab):
    return jnp.moveaxis(slab.reshape(slab.shape[:-2] + (KVH, DH, slab.shape[-1])), -1, -3)


def _row_tile(m):
    for tm in (256, 128):
        if m % tm == 0:
            return tm
    raise ValueError(f"row count {m} is not a multiple of {LANES}")


def _nsa_layer(xp, xs, li, n_layers, page_table, pools, wins, w_in, w_out, cmp_k, cmp_v, ln_g, ln_b, alpha,
               prev_p, prev_s, prev_win):
    nb, t, d = xp.shape
    nd = xs.shape[0]
    past = page_table.shape[1] * PAGE_SIZE
    assert xs.shape[1] == 1 and t % SEL_TILE == 0 and t >= WIN_TILE
    w_in = _nsa_w_in_layout(w_in)
    w_out = w_out.astype(MXU_DTYPE)

    xp2 = xp.reshape(nb * t, d)
    tm = _row_tile(t)
    q_hm, kv, ks_aug, kw_hm, v_t, g_t, z, *cache_p = _nsa_inproj(xp2, w_in, _rope_tables(jnp.arange(t)), nb, t, tm,
                                                        li, n_layers, prev_p)
    kv = kv.reshape(6, nb, t, KV_W)
    kc_hm = _compress_prompt(kv[0], cmp_k, "head_major")
    vc_t = _compress_prompt(kv[1], cmp_v, "channel_major")
    o = _nsa_attn_prompt(q_hm, kc_hm, vc_t, ks_aug, kw_hm, v_t, g_t)
    yp = _outproj_ln(xp2, o.reshape(nb * t, NSA_W), z, w_out, ln_g, ln_b, alpha, tm).reshape(nb, t, d)

    xs2 = xs.reshape(nd, d)
    tms = _row_tile(nd)
    pos_s = jnp.full((nd,), past, jnp.int32)
    q_s, kv_s, _, _, _, g_s, z_s, *cache_s = _nsa_inproj(xs2, w_in, _rope_tables(pos_s), 1, nd, tms,
                                                      li, n_layers, prev_s)
    q_tok = q_s[0].transpose(1, 0, 2).astype(F32).reshape(nd, KVH, GQ, 1, DH)
    q_bd = (q_tok * jnp.eye(KVH, dtype=F32)[None, :, None, :, None]).reshape(nd, HQ, KV_W).astype(MXU_DTYPE)
    kc_tm, vc_tm = _compress_paged(pools[0], pools[1], li, page_table, cmp_k, cmp_v)
    o_c, sel = _nsa_select(q_bd, kc_tm, vc_tm, past, min(16, nd))
    new_rows = jnp.stack([kv_s[2], kv_s[3], kv_s[4], kv_s[5]], axis=1)
    new_cols = jnp.stack([kv_s[4], kv_s[5]], axis=1)[..., None]
    gates_s = g_s[0, :3 * HQ].reshape(3, HQ, nd).transpose(2, 1, 0)
    o_s, *win_new = _nsa_attn_decode(page_table, q_bd, o_c, sel, new_rows, new_cols, gates_s,
                                     pools[2], pools[3], wins[0], wins[1], li, prev_win)
    ys = _outproj_ln(xs2, o_s.reshape(nd, NSA_W), z_s, w_out, ln_g, ln_b, alpha, tms).reshape(nd, 1, d)
    return yp, ys, cache_p, cache_s, win_new


def _ret_layer(xp, xs, li, state, past, w_in, gn_g, w_out, ln_g, ln_b, alpha, prev_state):
    nb, t, d = xp.shape
    nd = xs.shape[0]
    rdk = d // RH
    assert xs.shape[1] == 1 and t % RET_CHUNK == 0
    w_in = w_in.astype(MXU_DTYPE)
    w_out = w_out.astype(MXU_DTYPE)

    xp2 = xp.reshape(nb * t, d)
    tm = _row_tile(t)
    q, k, v, z = _ret_inproj(xp2, w_in, _xpos_tables(jnp.arange(t), rdk), t, tm)
    dmask, q_dec, k_dec, c_dec = _decay_tables(RET_CHUNK)
    dec = jnp.stack([q_dec, k_dec, jnp.broadcast_to(c_dec[:, None], q_dec.shape)]
                    + [jnp.zeros_like(q_dec)] * 5, axis=-1)
    xs2 = xs.reshape(nd, d)
    tms = _row_tile(nd)
    q_s, k_s, v_s, z_s = _ret_inproj(xs2, w_in, _xpos_tables(jnp.full((nd,), past, jnp.int32), rdk), nd, tms)
    dmask_s, q_dec, k_dec, c_dec = _decay_tables(1)
    dec_s = jnp.stack([dmask_s[:, 0, 0], q_dec[:, 0], k_dec[:, 0], c_dec], axis=-1)

    o, sp, o_s, ss = _ret_mixer(q.reshape(nb, t, d), k.reshape(nb, t, d), v.reshape(nb, t, 2 * d), dmask, dec,
                                q_s, k_s, v_s, state, li, dec_s, gn_g, prev_state)
    yp = _outproj_ln(xp2, o.reshape(nb * t, 2 * d), z, w_out, ln_g, ln_b, alpha, tm).reshape(nb, t, d)
    ys = _outproj_ln(xs2, o_s.reshape(nd, 2 * d), z_s, w_out, ln_g, ln_b, alpha, tms).reshape(nd, 1, d)
    return yp, ys, sp, ss


def kernel(x_prompt, x_sample, cache_k_cmp, cache_v_cmp, cache_k_sel, cache_v_sel, cache_k_win, cache_v_win,
           state_ret, page_table, nsa_w_in, nsa_w_out, nsa_pe_k, nsa_w1_k, nsa_w2_k, nsa_pe_v, nsa_w1_v,
           nsa_w2_v, ret_w_in, ret_gn_g, ret_w_out, ln_g, ln_b):
    depth = ln_g.shape[0]
    alpha = (2.0 * depth) ** 0.25
    n_nsa = nsa_w_in.shape[0]
    past = page_table.shape[1] * PAGE_SIZE
    t = x_prompt.shape[1]
    pools = [_channel_major(c) for c in (cache_k_cmp, cache_v_cmp, cache_k_sel, cache_v_sel)]
    wins = [_channel_major(c) for c in (cache_k_win, cache_v_win)]
    xp, xs = x_prompt, x_sample
    cache_p, cache_s, win_new, ret_p, ret_s = [], [], [], [], []
    for i in range(depth):
        li = i // 2
        if i % 2 == 0:
            cmp_k = _compress_weights(nsa_pe_k[li], nsa_w1_k[li], nsa_w2_k[li])
            cmp_v = _compress_weights(nsa_pe_v[li], nsa_w1_v[li], nsa_w2_v[li])
            xp, xs, cache_p, cache_s, win_new = _nsa_layer(
                xp, xs, li, n_nsa, page_table, pools, wins, nsa_w_in[li], nsa_w_out[li], cmp_k, cmp_v,
                ln_g[i], ln_b[i], alpha, cache_p, cache_s, win_new)
        else:
            xp, xs, sp, ss = _ret_layer(xp, xs, li, state_ret, past, ret_w_in[li], ret_gn_g[li], ret_w_out[li],
                                        ln_g[i], ln_b[i], alpha, ret_s)
            ret_p.append(sp)
            ret_s = [ss]
    wb = min(WINDOW, t)
    out_p = [_token_major(c) for c in cache_p[:4]] + [_token_major(c[..., t - wb:]) for c in cache_p[4:]]
    out_s = [jnp.moveaxis(_token_major(c), 2, 1) for c in cache_s[:4]] + [_token_major(w) for w in win_new]
    return (xp, xs,
            out_p[0], out_s[0], out_p[1], out_s[1], out_p[2], out_s[2], out_p[3], out_s[3],
            out_p[4], out_s[4], out_p[5], out_s[5],
            jnp.stack(ret_p), ret_s[0])
```

```python
import functools

import jax
import jax.numpy as jnp
from jax import lax
from jax.experimental import pallas as pl
from jax.experimental.pallas import tpu as pltpu

F32 = jnp.float32
MXU_DTYPE = jnp.bfloat16

HQ, KVH, DH = 32, 4, 64
GQ = HQ // KVH
NSA_W, KV_W = HQ * DH, KVH * DH
ROT_DIM, ROPE_THETA = DH // 4, 500000.0
CMP_LEN, CMP_STRIDE, CMP_HID = 32, 16, 2 * DH
SEL_BLK, N_SEL, WINDOW, QBLK = 64, 16, 512, 128
FORCE_BONUS, NEG = 1000.0, -1.0e30
RH, RET_CHUNK, XPOS_BASE = 4, 128, 10000.0
PAGE_SIZE, LN_EPS = 128, 1e-5

LANES = 128
VMEM_LIMIT_BYTES = 56 * 1024 * 1024
CMP_GROUP = 2048
GATE_PAD = LANES
KEY_TILE = 2 * SEL_BLK
SEL_TILE = 4 * KEY_TILE
BLOCKS_PER_TILE = SEL_TILE // SEL_BLK
WIN_TILE = WINDOW + QBLK
ONES_ROWS = 16
LOG2E = 1.4426950408889634
REMOVED = -3.0e38
RUNNING_MAX_INIT = -1.0e29


def _params(n_axes):
    return pltpu.CompilerParams(dimension_semantics=("arbitrary",) * n_axes,
                                vmem_limit_bytes=VMEM_LIMIT_BYTES)


def _dot(a, b):
    return jnp.dot(a, b, preferred_element_type=F32)


def _dot_nt(a, b):
    return lax.dot_general(a, b, (((1,), (1,)), ((), ())), preferred_element_type=F32)


def _dot_tn(a, b):
    return lax.dot_general(a, b, (((0,), (0,)), ((), ())), preferred_element_type=F32)


def _silu(z):
    return z * (1.0 / (1.0 + jnp.exp(-z)))


def _lane_tile(t, reps):
    return t if reps == 1 else jnp.concatenate([t] * reps, axis=1)


def _rotate(a, c, sa, sb, shift):
    n = a.shape[1]
    reps = n // c.shape[1]
    c, sa, sb = _lane_tile(c, reps), _lane_tile(sa, reps), _lane_tile(sb, reps)
    return a * c + pltpu.roll(a, shift, 1) * sa + pltpu.roll(a, n - shift, 1) * sb


def _masked_softmax(s, mask):
    s = jnp.where(mask, s, NEG)
    m = jnp.max(s, axis=-1, keepdims=True)
    e = jnp.where(mask, jnp.exp2(s - m), 0.0)
    d = jnp.sum(e, axis=-1, keepdims=True)
    return e * (1.0 / jnp.where(d > 0.0, d, 1.0))


def _split3(x):
    hi = x.astype(MXU_DTYPE)
    r = x - hi.astype(F32)
    mid = r.astype(MXU_DTYPE)
    lo = (r - mid.astype(F32)).astype(MXU_DTYPE)
    return hi, mid, lo


def _topk_mask(score, idx, n_pick, axis):
    big = jnp.float32(score.shape[axis])
    for _ in range(n_pick):
        m = jnp.max(score, axis=axis, keepdims=True)
        first = jnp.min(jnp.where(score == m, idx, big), axis=axis, keepdims=True)
        score = jnp.where(idx == first, REMOVED, score)
    return (score == REMOVED).astype(F32)


def _overlap(c_idx, s_idx):
    cs, ss = c_idx * CMP_STRIDE, s_idx * SEL_BLK
    ov = jnp.minimum(cs + CMP_LEN, ss + SEL_BLK) - jnp.maximum(cs, ss)
    return jnp.maximum(ov, 0).astype(F32) * (1.0 / CMP_LEN)


def _alias_specs(prev):
    return [pl.BlockSpec(memory_space=pl.ANY)] * len(prev)


def _nsa_inproj_kernel(x_ref, w_ref, c_ref, sa_ref, sb_ref, *rest):
    q_ref, kv_ref, ks_ref, kw_ref, vt_ref, gt_ref, z_ref = rest[-13:-6]
    cache_refs = rest[-6:]
    xb = x_ref[...].astype(MXU_DTYPE)
    c, sa, sb = c_ref[...], sa_ref[...], sb_ref[...]
    half = ROT_DIM // 2
    tm = x_ref.shape[0]
    token = pl.program_id(0) * tm + lax.broadcasted_iota(jnp.int32, (tm, DH), 0)
    blk_onehot = ((token // SEL_BLK) % BLOCKS_PER_TILE
                  == lax.broadcasted_iota(jnp.int32, (tm, DH), 1)).astype(F32)
    col = 0
    for ch in range(NSA_W // 512):
        a = _dot(xb, w_ref[:, col:col + 512])
        a = _rotate(a, c, sa, sb, half) * (DH ** -0.5 * LOG2E)
        for h in range(512 // DH):
            q_ref[0, ch * (512 // DH) + h] = a[:, h * DH:(h + 1) * DH].astype(q_ref.dtype)
        col += 512
    for j in range(6):
        a = _dot(xb, w_ref[:, col:col + KV_W])
        if j % 2 == 0:
            a = _rotate(a, c, sa, sb, half)
        kv_ref[j] = a
        a_t = a.T
        cache_refs[j][0, 0] = a_t
        if j == 2:
            for h in range(KVH):
                ks_ref[0, h] = jnp.concatenate([a[:, h * DH:(h + 1) * DH], blk_onehot], axis=1).astype(ks_ref.dtype)
        if j == 4:
            for h in range(KVH):
                kw_ref[0, h] = a[:, h * DH:(h + 1) * DH].astype(kw_ref.dtype)
        if j in (3, 5):
            vt_ref[j // 2 - 1, 0] = a_t.astype(vt_ref.dtype)
        col += KV_W
    a = _dot(xb, w_ref[:, col:col + GATE_PAD])
    gt_ref[0] = (1.0 / (1.0 + jnp.exp(-a))).T
    col += GATE_PAD
    for ch in range(NSA_W // 512):
        z_ref[:, ch * 512:(ch + 1) * 512] = _silu(_dot(xb, w_ref[:, col:col + 512])).astype(z_ref.dtype)
        col += 512


def _nsa_inproj(x2d, w, tabs, nb, t, tm, li, n_layers, prev):
    m, d = x2d.shape
    nt = t // tm
    n_cols = w.shape[1]
    row = lambda i: (i, 0)
    tab = pl.BlockSpec((tm, LANES), lambda i: (i % nt, 0))
    cache_spec = pl.BlockSpec((1, 1, KV_W, tm), lambda i: (li, i // nt, 0, i % nt))
    cache_shape = jax.ShapeDtypeStruct((n_layers, nb, KV_W, t), F32)
    n_in = 5
    return pl.pallas_call(
        _nsa_inproj_kernel,
        grid=(m // tm,),
        in_specs=[pl.BlockSpec((tm, d), row), pl.BlockSpec((d, n_cols), lambda i: (0, 0)), tab, tab, tab]
        + _alias_specs(prev),
        out_specs=[
            pl.BlockSpec((1, HQ, tm, DH), lambda i: (i // nt, 0, i % nt, 0)),
            pl.BlockSpec((6, tm, KV_W), lambda i: (0, i, 0)),
            pl.BlockSpec((1, KVH, tm, 2 * DH), lambda i: (i // nt, 0, i % nt, 0)),
            pl.BlockSpec((1, KVH, tm, DH), lambda i: (i // nt, 0, i % nt, 0)),
            pl.BlockSpec((2, 1, KV_W, tm), lambda i: (0, i // nt, 0, i % nt)),
            pl.BlockSpec((1, GATE_PAD, tm), lambda i: (i // nt, 0, i % nt)),
            pl.BlockSpec((tm, NSA_W), row),
        ] + [cache_spec] * 6,
        out_shape=[
            jax.ShapeDtypeStruct((nb, HQ, t, DH), MXU_DTYPE),
            jax.ShapeDtypeStruct((6, m, KV_W), F32),
            jax.ShapeDtypeStruct((nb, KVH, t, 2 * DH), MXU_DTYPE),
            jax.ShapeDtypeStruct((nb, KVH, t, DH), MXU_DTYPE),
            jax.ShapeDtypeStruct((2, nb, KV_W, t), MXU_DTYPE),
            jax.ShapeDtypeStruct((nb, GATE_PAD, t), F32),
            jax.ShapeDtypeStruct((m, NSA_W), MXU_DTYPE),
        ] + [cache_shape] * 6,
        input_output_aliases={n_in + j: 7 + j for j in range(len(prev))},
        compiler_params=_params(1),
        name="nsa_inproj",
    )(x2d, w, *tabs, *prev)


def _compress_kernel(*refs, n_prefetch, n_pages, layout, n_chunks):
    split = KV_W // LANES
    pos = n_prefetch
    if n_pages:
        page_refs = refs[pos:pos + n_pages]
        pos += n_pages
    else:
        tok_ref, halo_ref = refs[pos:pos + 2]
        pos += 2
    wpair_ref, pe_ref, w1_ref, w2_ref, out_ref, r_ref = refs[pos:pos + 6]
    r_ref[:, n_chunks:, :] = jnp.zeros((split, 8, CMP_STRIDE * LANES), F32)
    if n_pages:
        tok_sc = refs[pos + 6]
        for p, ref in enumerate(page_refs):
            page = ref[0, 0].T
            for s in range(split):
                tok_sc[s, p * PAGE_SIZE:(p + 1) * PAGE_SIZE, :] = page[:, s * LANES:(s + 1) * LANES]
        for s in range(split):
            for l in range(CMP_STRIDE):
                r_ref[s, 0:n_chunks, l * LANES:(l + 1) * LANES] = tok_sc[s, pl.ds(l, n_chunks, stride=CMP_STRIDE), :]
    else:
        for s in range(split):
            for l in range(CMP_STRIDE):
                lanes = slice(l * LANES, (l + 1) * LANES)
                j = l * split + s
                r_ref[s, 0:n_chunks, lanes] = tok_ref[0, pl.ds(j, n_chunks, stride=CMP_STRIDE * split), :]
                r_ref[s, n_chunks:n_chunks + 1, lanes] = halo_ref[0, j:j + 1, :]
    heads_per_tile = LANES // DH
    ab = [_dot(r_ref[s].astype(MXU_DTYPE), wpair_ref[...]) for s in range(split)]
    pe8 = jnp.broadcast_to(pe_ref[...], (8, CMP_LEN * DH)).astype(MXU_DTYPE)
    bias = _dot(pe8, w1_ref[...])[0:1, :]
    w2 = w2_ref[...]
    toks = []
    for h in range(KVH):
        base = (h % heads_per_tile) * 2 * CMP_HID
        first_half = ab[h // heads_per_tile][0:n_chunks, base:base + CMP_HID]
        second_half = ab[h // heads_per_tile][1:n_chunks + 1, base + CMP_HID:base + 2 * CMP_HID]
        u = first_half + second_half + bias
        hid = 0.5 * u * (1.0 + jnp.tanh(0.7978845608028654 * (u + 0.044715 * (u * u * u))))
        toks.append(_dot(hid.astype(MXU_DTYPE), w2))
    if layout == "head_major":
        for h in range(KVH):
            out_ref[0, h] = toks[h].astype(out_ref.dtype)
    elif layout == "token_major":
        out_ref[0] = jnp.concatenate(toks, axis=1).astype(out_ref.dtype)
    else:
        out_ref[0] = jnp.concatenate(toks, axis=1).T.astype(out_ref.dtype)


def _compress_weights(pe, w1, w2):
    heads = LANES // DH
    w1r = w1.reshape(2, CMP_STRIDE, DH, CMP_HID)
    eye = jnp.eye(heads, dtype=w1.dtype)
    wpair = jnp.einsum("sldn,hg->lhdgsn", w1r, eye).reshape(CMP_STRIDE * LANES, heads * 2 * CMP_HID)
    return (wpair.astype(MXU_DTYPE), pe.reshape(1, CMP_LEN * DH), w1.astype(MXU_DTYPE), w2.astype(MXU_DTYPE))


def _compress_prompt(kv_tok, group, weights, layout):
    _, nb, t, _ = kv_tok.shape
    gt = min(CMP_GROUP, t)
    n_chunks = gt // CMP_STRIDE
    ng = t // gt
    last_halo = t // CMP_STRIDE - 1
    wpair, pe, w1, w2 = weights
    split = KV_W // LANES
    k_tok = kv_tok.reshape(kv_tok.shape[0] * nb, t * split, LANES)
    first = group * nb
    full = lambda a: pl.BlockSpec(a.shape, lambda b, g: (0,) * a.ndim)
    kern = functools.partial(_compress_kernel, n_prefetch=0, n_pages=0, layout=layout, n_chunks=n_chunks)
    if layout == "head_major":
        out_spec = pl.BlockSpec((1, KVH, n_chunks, DH), lambda b, g: (b, 0, g, 0))
        out_shape = jax.ShapeDtypeStruct((nb, KVH, t // CMP_STRIDE, DH), MXU_DTYPE)
    else:
        out_spec = pl.BlockSpec((1, KV_W, n_chunks), lambda b, g: (b, 0, g))
        out_shape = jax.ShapeDtypeStruct((nb, KV_W, t // CMP_STRIDE), MXU_DTYPE)
    return pl.pallas_call(
        kern,
        grid=(nb, ng),
        in_specs=[pl.BlockSpec((1, gt * split, LANES), lambda b, g: (first + b, g, 0)),
                  pl.BlockSpec((1, CMP_STRIDE * split, LANES),
                               lambda b, g: (first + b, jnp.minimum((g + 1) * n_chunks, last_halo), 0)),
                  full(wpair), full(pe), full(w1), full(w2)],
        out_specs=out_spec,
        out_shape=out_shape,
        scratch_shapes=[pltpu.VMEM((split, n_chunks + 8, CMP_STRIDE * LANES), F32)],
        compiler_params=_params(2),
        name="nsa_compress_prompt",
    )(k_tok, k_tok, wpair, pe, w1, w2)


def _compress_paged_kernel(*refs, n_pages, n_chunks):
    n_in = n_pages + 4
    ins = refs[1:1 + 2 * n_in]
    outs = refs[1 + 2 * n_in:3 + 2 * n_in]
    scratch = refs[3 + 2 * n_in:]
    for s in range(2):
        _compress_kernel(*ins[s * n_in:(s + 1) * n_in], outs[s], *scratch[2 * s:2 * s + 2],
                         n_prefetch=0, n_pages=n_pages, layout="token_major", n_chunks=n_chunks)


def _compress_paged(pool_k, pool_v, li, page_table, weights_k, weights_v):
    nb, n_pages = page_table.shape
    n_chunks = n_pages * PAGE_SIZE // CMP_STRIDE
    split = KV_W // LANES
    full = lambda a: pl.BlockSpec(a.shape, lambda b, pt: (0,) * a.ndim)
    page = lambda p: pl.BlockSpec((1, 1, KV_W, PAGE_SIZE), lambda b, pt: (li, pt[b, p], 0, 0))
    stream_specs = lambda w: [page(p) for p in range(n_pages)] + [full(a) for a in w]
    out_spec = pl.BlockSpec((1, n_chunks, KV_W), lambda b, pt: (b, 0, 0))
    out_shape = jax.ShapeDtypeStruct((nb, n_chunks, KV_W), MXU_DTYPE)
    stream_scratch = [pltpu.VMEM((split, n_chunks + 8, CMP_STRIDE * LANES), F32),
                      pltpu.VMEM((split, n_pages * PAGE_SIZE, LANES), F32)]
    return pl.pallas_call(
        functools.partial(_compress_paged_kernel, n_pages=n_pages, n_chunks=n_chunks),
        grid_spec=pltpu.PrefetchScalarGridSpec(
            num_scalar_prefetch=1,
            grid=(nb,),
            in_specs=stream_specs(weights_k) + stream_specs(weights_v),
            out_specs=[out_spec, out_spec],
            scratch_shapes=stream_scratch + stream_scratch,
        ),
        out_shape=[out_shape, out_shape],
        compiler_params=_params(1),
        name="nsa_compress_paged",
    )(page_table, *([pool_k] * n_pages), *weights_k, *([pool_v] * n_pages), *weights_v)


def _nsa_attn_prompt_kernel(q_ref, kc_ref, vct_ref, ks_ref, vst_ref, kw_ref, vwt_ref, gc_ref, gs_ref, gw_ref,
                            o_ref, ot_sc, m_sc, alpha_sc, acc_sc, p_sc, s_sc, lanes_sc, *, seq_len):
    st = pl.program_id(2) * QBLK
    ncp = kc_ref.shape[2]
    ns = seq_len // SEL_BLK
    n_pairs = GQ // 2
    qpos = st + lax.broadcasted_iota(jnp.int32, (1, QBLK), 1)
    q_pairs = [q_ref[0, 2 * j:2 * j + 2].reshape(2 * QBLK, DH) for j in range(n_pairs)]
    both = lambda b: jnp.concatenate([b, b], axis=1)

    c_row = lax.broadcasted_iota(jnp.int32, (ncp, 1), 0)
    bias_c = both(jnp.where(c_row * CMP_STRIDE + (CMP_LEN - 1) <= qpos, 0.0, NEG))
    sees_any = both(qpos >= CMP_LEN - 1)
    kc, vct = kc_ref[0, 0], vct_ref[0]
    ws = pl.multiple_of(jnp.maximum(st - WINDOW, 0), KEY_TILE)
    w_row = lax.broadcasted_iota(jnp.int32, (WIN_TILE, 1), 0)
    dist = qpos - (ws + w_row)
    bias_w = both(jnp.where((dist >= 0) & (dist <= WINDOW), 0.0, NEG))
    kw_tile = kw_ref[0, 0, pl.ds(ws, WIN_TILE), :]
    vwt_tile = jnp.concatenate([vwt_ref[0, 0, :, pl.ds(ws, WIN_TILE)], jnp.ones((ONES_ROWS, WIN_TILE), MXU_DTYPE)],
                               axis=0)
    c_scores = [_dot_nt(kc, q_pairs[j]) for j in range(n_pairs)]
    w_scores = [_dot_nt(kw_tile, q_pairs[j]) for j in range(n_pairs)]
    ks_first = ks_ref[0, 0, 0:SEL_TILE, 0:DH]
    for j in range(n_pairs):
        s_sc[j] = _dot_nt(ks_first, q_pairs[j])
    p_sum = jnp.zeros((ncp, QBLK), F32)
    c_probs, w_probs = [], []
    for j in range(n_pairs):
        s = c_scores[j] + bias_c
        e = jnp.exp2(s - jnp.max(s, axis=0, keepdims=True))
        den = jnp.sum(e, axis=0, keepdims=True)
        p = e * jnp.where(sees_any, 1.0 / den, 0.0)
        c_probs.append(p.astype(MXU_DTYPE))
        p_sum = p_sum + p[:, :QBLK] + p[:, QBLK:]
    for j in range(n_pairs):
        s = w_scores[j] + bias_w
        w_probs.append(jnp.exp2(s - jnp.max(s, axis=0, keepdims=True)).astype(MXU_DTYPE))
    o_c = [_dot(vct, p) for p in c_probs]
    o_w = [_dot(vwt_tile, p) for p in w_probs]
    o_w = [o[:DH] * (1.0 / o[DH:DH + 1]) for o in o_w]

    s_idx = lax.broadcasted_iota(jnp.int32, (ns, 1), 0)
    ov_t = _overlap(lax.broadcasted_iota(jnp.int32, (1, ncp), 1), s_idx).astype(MXU_DTYPE)
    imp_t = sum(_dot(ov_t, part) for part in _split3(p_sum))
    blk_q = qpos // SEL_BLK
    forced = (s_idx == 0) | (s_idx == blk_q) | (s_idx == blk_q - 1)
    valid = s_idx * SEL_BLK <= qpos
    score = jnp.where(valid, imp_t + FORCE_BONUS * forced.astype(F32), NEG)
    sel_t = _topk_mask(score, s_idx.astype(F32), min(N_SEL, ns), 0)
    sel_bias = jnp.where(valid & (sel_t > 0.5), 0.0, NEG)
    sel_bias_q = sel_bias.T.astype(MXU_DTYPE)

    n_tiles = (st + QBLK + SEL_TILE - 1) // SEL_TILE
    m_sc[...] = jnp.full(m_sc.shape, RUNNING_MAX_INIT, F32)
    acc_sc[...] = jnp.zeros(acc_sc.shape, F32)
    alpha_sc[...] = jnp.ones(alpha_sc.shape, F32)
    p_sc[...] = jnp.zeros(p_sc.shape, MXU_DTYPE)
    ones_rows = jnp.ones((ONES_ROWS, SEL_TILE), MXU_DTYPE)
    first_bias = both(jnp.concatenate(
        [jnp.broadcast_to(sel_bias[b:b + 1, :], (SEL_BLK, QBLK)) for b in range(BLOCKS_PER_TILE)], axis=0))
    for j in range(n_pairs):
        s_sc[j] = s_sc[j] + first_bias
    blk_row = lax.broadcasted_iota(jnp.int32, (ns, DH), 0)
    blk_lane = lax.broadcasted_iota(jnp.int32, (ns, DH), 1)

    def key_start(i):
        return pl.multiple_of(i * SEL_TILE, SEL_TILE)

    def fold(i):
        vt_tile = jnp.concatenate([vst_ref[0, 0, :, pl.ds(key_start(i), SEL_TILE)], ones_rows], axis=0)
        for j in range(n_pairs):
            acc_sc[j] = alpha_sc[j] * acc_sc[j] + _dot(vt_tile, p_sc[j])

    def bias_lanes(i):
        pick = ((blk_row - i * BLOCKS_PER_TILE == blk_lane) & (blk_lane < BLOCKS_PER_TILE)).astype(MXU_DTYPE)
        return _dot(sel_bias_q, pick).astype(MXU_DTYPE)

    def masked_queries(lanes):
        return [jnp.concatenate([jnp.concatenate([q_ref[0, 2 * j + u], lanes], axis=1) for u in range(2)], axis=0)
                for j in range(n_pairs)]

    def softmax(j, s):
        m_old = m_sc[j]
        m_new = jnp.maximum(m_old, jnp.max(s, axis=0, keepdims=True))
        p_sc[j] = jnp.exp2(s - m_new).astype(MXU_DTYPE)
        alpha_sc[j] = jnp.exp2(m_old - m_new)
        m_sc[j] = m_new

    lanes_sc[...] = bias_lanes(1)

    def step(i, carry):
        fold(jnp.maximum(i - 1, 0))
        k_ahead = ks_ref[0, 0, pl.ds(key_start(i + 1), SEL_TILE), :]
        q_ahead = masked_queries(lanes_sc[...])
        for j in range(n_pairs):
            ahead = _dot_nt(k_ahead, q_ahead[j])
            softmax(j, s_sc[j])
            s_sc[j] = ahead
        lanes_sc[...] = bias_lanes(i + 2)
        return carry

    lax.fori_loop(0, n_tiles - 1, step, 0)
    fold(jnp.maximum(n_tiles - 2, 0))
    k_row = (n_tiles - 1) * SEL_TILE + lax.broadcasted_iota(jnp.int32, (SEL_TILE, 1), 0)
    causal = both(jnp.where(k_row <= qpos, 0.0, NEG))
    for j in range(n_pairs):
        softmax(j, s_sc[j] + causal)
    fold(n_tiles - 1)
    o_s = [acc_sc[j, :DH] * (1.0 / acc_sc[j, DH:DH + 1]) for j in range(n_pairs)]
    gc, gs, gw = gc_ref[0], gs_ref[0], gw_ref[0]
    for h in range(GQ):
        j, cols = h // 2, slice((h % 2) * QBLK, (h % 2 + 1) * QBLK)
        ot_sc[h * DH:(h + 1) * DH, :] = (gc[h:h + 1] * o_c[j][:, cols] + gs[h:h + 1] * o_s[j][:, cols]
                                         + gw[h:h + 1] * o_w[j][:, cols])
    o_ref[0] = ot_sc[...].T.astype(o_ref.dtype)


def _nsa_attn_prompt(q_hm, kc_hm, vc_t, ks_aug, kw_hm, v_t, gates_t):
    nb, _, t, _ = q_hm.shape
    ncp = kc_hm.shape[2]
    k_spec = lambda width: pl.BlockSpec((1, 1, t, width), lambda b, h, i: (b, h, 0, 0))
    vt_spec = lambda j: pl.BlockSpec((1, 1, DH, t), lambda b, h, i: (j, b, h, 0))
    gate_spec = lambda br: pl.BlockSpec((1, GQ, QBLK), lambda b, h, i: (b, br * KVH + h, i))
    kern = functools.partial(_nsa_attn_prompt_kernel, seq_len=t)
    return pl.pallas_call(
        kern,
        grid=(nb, KVH, t // QBLK),
        in_specs=[pl.BlockSpec((1, GQ, QBLK, DH), lambda b, h, i: (b, h, i, 0)),
                  pl.BlockSpec((1, 1, ncp, DH), lambda b, h, i: (b, h, 0, 0)),
                  pl.BlockSpec((1, DH, ncp), lambda b, h, i: (b, h, 0)),
                  k_spec(2 * DH), vt_spec(0), k_spec(DH), vt_spec(1),
                  gate_spec(0), gate_spec(1), gate_spec(2)],
        out_specs=pl.BlockSpec((1, QBLK, GQ * DH), lambda b, h, i: (b, i, h)),
        out_shape=jax.ShapeDtypeStruct((nb, t, NSA_W), MXU_DTYPE),
        scratch_shapes=[pltpu.VMEM((GQ * DH, QBLK), F32),
                        pltpu.VMEM((GQ // 2, 1, 2 * QBLK), F32), pltpu.VMEM((GQ // 2, 1, 2 * QBLK), F32),
                        pltpu.VMEM((GQ // 2, DH + ONES_ROWS, 2 * QBLK), F32),
                        pltpu.VMEM((GQ // 2, SEL_TILE, 2 * QBLK), MXU_DTYPE),
                        pltpu.VMEM((GQ // 2, SEL_TILE, 2 * QBLK), F32),
                        pltpu.VMEM((QBLK, DH), MXU_DTYPE)],
        compiler_params=_params(3),
        name="nsa_attn_prompt",
    )(q_hm, kc_hm, vc_t, ks_aug, v_t, kw_hm, v_t, gates_t, gates_t, gates_t)


def _block_diag_fold(x):
    r_kv = lax.broadcasted_iota(jnp.int32, (HQ, KV_W), 0) // GQ
    l_kv = lax.broadcasted_iota(jnp.int32, (HQ, KV_W), 1) // DH
    x = jnp.where(r_kv == l_kv, x, 0.0)
    return sum(x[..., h * DH:(h + 1) * DH] for h in range(KVH))


def _nsa_select_kernel(q_ref, kc_ref, vc_ref, oc_ref, sel_ref, *, past):
    bb, ncp = kc_ref.shape[0], kc_ref.shape[1]
    nsp = sel_ref.shape[1]
    q = q_ref[...]
    s = jnp.einsum("bqd,bkd->bqk", q, kc_ref[...], preferred_element_type=F32)
    c_idx = lax.broadcasted_iota(jnp.int32, (1, 1, ncp), 2)
    p_c = _masked_softmax(s, c_idx * CMP_STRIDE + (CMP_LEN - 1) <= past)
    o_c = jnp.einsum("bqk,bkd->bqd", p_c.astype(MXU_DTYPE), vc_ref[...], preferred_element_type=F32)
    oc_ref[...] = _block_diag_fold(o_c)
    ov = _overlap(lax.broadcasted_iota(jnp.int32, (ncp, 1), 0),
                  lax.broadcasted_iota(jnp.int32, (1, nsp), 1)).astype(MXU_DTYPE)
    imp_rows = sum(_dot(part, ov) for part in _split3(p_c.reshape(bb * HQ, ncp)))
    imp = jnp.sum(imp_rows.reshape(bb * KVH, GQ, nsp), axis=1)
    s_idx = lax.broadcasted_iota(jnp.int32, (1, nsp), 1)
    blk_q = past // SEL_BLK
    forced = (s_idx == 0) | (s_idx == blk_q) | (s_idx == blk_q - 1)
    valid = s_idx * SEL_BLK <= past
    score = jnp.where(valid, imp + FORCE_BONUS * forced.astype(F32), NEG)
    n_blocks = past // SEL_BLK + 1
    sel = _topk_mask(score, s_idx.astype(F32), min(N_SEL, n_blocks), 1)
    sel_ref[...] = jnp.where(valid, sel, 0.0)


def _nsa_select(q_bd, kc_tm, vc_tm, past, bb):
    nb, ncp, _ = kc_tm.shape
    nsp = LANES
    assert past // SEL_BLK + 1 <= nsp
    kern = functools.partial(_nsa_select_kernel, past=past)
    blk = lambda shape: pl.BlockSpec(shape, lambda i: (i, 0, 0))
    return pl.pallas_call(
        kern,
        grid=(nb // bb,),
        in_specs=[blk((bb, HQ, KV_W)), blk((bb, ncp, KV_W)), blk((bb, ncp, KV_W))],
        out_specs=[blk((bb, HQ, DH)), pl.BlockSpec((bb * KVH, nsp), lambda i: (i, 0))],
        out_shape=[jax.ShapeDtypeStruct((nb, HQ, DH), F32), jax.ShapeDtypeStruct((nb * KVH, nsp), F32)],
        compiler_params=_params(1),
        name="nsa_select_decode",
    )(q_bd, kc_tm, vc_tm)


def _nsa_attn_decode_kernel(*refs, n_pages, n_prev):
    q_ref, oc_ref, sel_ref, new_ref, newc_ref, g_ref = refs[1:7]
    k_pages = refs[7:7 + n_pages]
    v_pages = refs[7 + n_pages:7 + 2 * n_pages]
    pos = 7 + 2 * n_pages
    kwin_ref, vwin_ref = refs[pos:pos + 2]
    o_ref, kwin_out, vwin_out = refs[pos + 2 + n_prev:pos + 5 + n_prev]
    q = q_ref[0]
    qf = q.astype(F32)
    sel4 = sel_ref[0]
    sel_rows = jnp.concatenate([jnp.broadcast_to(sel4[k:k + 1], (GQ, sel4.shape[1])) for k in range(KVH)], axis=0)
    new = new_ref[0]
    first_half = lax.broadcasted_iota(jnp.int32, (1, PAGE_SIZE), 1) < SEL_BLK

    scores, masks = [], []
    for p in range(n_pages):
        scores.append(_dot(q, k_pages[p][0, 0].astype(MXU_DTYPE)))
        masks.append(jnp.where(first_half, sel_rows[:, 2 * p:2 * p + 1], sel_rows[:, 2 * p + 1:2 * p + 2]) > 0.5)
    own = 2 * n_pages
    s_new = jnp.sum(qf * new[0:1, :].astype(MXU_DTYPE).astype(F32), axis=-1, keepdims=True)
    ok_new = sel_rows[:, own:own + 1] > 0.5
    s_new = jnp.where(ok_new, s_new, NEG)
    m = s_new
    for s, ok in zip(scores, masks):
        m = jnp.maximum(m, jnp.max(jnp.where(ok, s, NEG), axis=-1, keepdims=True))
    e_new = jnp.where(ok_new, jnp.exp2(s_new - m), 0.0)
    den = e_new
    acc = e_new * new[1:2, :].astype(MXU_DTYPE).astype(F32)
    for p, (s, ok) in enumerate(zip(scores, masks)):
        e = jnp.where(ok, jnp.exp2(s - m), 0.0)
        den = den + jnp.sum(e, axis=-1, keepdims=True)
        acc = acc + _dot_nt(e.astype(MXU_DTYPE), v_pages[p][0, 0].astype(MXU_DTYPE))
    o_s = _block_diag_fold(acc * (1.0 / den))

    kwin, vwin = kwin_ref[0, 0], vwin_ref[0, 0]
    s_w = _dot(q, kwin.astype(MXU_DTYPE))
    s_wn = jnp.sum(qf * new[2:3, :].astype(MXU_DTYPE).astype(F32), axis=-1, keepdims=True)
    m_w = jnp.maximum(jnp.max(s_w, axis=-1, keepdims=True), s_wn)
    e_w, e_wn = jnp.exp2(s_w - m_w), jnp.exp2(s_wn - m_w)
    den_w = jnp.sum(e_w, axis=-1, keepdims=True) + e_wn
    acc_w = _dot_nt(e_w.astype(MXU_DTYPE), vwin.astype(MXU_DTYPE)) + e_wn * new[3:4, :].astype(MXU_DTYPE).astype(F32)
    o_w = _block_diag_fold(acc_w * (1.0 / den_w))

    g = g_ref[0]
    o_ref[0] = g[:, 0:1] * oc_ref[0] + g[:, 1:2] * o_s + g[:, 2:3] * o_w

    wb = kwin.shape[1]
    last = lax.broadcasted_iota(jnp.int32, (1, wb), 1) == wb - 1
    kwin_out[0, 0] = jnp.where(last, newc_ref[0, 0], pltpu.roll(kwin, wb - 1, 1))
    vwin_out[0, 0] = jnp.where(last, newc_ref[0, 1], pltpu.roll(vwin, wb - 1, 1))


def _nsa_attn_decode(page_table, q_bd, o_c, sel, new_rows, new_cols, gates, pool_k, pool_v, win_k, win_v,
                     li, prev):
    nb, n_pages = page_table.shape
    n_layers, _, _, wb = win_k.shape
    nsp = sel.shape[-1]
    per_b = lambda shape: pl.BlockSpec((1,) + shape, lambda b, pt: (b,) + (0,) * len(shape))
    page = lambda p: pl.BlockSpec((1, 1, KV_W, PAGE_SIZE), lambda b, pt: (li, pt[b, p], 0, 0))
    win = pl.BlockSpec((1, 1, KV_W, wb), lambda b, pt: (li, b, 0, 0))
    kern = functools.partial(_nsa_attn_decode_kernel, n_pages=n_pages, n_prev=len(prev))
    n_in = 7 + 2 * n_pages + 2
    win_shape = jax.ShapeDtypeStruct((n_layers, nb, KV_W, wb), F32)
    return pl.pallas_call(
        kern,
        grid_spec=pltpu.PrefetchScalarGridSpec(
            num_scalar_prefetch=1,
            grid=(nb,),
            in_specs=([per_b((HQ, KV_W)), per_b((HQ, DH)), per_b((KVH, nsp)), per_b((4, KV_W)),
                       per_b((2, KV_W, 1)), per_b((HQ, 3))]
                      + [page(p) for p in range(n_pages)] + [page(p) for p in range(n_pages)]
                      + [win, win] + _alias_specs(prev)),
            out_specs=[per_b((HQ, DH)), win, win],
        ),
        out_shape=[jax.ShapeDtypeStruct((nb, HQ, DH), F32), win_shape, win_shape],
        input_output_aliases={n_in + j: 1 + j for j in range(len(prev))},
        compiler_params=_params(1),
        name="nsa_attn_decode",
    )(page_table, q_bd, o_c, sel.reshape(nb, KVH, nsp), new_rows, new_cols, gates,
      *([pool_k] * n_pages), *([pool_v] * n_pages), win_k, win_v, *prev)


def _outproj_ln_kernel(x_ref, o_ref, z_ref, w_ref, g_ref, b_ref, y_ref, *, alpha):
    a = o_ref[...].astype(F32) * z_ref[...].astype(F32)
    h = alpha * x_ref[...] + _dot(a.astype(MXU_DTYPE), w_ref[...])
    d = h - jnp.mean(h, axis=-1, keepdims=True)
    var = jnp.mean(d * d, axis=-1, keepdims=True)
    y_ref[...] = d * lax.rsqrt(var + LN_EPS) * g_ref[...] + b_ref[...]


def _outproj_ln(x2d, o2d, z2d, w, ln_g, ln_b, alpha, tm):
    m, d = x2d.shape
    wdt = o2d.shape[1]
    if m % (2 * tm) == 0:
        tm = 2 * tm
    row = lambda n: pl.BlockSpec((tm, n), lambda i: (i, 0))
    fixed = lambda shape: pl.BlockSpec(shape, lambda i: (0, 0))
    return pl.pallas_call(
        functools.partial(_outproj_ln_kernel, alpha=alpha),
        grid=(m // tm,),
        in_specs=[row(d), row(wdt), row(wdt), fixed((wdt, d)), fixed((1, d)), fixed((1, d))],
        out_specs=row(d),
        out_shape=jax.ShapeDtypeStruct((m, d), F32),
        compiler_params=_params(1),
        name="outproj_ln",
    )(x2d, o2d, z2d, w, ln_g.reshape(1, d), ln_b.reshape(1, d))


def _ret_inproj_kernel(x_ref, w_ref, c_ref, sa_ref, sb_ref, q_ref, k_ref, v_ref, z_ref, *, rdk):
    xb = x_ref[...].astype(MXU_DTYPE)
    c, sa, sb = c_ref[...], sa_ref[...], sb_ref[...]
    qk_w = RH * rdk
    col = 0
    for dst, scale in ((q_ref, 1.0), (k_ref, rdk ** -0.5)):
        for ch in range(qk_w // 512):
            a = _rotate(_dot(xb, w_ref[:, col:col + 512]), c, sa, sb, 1)
            dst[:, ch * 512:(ch + 1) * 512] = a * scale
            col += 512
    for dst, act in ((v_ref, lambda a: a), (z_ref, _silu)):
        for ch in range(2 * qk_w // 512):
            dst[:, ch * 512:(ch + 1) * 512] = act(_dot(xb, w_ref[:, col:col + 512])).astype(dst.dtype)
            col += 512


def _ret_inproj(x2d, w, tabs, t, tm):
    m, d = x2d.shape
    nt = t // tm
    rdk = d // RH
    row = lambda n: pl.BlockSpec((tm, n), lambda i: (i, 0))
    tab = pl.BlockSpec((tm, rdk), lambda i: (i % nt, 0))
    return pl.pallas_call(
        functools.partial(_ret_inproj_kernel, rdk=rdk),
        grid=(m // tm,),
        in_specs=[row(d), pl.BlockSpec(w.shape, lambda i: (0, 0)), tab, tab, tab],
        out_specs=[row(d), row(d), row(2 * d), row(2 * d)],
        out_shape=[jax.ShapeDtypeStruct((m, d), F32), jax.ShapeDtypeStruct((m, d), F32),
                   jax.ShapeDtypeStruct((m, 2 * d), F32), jax.ShapeDtypeStruct((m, 2 * d), MXU_DTYPE)],
        compiler_params=_params(1),
        name="ret_inproj",
    )(x2d, w, *tabs)


def _group_norm(o, gain):
    d = o - jnp.mean(o, axis=-1, keepdims=True)
    return d * lax.rsqrt(jnp.mean(d * d, axis=-1, keepdims=True) + LN_EPS) * gain


def _ret_prompt_kernel(q_ref, k_ref, v_ref, dm_ref, dec_ref, gn_ref, o_ref, s_ref, s_sc, *, rdk, rdv):
    ci = pl.program_id(1)

    @pl.when(ci == 0)
    def _():
        s_sc[...] = jnp.zeros(s_sc.shape, F32)

    heads = range(RH)
    q = [q_ref[0, :, h * rdk:(h + 1) * rdk] for h in heads]
    k = [k_ref[0, :, h * rdk:(h + 1) * rdk] for h in heads]
    v = [v_ref[0, :, h * rdv:(h + 1) * rdv].astype(MXU_DTYPE) for h in heads]
    dec = [dec_ref[h] for h in heads]
    s_old = [s_sc[h] for h in heads]
    inner = [_dot_nt(q[h].astype(MXU_DTYPE), k[h].astype(MXU_DTYPE)) for h in heads]
    cross = [_dot((q[h] * dec[h][:, 0:1]).astype(MXU_DTYPE), s_old[h].astype(MXU_DTYPE)) for h in heads]
    outer = [_dot_tn((k[h] * dec[h][:, 1:2]).astype(MXU_DTYPE), v[h]) for h in heads]
    intra = [_dot((inner[h] * dm_ref[h]).astype(MXU_DTYPE), v[h]) for h in heads]
    for h in heads:
        s_sc[h] = s_old[h] * dec[h][0:1, 2:3] + outer[h]
        o_ref[0, :, h * rdv:(h + 1) * rdv] = _group_norm(
            intra[h] + cross[h], gn_ref[:, h * rdv:(h + 1) * rdv]).astype(o_ref.dtype)

    @pl.when(ci == pl.num_programs(1) - 1)
    def _():
        s_ref[0] = s_sc[...]


def _ret_decode_kernel(dec_ref, q_ref, k_ref, v_ref, s0_ref, gn_ref, o_ref, s_ref, *, rdk, rdv):
    pad = LANES
    first_row = lax.broadcasted_iota(jnp.int32, (pad, 1), 0) == 0
    for r in range(q_ref.shape[0]):
        for h in range(RH):
            q = q_ref[r, :, h * rdk:(h + 1) * rdk]
            k = k_ref[r, :, h * rdk:(h + 1) * rdk]
            v = v_ref[r, :, h * rdv:(h + 1) * rdv]
            dmask, q_dec, k_dec, c_dec = dec_ref[h, 0], dec_ref[h, 1], dec_ref[h, 2], dec_ref[h, 3]
            s0 = s0_ref[0, r, h]
            k_pad = jnp.where(first_row, jnp.broadcast_to(k * k_dec, (pad, rdk)), 0.0).astype(MXU_DTYPE)
            v_pad = jnp.broadcast_to(v, (pad, rdv)).astype(MXU_DTYPE)
            s_ref[0, r, h] = s0 * c_dec + _dot_tn(k_pad, v_pad)
            qk = jnp.sum(q.astype(MXU_DTYPE).astype(F32) * k.astype(MXU_DTYPE).astype(F32), axis=-1, keepdims=True)
            inner = (qk * dmask).astype(MXU_DTYPE).astype(F32)
            q_pad = jnp.broadcast_to(q * q_dec, (8, rdk)).astype(MXU_DTYPE)
            o = inner * v.astype(MXU_DTYPE).astype(F32) + _dot(q_pad, s0.astype(MXU_DTYPE))[0:1, :]
            o_ref[r, :, h * rdv:(h + 1) * rdv] = _group_norm(o, gn_ref[:, h * rdv:(h + 1) * rdv])


def _ret_mixer_kernel(dec_s_ref, q_ref, k_ref, v_ref, dm_ref, dec_ref, gn_ref, qs_ref, ks_ref, vs_ref, s0_ref,
                      *rest, rdk, rdv):
    o_ref, sp_ref, os_ref, ss_ref, s_sc = rest[-5:]
    _ret_prompt_kernel(q_ref, k_ref, v_ref, dm_ref, dec_ref, gn_ref, o_ref, sp_ref, s_sc, rdk=rdk, rdv=rdv)
    _ret_decode_kernel(dec_s_ref, qs_ref, ks_ref, vs_ref, s0_ref, gn_ref, os_ref, ss_ref, rdk=rdk, rdv=rdv)


def _ret_mixer(q, k, v, dmask, dec, qs, ks, vs, state, li, dec_s, gn_g, prev):
    nb, t, qk_w = q.shape
    nd = qs.shape[0]
    rdk = qk_w // RH
    rdv = v.shape[2] // RH
    c = dmask.shape[1]
    nch = t // c
    assert nd % (nb * nch) == 0
    per = nd // (nb * nch)
    tok = lambda n: pl.BlockSpec((1, c, n), lambda b, i: (b, i, 0))
    full = lambda a: pl.BlockSpec(a.shape, lambda b, i: (0,) * a.ndim)
    vec = lambda n: pl.BlockSpec((per, 1, n), lambda b, i: (b * nch + i, 0, 0))
    st_spec = pl.BlockSpec((1, per, RH, rdk, rdv), lambda b, i: (li, b * nch + i, 0, 0, 0))
    gn_g = gn_g.reshape(1, RH * rdv)
    n_in = 11
    return pl.pallas_call(
        functools.partial(_ret_mixer_kernel, rdk=rdk, rdv=rdv),
        grid=(nb, nch),
        in_specs=[pl.BlockSpec(memory_space=pltpu.SMEM), tok(qk_w), tok(qk_w), tok(RH * rdv), full(dmask),
                  full(dec), full(gn_g), vec(qk_w), vec(qk_w), vec(RH * rdv), st_spec] + _alias_specs(prev),
        out_specs=[tok(RH * rdv), pl.BlockSpec((1, RH, rdk, rdv), lambda b, i: (b, 0, 0, 0)),
                   vec(RH * rdv), st_spec],
        out_shape=[jax.ShapeDtypeStruct((nb, t, RH * rdv), MXU_DTYPE),
                   jax.ShapeDtypeStruct((nb, RH, rdk, rdv), F32),
                   jax.ShapeDtypeStruct((nd, 1, RH * rdv), F32), jax.ShapeDtypeStruct(state.shape, F32)],
        input_output_aliases={n_in + j: 3 + j for j in range(len(prev))},
        scratch_shapes=[pltpu.VMEM((RH, rdk, rdv), F32)],
        compiler_params=_params(2),
        name="ret_mixer",
    )(dec_s, q, k, v, dmask, dec, gn_g, qs.reshape(nd, 1, qk_w), ks.reshape(nd, 1, qk_w),
      vs.reshape(nd, 1, RH * rdv), state, *prev)


def _rope_tables(pos):
    half = ROT_DIM // 2
    inv = ROPE_THETA ** (-jnp.arange(half, dtype=F32) / half)
    ang = pos.astype(F32)[:, None] * inv[None, :]
    lane = jnp.arange(LANES) % DH
    cos, sin = jnp.cos(ang)[:, lane % half], jnp.sin(ang)[:, lane % half]
    c = jnp.where(lane < ROT_DIM, cos, 1.0)
    sa = jnp.where((lane >= half) & (lane < ROT_DIM), sin, 0.0)
    sb = jnp.where(lane < half, -sin, 0.0)
    return c, sa, sb


def _xpos_tables(pos, rdk):
    half = rdk // 2
    inv = 1.0 / (XPOS_BASE ** jnp.linspace(0.0, 1.0, half, dtype=F32))
    ang = pos.astype(F32)[:, None] * inv[None, :]
    lane = jnp.arange(rdk)
    cos, sin = jnp.cos(ang)[:, lane // 2], jnp.sin(ang)[:, lane // 2]
    sa = jnp.where(lane % 2 == 1, sin, 0.0)
    sb = jnp.where(lane % 2 == 0, -sin, 0.0)
    return cos, sa, sb


def _decay_tables(c):
    log_g = jnp.log1p(-jnp.power(2.0, -5.0 - jnp.arange(RH, dtype=F32)))
    i = jnp.arange(c, dtype=F32)
    diff = i[:, None] - i[None, :]
    dmask = jnp.where(diff >= 0, jnp.exp(jnp.maximum(diff, 0.0)[None] * log_g[:, None, None]), 0.0)
    q_dec = jnp.exp((i + 1.0)[None, :] * log_g[:, None])
    k_dec = jnp.exp((c - 1.0 - i)[None, :] * log_g[:, None])
    c_dec = jnp.exp(c * log_g)
    return dmask, q_dec, k_dec, c_dec


def _nsa_w_in_layout(w):
    kv_end = NSA_W + 6 * KV_W
    gates = jnp.pad(w[:, kv_end:kv_end + 3 * HQ], ((0, 0), (0, GATE_PAD - 3 * HQ)))
    return jnp.concatenate([w[:, :kv_end], gates, w[:, kv_end + 3 * HQ:]], axis=1).astype(MXU_DTYPE)


def _channel_major(cache):
    return jnp.moveaxis(cache, -3, -1).reshape(cache.shape[:-3] + (KV_W, cache.shape[-3]))


def _token_major(slab):
    return jnp.moveaxis(slab.reshape(slab.shape[:-2] + (KVH, DH, slab.shape[-1])), -1, -3)


def _row_tile(m):
    for tm in (256, 128):
        if m % tm == 0:
            return tm
    raise ValueError(f"row count {m} is not a multiple of {LANES}")


def _nsa_layer(xp, xs, li, n_layers, page_table, pools, wins, w_in, w_out, cmp_k, cmp_v, ln_g, ln_b, alpha,
               prev_p, prev_s, prev_win):
    nb, t, d = xp.shape
    nd = xs.shape[0]
    past = page_table.shape[1] * PAGE_SIZE
    assert xs.shape[1] == 1 and t % SEL_TILE == 0 and t >= WIN_TILE
    w_in = _nsa_w_in_layout(w_in)
    w_out = w_out.astype(MXU_DTYPE)

    xp2 = xp.reshape(nb * t, d)
    tm = _row_tile(t)
    q_hm, kv, ks_aug, kw_hm, v_t, g_t, z, *cache_p = _nsa_inproj(xp2, w_in, _rope_tables(jnp.arange(t)), nb, t, tm,
                                                        li, n_layers, prev_p)
    kv = kv.reshape(6, nb, t, KV_W)
    kc_hm = _compress_prompt(kv, 0, cmp_k, "head_major")
    vc_t = _compress_prompt(kv, 1, cmp_v, "channel_major")
    o = _nsa_attn_prompt(q_hm, kc_hm, vc_t, ks_aug, kw_hm, v_t, g_t)
    yp = _outproj_ln(xp2, o.reshape(nb * t, NSA_W), z, w_out, ln_g, ln_b, alpha, tm).reshape(nb, t, d)

    xs2 = xs.reshape(nd, d)
    tms = _row_tile(nd)
    pos_s = jnp.full((nd,), past, jnp.int32)
    q_s, kv_s, _, _, _, g_s, z_s, *cache_s = _nsa_inproj(xs2, w_in, _rope_tables(pos_s), 1, nd, tms,
                                                      li, n_layers, prev_s)
    q_tok = q_s[0].transpose(1, 0, 2).astype(F32).reshape(nd, KVH, GQ, 1, DH)
    q_bd = (q_tok * jnp.eye(KVH, dtype=F32)[None, :, None, :, None]).reshape(nd, HQ, KV_W).astype(MXU_DTYPE)
    kc_tm, vc_tm = _compress_paged(pools[0], pools[1], li, page_table, cmp_k, cmp_v)
    o_c, sel = _nsa_select(q_bd, kc_tm, vc_tm, past, min(16, nd))
    new_rows = jnp.stack([kv_s[2], kv_s[3], kv_s[4], kv_s[5]], axis=1)
    new_cols = jnp.stack([kv_s[4], kv_s[5]], axis=1)[..., None]
    gates_s = g_s[0, :3 * HQ].reshape(3, HQ, nd).transpose(2, 1, 0)
    o_s, *win_new = _nsa_attn_decode(page_table, q_bd, o_c, sel, new_rows, new_cols, gates_s,
                                     pools[2], pools[3], wins[0], wins[1], li, prev_win)
    ys = _outproj_ln(xs2, o_s.reshape(nd, NSA_W), z_s, w_out, ln_g, ln_b, alpha, tms).reshape(nd, 1, d)
    return yp, ys, cache_p, cache_s, win_new


def _ret_layer(xp, xs, li, state, past, w_in, gn_g, w_out, ln_g, ln_b, alpha, prev_state):
    nb, t, d = xp.shape
    nd = xs.shape[0]
    rdk = d // RH
    assert xs.shape[1] == 1 and t % RET_CHUNK == 0
    w_in = w_in.astype(MXU_DTYPE)
    w_out = w_out.astype(MXU_DTYPE)

    xp2 = xp.reshape(nb * t, d)
    tm = _row_tile(t)
    q, k, v, z = _ret_inproj(xp2, w_in, _xpos_tables(jnp.arange(t), rdk), t, tm)
    dmask, q_dec, k_dec, c_dec = _decay_tables(RET_CHUNK)
    dec = jnp.stack([q_dec, k_dec, jnp.broadcast_to(c_dec[:, None], q_dec.shape)]
                    + [jnp.zeros_like(q_dec)] * 5, axis=-1)
    xs2 = xs.reshape(nd, d)
    tms = _row_tile(nd)
    q_s, k_s, v_s, z_s = _ret_inproj(xs2, w_in, _xpos_tables(jnp.full((nd,), past, jnp.int32), rdk), nd, tms)
    dmask_s, q_dec, k_dec, c_dec = _decay_tables(1)
    dec_s = jnp.stack([dmask_s[:, 0, 0], q_dec[:, 0], k_dec[:, 0], c_dec], axis=-1)

    o, sp, o_s, ss = _ret_mixer(q.reshape(nb, t, d), k.reshape(nb, t, d), v.reshape(nb, t, 2 * d), dmask, dec,
                                q_s, k_s, v_s, state, li, dec_s, gn_g, prev_state)
    yp = _outproj_ln(xp2, o.reshape(nb * t, 2 * d), z, w_out, ln_g, ln_b, alpha, tm).reshape(nb, t, d)
    ys = _outproj_ln(xs2, o_s.reshape(nd, 2 * d), z_s, w_out, ln_g, ln_b, alpha, tms).reshape(nd, 1, d)
    return yp, ys, sp, ss


def kernel(x_prompt, x_sample, cache_k_cmp, cache_v_cmp, cache_k_sel, cache_v_sel, cache_k_win, cache_v_win,
           state_ret, page_table, nsa_w_in, nsa_w_out, nsa_pe_k, nsa_w1_k, nsa_w2_k, nsa_pe_v, nsa_w1_v,
           nsa_w2_v, ret_w_in, ret_gn_g, ret_w_out, ln_g, ln_b):
    depth = ln_g.shape[0]
    alpha = (2.0 * depth) ** 0.25
    n_nsa = nsa_w_in.shape[0]
    past = page_table.shape[1] * PAGE_SIZE
    t = x_prompt.shape[1]
    pools = [_channel_major(c) for c in (cache_k_cmp, cache_v_cmp, cache_k_sel, cache_v_sel)]
    wins = [_channel_major(c) for c in (cache_k_win, cache_v_win)]
    xp, xs = x_prompt, x_sample
    cache_p, cache_s, win_new, ret_p, ret_s = [], [], [], [], []
    for i in range(depth):
        li = i // 2
        if i % 2 == 0:
            cmp_k = _compress_weights(nsa_pe_k[li], nsa_w1_k[li], nsa_w2_k[li])
            cmp_v = _compress_weights(nsa_pe_v[li], nsa_w1_v[li], nsa_w2_v[li])
            xp, xs, cache_p, cache_s, win_new = _nsa_layer(
                xp, xs, li, n_nsa, page_table, pools, wins, nsa_w_in[li], nsa_w_out[li], cmp_k, cmp_v,
                ln_g[i], ln_b[i], alpha, cache_p, cache_s, win_new)
        else:
            xp, xs, sp, ss = _ret_layer(xp, xs, li, state_ret, past, ret_w_in[li], ret_gn_g[li], ret_w_out[li],
                                        ln_g[i], ln_b[i], alpha, ret_s)
            ret_p.append(sp)
            ret_s = [ss]
    wb = min(WINDOW, t)
    out_p = [_token_major(c) for c in cache_p[:4]] + [_token_major(c[..., t - wb:]) for c in cache_p[4:]]
    out_s = [jnp.moveaxis(_token_major(c), 2, 1) for c in cache_s[:4]] + [_token_major(w) for w in win_new]
    return (xp, xs,
            out_p[0], out_s[0], out_p[1], out_s[1], out_p[2], out_s[2], out_p[3], out_s[3],
            out_p[4], out_s[4], out_p[5], out_s[5],
            jnp.stack(ret_p), ret_s[0])
```

```python
import functools

import jax
import jax.numpy as jnp
from jax import lax
from jax.experimental import pallas as pl
from jax.experimental.pallas import tpu as pltpu

F32 = jnp.float32
MXU_DTYPE = jnp.bfloat16

HQ, KVH, DH = 32, 4, 64
GQ = HQ // KVH
NSA_W, KV_W = HQ * DH, KVH * DH
ROT_DIM, ROPE_THETA = DH // 4, 500000.0
CMP_LEN, CMP_STRIDE, CMP_HID = 32, 16, 2 * DH
SEL_BLK, N_SEL, WINDOW, QBLK = 64, 16, 512, 128
FORCE_BONUS, NEG = 1000.0, -1.0e30
RH, RET_CHUNK, XPOS_BASE = 4, 128, 10000.0
PAGE_SIZE, LN_EPS = 128, 1e-5

LANES = 128
VMEM_LIMIT_BYTES = 56 * 1024 * 1024
CMP_GROUP = 2048
GATE_PAD = LANES
KEY_TILE = 2 * SEL_BLK
SEL_TILE = 4 * KEY_TILE
BLOCKS_PER_TILE = SEL_TILE // SEL_BLK
WIN_TILE = WINDOW + QBLK
ONES_ROWS = 16
LOG2E = 1.4426950408889634
REMOVED = -3.0e38
RUNNING_MAX_INIT = -1.0e29


def _params(n_axes):
    return pltpu.CompilerParams(dimension_semantics=("arbitrary",) * n_axes,
                                vmem_limit_bytes=VMEM_LIMIT_BYTES)


def _dot(a, b):
    return jnp.dot(a, b, preferred_element_type=F32)


def _dot_nt(a, b):
    return lax.dot_general(a, b, (((1,), (1,)), ((), ())), preferred_element_type=F32)


def _dot_tn(a, b):
    return lax.dot_general(a, b, (((0,), (0,)), ((), ())), preferred_element_type=F32)


def _silu(z):
    return z * (1.0 / (1.0 + jnp.exp(-z)))


def _lane_tile(t, reps):
    return t if reps == 1 else jnp.concatenate([t] * reps, axis=1)


def _rotate(a, c, sa, sb, shift):
    n = a.shape[1]
    reps = n // c.shape[1]
    c, sa, sb = _lane_tile(c, reps), _lane_tile(sa, reps), _lane_tile(sb, reps)
    return a * c + pltpu.roll(a, shift, 1) * sa + pltpu.roll(a, n - shift, 1) * sb


def _masked_softmax(s, mask):
    s = jnp.where(mask, s, NEG)
    m = jnp.max(s, axis=-1, keepdims=True)
    e = jnp.where(mask, jnp.exp2(s - m), 0.0)
    d = jnp.sum(e, axis=-1, keepdims=True)
    return e * (1.0 / jnp.where(d > 0.0, d, 1.0))


def _split3(x):
    hi = x.astype(MXU_DTYPE)
    r = x - hi.astype(F32)
    mid = r.astype(MXU_DTYPE)
    lo = (r - mid.astype(F32)).astype(MXU_DTYPE)
    return hi, mid, lo


def _topk_mask(score, idx, n_pick, axis):
    big = jnp.float32(score.shape[axis])
    for _ in range(n_pick):
        m = jnp.max(score, axis=axis, keepdims=True)
        first = jnp.min(jnp.where(score == m, idx, big), axis=axis, keepdims=True)
        score = jnp.where(idx == first, REMOVED, score)
    return (score == REMOVED).astype(F32)


def _overlap(c_idx, s_idx):
    cs, ss = c_idx * CMP_STRIDE, s_idx * SEL_BLK
    ov = jnp.minimum(cs + CMP_LEN, ss + SEL_BLK) - jnp.maximum(cs, ss)
    return jnp.maximum(ov, 0).astype(F32) * (1.0 / CMP_LEN)


def _alias_specs(prev):
    return [pl.BlockSpec(memory_space=pl.ANY)] * len(prev)


def _nsa_inproj_kernel(x_ref, w_ref, c_ref, sa_ref, sb_ref, *rest):
    q_ref, kv_ref, ks_ref, kw_ref, vt_ref, gt_ref, z_ref = rest[-13:-6]
    cache_refs = rest[-6:]
    xb = x_ref[...].astype(MXU_DTYPE)
    c, sa, sb = c_ref[...], sa_ref[...], sb_ref[...]
    half = ROT_DIM // 2
    tm = x_ref.shape[0]
    token = pl.program_id(0) * tm + lax.broadcasted_iota(jnp.int32, (tm, DH), 0)
    blk_onehot = ((token // SEL_BLK) % BLOCKS_PER_TILE
                  == lax.broadcasted_iota(jnp.int32, (tm, DH), 1)).astype(F32)
    col = 0
    for ch in range(NSA_W // 512):
        a = _dot(xb, w_ref[:, col:col + 512])
        a = _rotate(a, c, sa, sb, half) * (DH ** -0.5 * LOG2E)
        for h in range(512 // DH):
            q_ref[0, ch * (512 // DH) + h] = a[:, h * DH:(h + 1) * DH].astype(q_ref.dtype)
        col += 512
    for j in range(6):
        a = _dot(xb, w_ref[:, col:col + KV_W])
        if j % 2 == 0:
            a = _rotate(a, c, sa, sb, half)
        kv_ref[j] = a
        a_t = a.T
        cache_refs[j][0, 0] = a_t
        if j == 2:
            for h in range(KVH):
                ks_ref[0, h] = jnp.concatenate([a[:, h * DH:(h + 1) * DH], blk_onehot], axis=1).astype(ks_ref.dtype)
        if j == 4:
            for h in range(KVH):
                kw_ref[0, h] = a[:, h * DH:(h + 1) * DH].astype(kw_ref.dtype)
        if j in (3, 5):
            vt_ref[j // 2 - 1, 0] = a_t.astype(vt_ref.dtype)
        col += KV_W
    a = _dot(xb, w_ref[:, col:col + GATE_PAD])
    gt_ref[0] = (1.0 / (1.0 + jnp.exp(-a))).T
    col += GATE_PAD
    for ch in range(NSA_W // 512):
        z_ref[:, ch * 512:(ch + 1) * 512] = _silu(_dot(xb, w_ref[:, col:col + 512])).astype(z_ref.dtype)
        col += 512


def _nsa_inproj(x2d, w, tabs, nb, t, tm, li, n_layers, prev):
    m, d = x2d.shape
    nt = t // tm
    n_cols = w.shape[1]
    row = lambda i: (i, 0)
    tab = pl.BlockSpec((tm, LANES), lambda i: (i % nt, 0))
    cache_spec = pl.BlockSpec((1, 1, KV_W, tm), lambda i: (li, i // nt, 0, i % nt))
    cache_shape = jax.ShapeDtypeStruct((n_layers, nb, KV_W, t), F32)
    n_in = 5
    return pl.pallas_call(
        _nsa_inproj_kernel,
        grid=(m // tm,),
        in_specs=[pl.BlockSpec((tm, d), row), pl.BlockSpec((d, n_cols), lambda i: (0, 0)), tab, tab, tab]
        + _alias_specs(prev),
        out_specs=[
            pl.BlockSpec((1, HQ, tm, DH), lambda i: (i // nt, 0, i % nt, 0)),
            pl.BlockSpec((6, tm, KV_W), lambda i: (0, i, 0)),
            pl.BlockSpec((1, KVH, tm, 2 * DH), lambda i: (i // nt, 0, i % nt, 0)),
            pl.BlockSpec((1, KVH, tm, DH), lambda i: (i // nt, 0, i % nt, 0)),
            pl.BlockSpec((2, 1, KV_W, tm), lambda i: (0, i // nt, 0, i % nt)),
            pl.BlockSpec((1, GATE_PAD, tm), lambda i: (i // nt, 0, i % nt)),
            pl.BlockSpec((tm, NSA_W), row),
        ] + [cache_spec] * 6,
        out_shape=[
            jax.ShapeDtypeStruct((nb, HQ, t, DH), MXU_DTYPE),
            jax.ShapeDtypeStruct((6, m, KV_W), F32),
            jax.ShapeDtypeStruct((nb, KVH, t, 2 * DH), MXU_DTYPE),
            jax.ShapeDtypeStruct((nb, KVH, t, DH), MXU_DTYPE),
            jax.ShapeDtypeStruct((2, nb, KV_W, t), MXU_DTYPE),
            jax.ShapeDtypeStruct((nb, GATE_PAD, t), F32),
            jax.ShapeDtypeStruct((m, NSA_W), MXU_DTYPE),
        ] + [cache_shape] * 6,
        input_output_aliases={n_in + j: 7 + j for j in range(len(prev))},
        compiler_params=_params(1),
        name="nsa_inproj",
    )(x2d, w, *tabs, *prev)


def _compress_kernel(*refs, n_prefetch, n_pages, layout, n_chunks):
    split = KV_W // LANES
    pos = n_prefetch
    if n_pages:
        page_refs = refs[pos:pos + n_pages]
        pos += n_pages
    else:
        tok_refs, halo_refs = refs[pos:pos + split], refs[pos + split:pos + 2 * split]
        pos += 2 * split
    wpair_ref, pe_ref, w1_ref, w2_ref, out_ref, r_ref = refs[pos:pos + 6]
    r_ref[:, n_chunks:, :] = jnp.zeros((split, 8, CMP_STRIDE * LANES), F32)
    if n_pages:
        tok_sc = refs[pos + 6]
        for p, ref in enumerate(page_refs):
            page = ref[0, 0].T
            for s in range(split):
                tok_sc[s, p * PAGE_SIZE:(p + 1) * PAGE_SIZE, :] = page[:, s * LANES:(s + 1) * LANES]
        for s in range(split):
            for l in range(CMP_STRIDE):
                r_ref[s, 0:n_chunks, l * LANES:(l + 1) * LANES] = tok_sc[s, pl.ds(l, n_chunks, stride=CMP_STRIDE), :]
    else:
        for s in range(split):
            for l in range(CMP_STRIDE):
                lanes = slice(l * LANES, (l + 1) * LANES)
                r_ref[s, 0:n_chunks, lanes] = tok_refs[s][0, pl.ds(l, n_chunks, stride=CMP_STRIDE), :]
                r_ref[s, n_chunks:n_chunks + 1, lanes] = halo_refs[s][0, l:l + 1, :]
    heads_per_tile = LANES // DH
    ab = [_dot(r_ref[s].astype(MXU_DTYPE), wpair_ref[...]) for s in range(split)]
    pe8 = jnp.broadcast_to(pe_ref[...], (8, CMP_LEN * DH)).astype(MXU_DTYPE)
    bias = _dot(pe8, w1_ref[...])[0:1, :]
    w2 = w2_ref[...]
    toks = []
    for h in range(KVH):
        base = (h % heads_per_tile) * 2 * CMP_HID
        first_half = ab[h // heads_per_tile][0:n_chunks, base:base + CMP_HID]
        second_half = ab[h // heads_per_tile][1:n_chunks + 1, base + CMP_HID:base + 2 * CMP_HID]
        u = first_half + second_half + bias
        hid = 0.5 * u * (1.0 + jnp.tanh(0.7978845608028654 * (u + 0.044715 * (u * u * u))))
        toks.append(_dot(hid.astype(MXU_DTYPE), w2))
    if layout == "head_major":
        for h in range(KVH):
            out_ref[0, h] = toks[h].astype(out_ref.dtype)
    elif layout == "token_major":
        out_ref[0] = jnp.concatenate(toks, axis=1).astype(out_ref.dtype)
    else:
        out_ref[0] = jnp.concatenate(toks, axis=1).T.astype(out_ref.dtype)


def _compress_weights(pe, w1, w2):
    heads = LANES // DH
    w1r = w1.reshape(2, CMP_STRIDE, DH, CMP_HID)
    eye = jnp.eye(heads, dtype=w1.dtype)
    wpair = jnp.einsum("sldn,hg->lhdgsn", w1r, eye).reshape(CMP_STRIDE * LANES, heads * 2 * CMP_HID)
    return (wpair.astype(MXU_DTYPE), pe.reshape(1, CMP_LEN * DH), w1.astype(MXU_DTYPE), w2.astype(MXU_DTYPE))


def _compress_prompt(kv_tok, group, weights, layout):
    _, nb, t, _ = kv_tok.shape
    gt = min(CMP_GROUP, t)
    n_chunks = gt // CMP_STRIDE
    ng = t // gt
    last_halo = t // CMP_STRIDE - 1
    wpair, pe, w1, w2 = weights
    split = KV_W // LANES
    k_tok = kv_tok.reshape(kv_tok.shape[0] * nb, t, KV_W)
    first = group * nb
    full = lambda a: pl.BlockSpec(a.shape, lambda b, g: (0,) * a.ndim)
    toks = [pl.BlockSpec((1, gt, LANES), lambda b, g, s=s: (first + b, g, s)) for s in range(split)]
    halos = [pl.BlockSpec((1, CMP_STRIDE, LANES),
                          lambda b, g, s=s: (first + b, jnp.minimum((g + 1) * n_chunks, last_halo), s))
             for s in range(split)]
    kern = functools.partial(_compress_kernel, n_prefetch=0, n_pages=0, layout=layout, n_chunks=n_chunks)
    if layout == "head_major":
        out_spec = pl.BlockSpec((1, KVH, n_chunks, DH), lambda b, g: (b, 0, g, 0))
        out_shape = jax.ShapeDtypeStruct((nb, KVH, t // CMP_STRIDE, DH), MXU_DTYPE)
    else:
        out_spec = pl.BlockSpec((1, KV_W, n_chunks), lambda b, g: (b, 0, g))
        out_shape = jax.ShapeDtypeStruct((nb, KV_W, t // CMP_STRIDE), MXU_DTYPE)
    return pl.pallas_call(
        kern,
        grid=(nb, ng),
        in_specs=toks + halos + [full(wpair), full(pe), full(w1), full(w2)],
        out_specs=out_spec,
        out_shape=out_shape,
        scratch_shapes=[pltpu.VMEM((split, n_chunks + 8, CMP_STRIDE * LANES), F32)],
        compiler_params=_params(2),
        name="nsa_compress_prompt",
    )(*([k_tok] * (2 * split)), wpair, pe, w1, w2)


def _compress_paged_kernel(*refs, n_pages, n_chunks):
    n_in = n_pages + 4
    ins = refs[1:1 + 2 * n_in]
    outs = refs[1 + 2 * n_in:3 + 2 * n_in]
    scratch = refs[3 + 2 * n_in:]
    for s in range(2):
        _compress_kernel(*ins[s * n_in:(s + 1) * n_in], outs[s], *scratch[2 * s:2 * s + 2],
                         n_prefetch=0, n_pages=n_pages, layout="token_major", n_chunks=n_chunks)


def _compress_paged(pool_k, pool_v, li, page_table, weights_k, weights_v):
    nb, n_pages = page_table.shape
    n_chunks = n_pages * PAGE_SIZE // CMP_STRIDE
    split = KV_W // LANES
    full = lambda a: pl.BlockSpec(a.shape, lambda b, pt: (0,) * a.ndim)
    page = lambda p: pl.BlockSpec((1, 1, KV_W, PAGE_SIZE), lambda b, pt: (li, pt[b, p], 0, 0))
    stream_specs = lambda w: [page(p) for p in range(n_pages)] + [full(a) for a in w]
    out_spec = pl.BlockSpec((1, n_chunks, KV_W), lambda b, pt: (b, 0, 0))
    out_shape = jax.ShapeDtypeStruct((nb, n_chunks, KV_W), MXU_DTYPE)
    stream_scratch = [pltpu.VMEM((split, n_chunks + 8, CMP_STRIDE * LANES), F32),
                      pltpu.VMEM((split, n_pages * PAGE_SIZE, LANES), F32)]
    return pl.pallas_call(
        functools.partial(_compress_paged_kernel, n_pages=n_pages, n_chunks=n_chunks),
        grid_spec=pltpu.PrefetchScalarGridSpec(
            num_scalar_prefetch=1,
            grid=(nb,),
            in_specs=stream_specs(weights_k) + stream_specs(weights_v),
            out_specs=[out_spec, out_spec],
            scratch_shapes=stream_scratch + stream_scratch,
        ),
        out_shape=[out_shape, out_shape],
        compiler_params=_params(1),
        name="nsa_compress_paged",
    )(page_table, *([pool_k] * n_pages), *weights_k, *([pool_v] * n_pages), *weights_v)


def _nsa_attn_prompt_kernel(q_ref, kc_ref, vct_ref, ks_ref, vst_ref, kw_ref, vwt_ref, gc_ref, gs_ref, gw_ref,
                            o_ref, ot_sc, m_sc, alpha_sc, acc_sc, p_sc, s_sc, lanes_sc, *, seq_len):
    st = pl.program_id(2) * QBLK
    ncp = kc_ref.shape[2]
    ns = seq_len // SEL_BLK
    n_pairs = GQ // 2
    qpos = st + lax.broadcasted_iota(jnp.int32, (1, QBLK), 1)
    q_pairs = [q_ref[0, 2 * j:2 * j + 2].reshape(2 * QBLK, DH) for j in range(n_pairs)]
    both = lambda b: jnp.concatenate([b, b], axis=1)

    c_row = lax.broadcasted_iota(jnp.int32, (ncp, 1), 0)
    bias_c = both(jnp.where(c_row * CMP_STRIDE + (CMP_LEN - 1) <= qpos, 0.0, NEG))
    sees_any = both(qpos >= CMP_LEN - 1)
    kc, vct = kc_ref[0, 0], vct_ref[0]
    ws = pl.multiple_of(jnp.maximum(st - WINDOW, 0), KEY_TILE)
    w_row = lax.broadcasted_iota(jnp.int32, (WIN_TILE, 1), 0)
    dist = qpos - (ws + w_row)
    bias_w = both(jnp.where((dist >= 0) & (dist <= WINDOW), 0.0, NEG))
    kw_tile = kw_ref[0, 0, pl.ds(ws, WIN_TILE), :]
    vwt_tile = jnp.concatenate([vwt_ref[0, 0, :, pl.ds(ws, WIN_TILE)], jnp.ones((ONES_ROWS, WIN_TILE), MXU_DTYPE)],
                               axis=0)
    c_scores = [_dot_nt(kc, q_pairs[j]) for j in range(n_pairs)]
    w_scores = [_dot_nt(kw_tile, q_pairs[j]) for j in range(n_pairs)]
    ks_first = ks_ref[0, 0, 0:SEL_TILE, 0:DH]
    for j in range(n_pairs):
        s_sc[j] = _dot_nt(ks_first, q_pairs[j])
    p_sum = jnp.zeros((ncp, QBLK), F32)
    c_probs, w_probs = [], []
    for j in range(n_pairs):
        s = c_scores[j] + bias_c
        e = jnp.exp2(s - jnp.max(s, axis=0, keepdims=True))
        den = jnp.sum(e, axis=0, keepdims=True)
        p = e * jnp.where(sees_any, 1.0 / den, 0.0)
        c_probs.append(p.astype(MXU_DTYPE))
        p_sum = p_sum + p[:, :QBLK] + p[:, QBLK:]
    for j in range(n_pairs):
        s = w_scores[j] + bias_w
        w_probs.append(jnp.exp2(s - jnp.max(s, axis=0, keepdims=True)).astype(MXU_DTYPE))
    o_c = [_dot(vct, p) for p in c_probs]
    o_w = [_dot(vwt_tile, p) for p in w_probs]
    o_w = [o[:DH] * (1.0 / o[DH:DH + 1]) for o in o_w]

    s_idx = lax.broadcasted_iota(jnp.int32, (ns, 1), 0)
    ov_t = _overlap(lax.broadcasted_iota(jnp.int32, (1, ncp), 1), s_idx).astype(MXU_DTYPE)
    imp_t = sum(_dot(ov_t, part) for part in _split3(p_sum))
    blk_q = qpos // SEL_BLK
    forced = (s_idx == 0) | (s_idx == blk_q) | (s_idx == blk_q - 1)
    valid = s_idx * SEL_BLK <= qpos
    score = jnp.where(valid, imp_t + FORCE_BONUS * forced.astype(F32), NEG)
    sel_t = _topk_mask(score, s_idx.astype(F32), min(N_SEL, ns), 0)
    sel_bias = jnp.where(valid & (sel_t > 0.5), 0.0, NEG)
    sel_bias_q = sel_bias.T.astype(MXU_DTYPE)

    n_tiles = (st + QBLK + SEL_TILE - 1) // SEL_TILE
    m_sc[...] = jnp.full(m_sc.shape, RUNNING_MAX_INIT, F32)
    acc_sc[...] = jnp.zeros(acc_sc.shape, F32)
    alpha_sc[...] = jnp.ones(alpha_sc.shape, F32)
    p_sc[...] = jnp.zeros(p_sc.shape, MXU_DTYPE)
    ones_rows = jnp.ones((ONES_ROWS, SEL_TILE), MXU_DTYPE)
    first_bias = both(jnp.concatenate(
        [jnp.broadcast_to(sel_bias[b:b + 1, :], (SEL_BLK, QBLK)) for b in range(BLOCKS_PER_TILE)], axis=0))
    for j in range(n_pairs):
        s_sc[j] = s_sc[j] + first_bias
    blk_row = lax.broadcasted_iota(jnp.int32, (ns, DH), 0)
    blk_lane = lax.broadcasted_iota(jnp.int32, (ns, DH), 1)

    def key_start(i):
        return pl.multiple_of(i * SEL_TILE, SEL_TILE)

    def fold(i):
        vt_tile = jnp.concatenate([vst_ref[0, 0, :, pl.ds(key_start(i), SEL_TILE)], ones_rows], axis=0)
        for j in range(n_pairs):
            acc_sc[j] = alpha_sc[j] * acc_sc[j] + _dot(vt_tile, p_sc[j])

    def bias_lanes(i):
        pick = ((blk_row - i * BLOCKS_PER_TILE == blk_lane) & (blk_lane < BLOCKS_PER_TILE)).astype(MXU_DTYPE)
        return _dot(sel_bias_q, pick).astype(MXU_DTYPE)

    def masked_queries(lanes):
        return [jnp.concatenate([jnp.concatenate([q_ref[0, 2 * j + u], lanes], axis=1) for u in range(2)], axis=0)
                for j in range(n_pairs)]

    def softmax(j, s):
        m_old = m_sc[j]
        m_new = jnp.maximum(m_old, jnp.max(s, axis=0, keepdims=True))
        p_sc[j] = jnp.exp2(s - m_new).astype(MXU_DTYPE)
        alpha_sc[j] = jnp.exp2(m_old - m_new)
        m_sc[j] = m_new

    lanes_sc[...] = bias_lanes(1)

    def step(i, carry):
        fold(jnp.maximum(i - 1, 0))
        k_ahead = ks_ref[0, 0, pl.ds(key_start(i + 1), SEL_TILE), :]
        q_ahead = masked_queries(lanes_sc[...])
        for j in range(n_pairs):
            ahead = _dot_nt(k_ahead, q_ahead[j])
            softmax(j, s_sc[j])
            s_sc[j] = ahead
        lanes_sc[...] = bias_lanes(i + 2)
        return carry

    lax.fori_loop(0, n_tiles - 1, step, 0)
    fold(jnp.maximum(n_tiles - 2, 0))
    k_row = (n_tiles - 1) * SEL_TILE + lax.broadcasted_iota(jnp.int32, (SEL_TILE, 1), 0)
    causal = both(jnp.where(k_row <= qpos, 0.0, NEG))
    for j in range(n_pairs):
        softmax(j, s_sc[j] + causal)
    fold(n_tiles - 1)
    o_s = [acc_sc[j, :DH] * (1.0 / acc_sc[j, DH:DH + 1]) for j in range(n_pairs)]
    gc, gs, gw = gc_ref[0], gs_ref[0], gw_ref[0]
    for h in range(GQ):
        j, cols = h // 2, slice((h % 2) * QBLK, (h % 2 + 1) * QBLK)
        ot_sc[h * DH:(h + 1) * DH, :] = (gc[h:h + 1] * o_c[j][:, cols] + gs[h:h + 1] * o_s[j][:, cols]
                                         + gw[h:h + 1] * o_w[j][:, cols])
    o_ref[0] = ot_sc[...].T.astype(o_ref.dtype)


def _nsa_attn_prompt(q_hm, kc_hm, vc_t, ks_aug, kw_hm, v_t, gates_t):
    nb, _, t, _ = q_hm.shape
    ncp = kc_hm.shape[2]
    k_spec = lambda width: pl.BlockSpec((1, 1, t, width), lambda b, h, i: (b, h, 0, 0))
    vt_spec = lambda j: pl.BlockSpec((1, 1, DH, t), lambda b, h, i: (j, b, h, 0))
    gate_spec = lambda br: pl.BlockSpec((1, GQ, QBLK), lambda b, h, i: (b, br * KVH + h, i))
    kern = functools.partial(_nsa_attn_prompt_kernel, seq_len=t)
    return pl.pallas_call(
        kern,
        grid=(nb, KVH, t // QBLK),
        in_specs=[pl.BlockSpec((1, GQ, QBLK, DH), lambda b, h, i: (b, h, i, 0)),
                  pl.BlockSpec((1, 1, ncp, DH), lambda b, h, i: (b, h, 0, 0)),
                  pl.BlockSpec((1, DH, ncp), lambda b, h, i: (b, h, 0)),
                  k_spec(2 * DH), vt_spec(0), k_spec(DH), vt_spec(1),
                  gate_spec(0), gate_spec(1), gate_spec(2)],
        out_specs=pl.BlockSpec((1, QBLK, GQ * DH), lambda b, h, i: (b, i, h)),
        out_shape=jax.ShapeDtypeStruct((nb, t, NSA_W), MXU_DTYPE),
        scratch_shapes=[pltpu.VMEM((GQ * DH, QBLK), F32),
                        pltpu.VMEM((GQ // 2, 1, 2 * QBLK), F32), pltpu.VMEM((GQ // 2, 1, 2 * QBLK), F32),
                        pltpu.VMEM((GQ // 2, DH + ONES_ROWS, 2 * QBLK), F32),
                        pltpu.VMEM((GQ // 2, SEL_TILE, 2 * QBLK), MXU_DTYPE),
                        pltpu.VMEM((GQ // 2, SEL_TILE, 2 * QBLK), F32),
                        pltpu.VMEM((QBLK, DH), MXU_DTYPE)],
        compiler_params=_params(3),
        name="nsa_attn_prompt",
    )(q_hm, kc_hm, vc_t, ks_aug, v_t, kw_hm, v_t, gates_t, gates_t, gates_t)


def _block_diag_fold(x):
    r_kv = lax.broadcasted_iota(jnp.int32, (HQ, KV_W), 0) // GQ
    l_kv = lax.broadcasted_iota(jnp.int32, (HQ, KV_W), 1) // DH
    x = jnp.where(r_kv == l_kv, x, 0.0)
    return sum(x[..., h * DH:(h + 1) * DH] for h in range(KVH))


def _nsa_select_kernel(q_ref, kc_ref, vc_ref, oc_ref, sel_ref, *, past):
    bb, ncp = kc_ref.shape[0], kc_ref.shape[1]
    nsp = sel_ref.shape[1]
    q = q_ref[...]
    s = jnp.einsum("bqd,bkd->bqk", q, kc_ref[...], preferred_element_type=F32)
    c_idx = lax.broadcasted_iota(jnp.int32, (1, 1, ncp), 2)
    p_c = _masked_softmax(s, c_idx * CMP_STRIDE + (CMP_LEN - 1) <= past)
    o_c = jnp.einsum("bqk,bkd->bqd", p_c.astype(MXU_DTYPE), vc_ref[...], preferred_element_type=F32)
    oc_ref[...] = _block_diag_fold(o_c)
    ov = _overlap(lax.broadcasted_iota(jnp.int32, (ncp, 1), 0),
                  lax.broadcasted_iota(jnp.int32, (1, nsp), 1)).astype(MXU_DTYPE)
    imp_rows = sum(_dot(part, ov) for part in _split3(p_c.reshape(bb * HQ, ncp)))
    imp = jnp.sum(imp_rows.reshape(bb * KVH, GQ, nsp), axis=1)
    s_idx = lax.broadcasted_iota(jnp.int32, (1, nsp), 1)
    blk_q = past // SEL_BLK
    forced = (s_idx == 0) | (s_idx == blk_q) | (s_idx == blk_q - 1)
    valid = s_idx * SEL_BLK <= past
    score = jnp.where(valid, imp + FORCE_BONUS * forced.astype(F32), NEG)
    n_blocks = past // SEL_BLK + 1
    sel = _topk_mask(score, s_idx.astype(F32), min(N_SEL, n_blocks), 1)
    sel_ref[...] = jnp.where(valid, sel, 0.0)


def _nsa_select(q_bd, kc_tm, vc_tm, past, bb):
    nb, ncp, _ = kc_tm.shape
    nsp = LANES
    assert past // SEL_BLK + 1 <= nsp
    kern = functools.partial(_nsa_select_kernel, past=past)
    blk = lambda shape: pl.BlockSpec(shape, lambda i: (i, 0, 0))
    return pl.pallas_call(
        kern,
        grid=(nb // bb,),
        in_specs=[blk((bb, HQ, KV_W)), blk((bb, ncp, KV_W)), blk((bb, ncp, KV_W))],
        out_specs=[blk((bb, HQ, DH)), pl.BlockSpec((bb * KVH, nsp), lambda i: (i, 0))],
        out_shape=[jax.ShapeDtypeStruct((nb, HQ, DH), F32), jax.ShapeDtypeStruct((nb * KVH, nsp), F32)],
        compiler_params=_params(1),
        name="nsa_select_decode",
    )(q_bd, kc_tm, vc_tm)


def _nsa_attn_decode_kernel(*refs, n_pages, n_prev):
    q_ref, oc_ref, sel_ref, new_ref, newc_ref, g_ref = refs[1:7]
    k_pages = refs[7:7 + n_pages]
    v_pages = refs[7 + n_pages:7 + 2 * n_pages]
    pos = 7 + 2 * n_pages
    kwin_ref, vwin_ref = refs[pos:pos + 2]
    o_ref, kwin_out, vwin_out = refs[pos + 2 + n_prev:pos + 5 + n_prev]
    q = q_ref[0]
    qf = q.astype(F32)
    sel4 = sel_ref[0]
    sel_rows = jnp.concatenate([jnp.broadcast_to(sel4[k:k + 1], (GQ, sel4.shape[1])) for k in range(KVH)], axis=0)
    new = new_ref[0]
    first_half = lax.broadcasted_iota(jnp.int32, (1, PAGE_SIZE), 1) < SEL_BLK

    scores, masks = [], []
    for p in range(n_pages):
        scores.append(_dot(q, k_pages[p][0, 0].astype(MXU_DTYPE)))
        masks.append(jnp.where(first_half, sel_rows[:, 2 * p:2 * p + 1], sel_rows[:, 2 * p + 1:2 * p + 2]) > 0.5)
    own = 2 * n_pages
    s_new = jnp.sum(qf * new[0:1, :].astype(MXU_DTYPE).astype(F32), axis=-1, keepdims=True)
    ok_new = sel_rows[:, own:own + 1] > 0.5
    s_new = jnp.where(ok_new, s_new, NEG)
    m = s_new
    for s, ok in zip(scores, masks):
        m = jnp.maximum(m, jnp.max(jnp.where(ok, s, NEG), axis=-1, keepdims=True))
    e_new = jnp.where(ok_new, jnp.exp2(s_new - m), 0.0)
    den = e_new
    acc = e_new * new[1:2, :].astype(MXU_DTYPE).astype(F32)
    for p, (s, ok) in enumerate(zip(scores, masks)):
        e = jnp.where(ok, jnp.exp2(s - m), 0.0)
        den = den + jnp.sum(e, axis=-1, keepdims=True)
        acc = acc + _dot_nt(e.astype(MXU_DTYPE), v_pages[p][0, 0].astype(MXU_DTYPE))
    o_s = _block_diag_fold(acc * (1.0 / den))

    kwin, vwin = kwin_ref[0, 0], vwin_ref[0, 0]
    s_w = _dot(q, kwin.astype(MXU_DTYPE))
    s_wn = jnp.sum(qf * new[2:3, :].astype(MXU_DTYPE).astype(F32), axis=-1, keepdims=True)
    m_w = jnp.maximum(jnp.max(s_w, axis=-1, keepdims=True), s_wn)
    e_w, e_wn = jnp.exp2(s_w - m_w), jnp.exp2(s_wn - m_w)
    den_w = jnp.sum(e_w, axis=-1, keepdims=True) + e_wn
    acc_w = _dot_nt(e_w.astype(MXU_DTYPE), vwin.astype(MXU_DTYPE)) + e_wn * new[3:4, :].astype(MXU_DTYPE).astype(F32)
    o_w = _block_diag_fold(acc_w * (1.0 / den_w))

    g = g_ref[0]
    o_ref[0] = g[:, 0:1] * oc_ref[0] + g[:, 1:2] * o_s + g[:, 2:3] * o_w

    wb = kwin.shape[1]
    last = lax.broadcasted_iota(jnp.int32, (1, wb), 1) == wb - 1
    kwin_out[0, 0] = jnp.where(last, newc_ref[0, 0], pltpu.roll(kwin, wb - 1, 1))
    vwin_out[0, 0] = jnp.where(last, newc_ref[0, 1], pltpu.roll(vwin, wb - 1, 1))


def _nsa_attn_decode(page_table, q_bd, o_c, sel, new_rows, new_cols, gates, pool_k, pool_v, win_k, win_v,
                     li, prev):
    nb, n_pages = page_table.shape
    n_layers, _, _, wb = win_k.shape
    nsp = sel.shape[-1]
    per_b = lambda shape: pl.BlockSpec((1,) + shape, lambda b, pt: (b,) + (0,) * len(shape))
    page = lambda p: pl.BlockSpec((1, 1, KV_W, PAGE_SIZE), lambda b, pt: (li, pt[b, p], 0, 0))
    win = pl.BlockSpec((1, 1, KV_W, wb), lambda b, pt: (li, b, 0, 0))
    kern = functools.partial(_nsa_attn_decode_kernel, n_pages=n_pages, n_prev=len(prev))
    n_in = 7 + 2 * n_pages + 2
    win_shape = jax.ShapeDtypeStruct((n_layers, nb, KV_W, wb), F32)
    return pl.pallas_call(
        kern,
        grid_spec=pltpu.PrefetchScalarGridSpec(
            num_scalar_prefetch=1,
            grid=(nb,),
            in_specs=([per_b((HQ, KV_W)), per_b((HQ, DH)), per_b((KVH, nsp)), per_b((4, KV_W)),
                       per_b((2, KV_W, 1)), per_b((HQ, 3))]
                      + [page(p) for p in range(n_pages)] + [page(p) for p in range(n_pages)]
                      + [win, win] + _alias_specs(prev)),
            out_specs=[per_b((HQ, DH)), win, win],
        ),
        out_shape=[jax.ShapeDtypeStruct((nb, HQ, DH), F32), win_shape, win_shape],
        input_output_aliases={n_in + j: 1 + j for j in range(len(prev))},
        compiler_params=_params(1),
        name="nsa_attn_decode",
    )(page_table, q_bd, o_c, sel.reshape(nb, KVH, nsp), new_rows, new_cols, gates,
      *([pool_k] * n_pages), *([pool_v] * n_pages), win_k, win_v, *prev)


def _outproj_ln_kernel(x_ref, o_ref, z_ref, w_ref, g_ref, b_ref, y_ref, *, alpha):
    a = o_ref[...].astype(F32) * z_ref[...].astype(F32)
    h = alpha * x_ref[...] + _dot(a.astype(MXU_DTYPE), w_ref[...])
    d = h - jnp.mean(h, axis=-1, keepdims=True)
    var = jnp.mean(d * d, axis=-1, keepdims=True)
    y_ref[...] = d * lax.rsqrt(var + LN_EPS) * g_ref[...] + b_ref[...]


def _outproj_ln(x2d, o2d, z2d, w, ln_g, ln_b, alpha, tm):
    m, d = x2d.shape
    wdt = o2d.shape[1]
    if m % (2 * tm) == 0:
        tm = 2 * tm
    row = lambda n: pl.BlockSpec((tm, n), lambda i: (i, 0))
    fixed = lambda shape: pl.BlockSpec(shape, lambda i: (0, 0))
    return pl.pallas_call(
        functools.partial(_outproj_ln_kernel, alpha=alpha),
        grid=(m // tm,),
        in_specs=[row(d), row(wdt), row(wdt), fixed((wdt, d)), fixed((1, d)), fixed((1, d))],
        out_specs=row(d),
        out_shape=jax.ShapeDtypeStruct((m, d), F32),
        compiler_params=_params(1),
        name="outproj_ln",
    )(x2d, o2d, z2d, w, ln_g.reshape(1, d), ln_b.reshape(1, d))


def _ret_inproj_kernel(x_ref, w_ref, c_ref, sa_ref, sb_ref, q_ref, k_ref, v_ref, z_ref, *, rdk):
    xb = x_ref[...].astype(MXU_DTYPE)
    c, sa, sb = c_ref[...], sa_ref[...], sb_ref[...]
    qk_w = RH * rdk
    col = 0
    for dst, scale in ((q_ref, 1.0), (k_ref, rdk ** -0.5)):
        for ch in range(qk_w // 512):
            a = _rotate(_dot(xb, w_ref[:, col:col + 512]), c, sa, sb, 1)
            dst[:, ch * 512:(ch + 1) * 512] = a * scale
            col += 512
    for dst, act in ((v_ref, lambda a: a), (z_ref, _silu)):
        for ch in range(2 * qk_w // 512):
            dst[:, ch * 512:(ch + 1) * 512] = act(_dot(xb, w_ref[:, col:col + 512])).astype(dst.dtype)
            col += 512


def _ret_inproj(x2d, w, tabs, t, tm):
    m, d = x2d.shape
    nt = t // tm
    rdk = d // RH
    row = lambda n: pl.BlockSpec((tm, n), lambda i: (i, 0))
    tab = pl.BlockSpec((tm, rdk), lambda i: (i % nt, 0))
    return pl.pallas_call(
        functools.partial(_ret_inproj_kernel, rdk=rdk),
        grid=(m // tm,),
        in_specs=[row(d), pl.BlockSpec(w.shape, lambda i: (0, 0)), tab, tab, tab],
        out_specs=[row(d), row(d), row(2 * d), row(2 * d)],
        out_shape=[jax.ShapeDtypeStruct((m, d), F32), jax.ShapeDtypeStruct((m, d), F32),
                   jax.ShapeDtypeStruct((m, 2 * d), F32), jax.ShapeDtypeStruct((m, 2 * d), MXU_DTYPE)],
        compiler_params=_params(1),
        name="ret_inproj",
    )(x2d, w, *tabs)


def _group_norm(o, gain):
    d = o - jnp.mean(o, axis=-1, keepdims=True)
    return d * lax.rsqrt(jnp.mean(d * d, axis=-1, keepdims=True) + LN_EPS) * gain


def _ret_prompt_kernel(q_ref, k_ref, v_ref, dm_ref, dec_ref, gn_ref, o_ref, s_ref, s_sc, *, rdk, rdv):
    ci = pl.program_id(1)

    @pl.when(ci == 0)
    def _():
        s_sc[...] = jnp.zeros(s_sc.shape, F32)

    heads = range(RH)
    q = [q_ref[0, :, h * rdk:(h + 1) * rdk] for h in heads]
    k = [k_ref[0, :, h * rdk:(h + 1) * rdk] for h in heads]
    v = [v_ref[0, :, h * rdv:(h + 1) * rdv].astype(MXU_DTYPE) for h in heads]
    dec = [dec_ref[h] for h in heads]
    s_old = [s_sc[h] for h in heads]
    inner = [_dot_nt(q[h].astype(MXU_DTYPE), k[h].astype(MXU_DTYPE)) for h in heads]
    cross = [_dot((q[h] * dec[h][:, 0:1]).astype(MXU_DTYPE), s_old[h].astype(MXU_DTYPE)) for h in heads]
    outer = [_dot_tn((k[h] * dec[h][:, 1:2]).astype(MXU_DTYPE), v[h]) for h in heads]
    intra = [_dot((inner[h] * dm_ref[h]).astype(MXU_DTYPE), v[h]) for h in heads]
    for h in heads:
        s_sc[h] = s_old[h] * dec[h][0:1, 2:3] + outer[h]
        o_ref[0, :, h * rdv:(h + 1) * rdv] = _group_norm(
            intra[h] + cross[h], gn_ref[:, h * rdv:(h + 1) * rdv]).astype(o_ref.dtype)

    @pl.when(ci == pl.num_programs(1) - 1)
    def _():
        s_ref[0] = s_sc[...]


def _ret_decode_kernel(dec_ref, q_ref, k_ref, v_ref, s0_ref, gn_ref, o_ref, s_ref, *, rdk, rdv):
    pad = LANES
    first_row = lax.broadcasted_iota(jnp.int32, (pad, 1), 0) == 0
    for r in range(q_ref.shape[0]):
        for h in range(RH):
            q = q_ref[r, :, h * rdk:(h + 1) * rdk]
            k = k_ref[r, :, h * rdk:(h + 1) * rdk]
            v = v_ref[r, :, h * rdv:(h + 1) * rdv]
            dmask, q_dec, k_dec, c_dec = dec_ref[h, 0], dec_ref[h, 1], dec_ref[h, 2], dec_ref[h, 3]
            s0 = s0_ref[0, r, h]
            k_pad = jnp.where(first_row, jnp.broadcast_to(k * k_dec, (pad, rdk)), 0.0).astype(MXU_DTYPE)
            v_pad = jnp.broadcast_to(v, (pad, rdv)).astype(MXU_DTYPE)
            s_ref[0, r, h] = s0 * c_dec + _dot_tn(k_pad, v_pad)
            qk = jnp.sum(q.astype(MXU_DTYPE).astype(F32) * k.astype(MXU_DTYPE).astype(F32), axis=-1, keepdims=True)
            inner = (qk * dmask).astype(MXU_DTYPE).astype(F32)
            q_pad = jnp.broadcast_to(q * q_dec, (8, rdk)).astype(MXU_DTYPE)
            o = inner * v.astype(MXU_DTYPE).astype(F32) + _dot(q_pad, s0.astype(MXU_DTYPE))[0:1, :]
            o_ref[r, :, h * rdv:(h + 1) * rdv] = _group_norm(o, gn_ref[:, h * rdv:(h + 1) * rdv])


def _ret_mixer_kernel(dec_s_ref, q_ref, k_ref, v_ref, dm_ref, dec_ref, gn_ref, qs_ref, ks_ref, vs_ref, s0_ref,
                      *rest, rdk, rdv):
    o_ref, sp_ref, os_ref, ss_ref, s_sc = rest[-5:]
    _ret_prompt_kernel(q_ref, k_ref, v_ref, dm_ref, dec_ref, gn_ref, o_ref, sp_ref, s_sc, rdk=rdk, rdv=rdv)
    _ret_decode_kernel(dec_s_ref, qs_ref, ks_ref, vs_ref, s0_ref, gn_ref, os_ref, ss_ref, rdk=rdk, rdv=rdv)


def _ret_mixer(q, k, v, dmask, dec, qs, ks, vs, state, li, dec_s, gn_g, prev):
    nb, t, qk_w = q.shape
    nd = qs.shape[0]
    rdk = qk_w // RH
    rdv = v.shape[2] // RH
    c = dmask.shape[1]
    nch = t // c
    assert nd % (nb * nch) == 0
    per = nd // (nb * nch)
    tok = lambda n: pl.BlockSpec((1, c, n), lambda b, i: (b, i, 0))
    full = lambda a: pl.BlockSpec(a.shape, lambda b, i: (0,) * a.ndim)
    vec = lambda n: pl.BlockSpec((per, 1, n), lambda b, i: (b * nch + i, 0, 0))
    st_spec = pl.BlockSpec((1, per, RH, rdk, rdv), lambda b, i: (li, b * nch + i, 0, 0, 0))
    gn_g = gn_g.reshape(1, RH * rdv)
    n_in = 11
    return pl.pallas_call(
        functools.partial(_ret_mixer_kernel, rdk=rdk, rdv=rdv),
        grid=(nb, nch),
        in_specs=[pl.BlockSpec(memory_space=pltpu.SMEM), tok(qk_w), tok(qk_w), tok(RH * rdv), full(dmask),
                  full(dec), full(gn_g), vec(qk_w), vec(qk_w), vec(RH * rdv), st_spec] + _alias_specs(prev),
        out_specs=[tok(RH * rdv), pl.BlockSpec((1, RH, rdk, rdv), lambda b, i: (b, 0, 0, 0)),
                   vec(RH * rdv), st_spec],
        out_shape=[jax.ShapeDtypeStruct((nb, t, RH * rdv), MXU_DTYPE),
                   jax.ShapeDtypeStruct((nb, RH, rdk, rdv), F32),
                   jax.ShapeDtypeStruct((nd, 1, RH * rdv), F32), jax.ShapeDtypeStruct(state.shape, F32)],
        input_output_aliases={n_in + j: 3 + j for j in range(len(prev))},
        scratch_shapes=[pltpu.VMEM((RH, rdk, rdv), F32)],
        compiler_params=_params(2),
        name="ret_mixer",
    )(dec_s, q, k, v, dmask, dec, gn_g, qs.reshape(nd, 1, qk_w), ks.reshape(nd, 1, qk_w),
      vs.reshape(nd, 1, RH * rdv), state, *prev)


def _rope_tables(pos):
    half = ROT_DIM // 2
    inv = ROPE_THETA ** (-jnp.arange(half, dtype=F32) / half)
    ang = pos.astype(F32)[:, None] * inv[None, :]
    lane = jnp.arange(LANES) % DH
    cos, sin = jnp.cos(ang)[:, lane % half], jnp.sin(ang)[:, lane % half]
    c = jnp.where(lane < ROT_DIM, cos, 1.0)
    sa = jnp.where((lane >= half) & (lane < ROT_DIM), sin, 0.0)
    sb = jnp.where(lane < half, -sin, 0.0)
    return c, sa, sb


def _xpos_tables(pos, rdk):
    half = rdk // 2
    inv = 1.0 / (XPOS_BASE ** jnp.linspace(0.0, 1.0, half, dtype=F32))
    ang = pos.astype(F32)[:, None] * inv[None, :]
    lane = jnp.arange(rdk)
    cos, sin = jnp.cos(ang)[:, lane // 2], jnp.sin(ang)[:, lane // 2]
    sa = jnp.where(lane % 2 == 1, sin, 0.0)
    sb = jnp.where(lane % 2 == 0, -sin, 0.0)
    return cos, sa, sb


def _decay_tables(c):
    log_g = jnp.log1p(-jnp.power(2.0, -5.0 - jnp.arange(RH, dtype=F32)))
    i = jnp.arange(c, dtype=F32)
    diff = i[:, None] - i[None, :]
    dmask = jnp.where(diff >= 0, jnp.exp(jnp.maximum(diff, 0.0)[None] * log_g[:, None, None]), 0.0)
    q_dec = jnp.exp((i + 1.0)[None, :] * log_g[:, None])
    k_dec = jnp.exp((c - 1.0 - i)[None, :] * log_g[:, None])
    c_dec = jnp.exp(c * log_g)
    return dmask, q_dec, k_dec, c_dec


def _nsa_w_in_layout(w):
    kv_end = NSA_W + 6 * KV_W
    gates = jnp.pad(w[:, kv_end:kv_end + 3 * HQ], ((0, 0), (0, GATE_PAD - 3 * HQ)))
    return jnp.concatenate([w[:, :kv_end], gates, w[:, kv_end + 3 * HQ:]], axis=1).astype(MXU_DTYPE)


def _channel_major(cache):
    return jnp.moveaxis(cache, -3, -1).reshape(cache.shape[:-3] + (KV_W, cache.shape[-3]))


def _token_major(slab):
    return jnp.moveaxis(slab.reshape(slab.shape[:-2] + (KVH, DH, slab.shape[-1])), -1, -3)


def _row_tile(m):
    for tm in (256, 128):
        if m % tm == 0:
            return tm
    raise ValueError(f"row count {m} is not a multiple of {LANES}")


def _nsa_layer(xp, xs, li, n_layers, page_table, pools, wins, w_in, w_out, cmp_k, cmp_v, ln_g, ln_b, alpha,
               prev_p, prev_s, prev_win):
    nb, t, d = xp.shape
    nd = xs.shape[0]
    past = page_table.shape[1] * PAGE_SIZE
    assert xs.shape[1] == 1 and t % SEL_TILE == 0 and t >= WIN_TILE
    w_in = _nsa_w_in_layout(w_in)
    w_out = w_out.astype(MXU_DTYPE)

    xp2 = xp.reshape(nb * t, d)
    tm = _row_tile(t)
    q_hm, kv, ks_aug, kw_hm, v_t, g_t, z, *cache_p = _nsa_inproj(xp2, w_in, _rope_tables(jnp.arange(t)), nb, t, tm,
                                                        li, n_layers, prev_p)
    kv = kv.reshape(6, nb, t, KV_W)
    kc_hm = _compress_prompt(kv, 0, cmp_k, "head_major")
    vc_t = _compress_prompt(kv, 1, cmp_v, "channel_major")
    o = _nsa_attn_prompt(q_hm, kc_hm, vc_t, ks_aug, kw_hm, v_t, g_t)
    yp = _outproj_ln(xp2, o.reshape(nb * t, NSA_W), z, w_out, ln_g, ln_b, alpha, tm).reshape(nb, t, d)

    xs2 = xs.reshape(nd, d)
    tms = _row_tile(nd)
    pos_s = jnp.full((nd,), past, jnp.int32)
    q_s, kv_s, _, _, _, g_s, z_s, *cache_s = _nsa_inproj(xs2, w_in, _rope_tables(pos_s), 1, nd, tms,
                                                      li, n_layers, prev_s)
    q_tok = q_s[0].transpose(1, 0, 2).astype(F32).reshape(nd, KVH, GQ, 1, DH)
    q_bd = (q_tok * jnp.eye(KVH, dtype=F32)[None, :, None, :, None]).reshape(nd, HQ, KV_W).astype(MXU_DTYPE)
    kc_tm, vc_tm = _compress_paged(pools[0], pools[1], li, page_table, cmp_k, cmp_v)
    o_c, sel = _nsa_select(q_bd, kc_tm, vc_tm, past, min(16, nd))
    new_rows = jnp.stack([kv_s[2], kv_s[3], kv_s[4], kv_s[5]], axis=1)
    new_cols = jnp.stack([kv_s[4], kv_s[5]], axis=1)[..., None]
    gates_s = g_s[0, :3 * HQ].reshape(3, HQ, nd).transpose(2, 1, 0)
    o_s, *win_new = _nsa_attn_decode(page_table, q_bd, o_c, sel, new_rows, new_cols, gates_s,
                                     pools[2], pools[3], wins[0], wins[1], li, prev_win)
    ys = _outproj_ln(xs2, o_s.reshape(nd, NSA_W), z_s, w_out, ln_g, ln_b, alpha, tms).reshape(nd, 1, d)
    return yp, ys, cache_p, cache_s, win_new


def _ret_layer(xp, xs, li, state, past, w_in, gn_g, w_out, ln_g, ln_b, alpha, prev_state):
    nb, t, d = xp.shape
    nd = xs.shape[0]
    rdk = d // RH
    assert xs.shape[1] == 1 and t % RET_CHUNK == 0
    w_in = w_in.astype(MXU_DTYPE)
    w_out = w_out.astype(MXU_DTYPE)

    xp2 = xp.reshape(nb * t, d)
    tm = _row_tile(t)
    q, k, v, z = _ret_inproj(xp2, w_in, _xpos_tables(jnp.arange(t), rdk), t, tm)
    dmask, q_dec, k_dec, c_dec = _decay_tables(RET_CHUNK)
    dec = jnp.stack([q_dec, k_dec, jnp.broadcast_to(c_dec[:, None], q_dec.shape)]
                    + [jnp.zeros_like(q_dec)] * 5, axis=-1)
    xs2 = xs.reshape(nd, d)
    tms = _row_tile(nd)
    q_s, k_s, v_s, z_s = _ret_inproj(xs2, w_in, _xpos_tables(jnp.full((nd,), past, jnp.int32), rdk), nd, tms)
    dmask_s, q_dec, k_dec, c_dec = _decay_tables(1)
    dec_s = jnp.stack([dmask_s[:, 0, 0], q_dec[:, 0], k_dec[:, 0], c_dec], axis=-1)

    o, sp, o_s, ss = _ret_mixer(q.reshape(nb, t, d), k.reshape(nb, t, d), v.reshape(nb, t, 2 * d), dmask, dec,
                                q_s, k_s, v_s, state, li, dec_s, gn_g, prev_state)
    yp = _outproj_ln(xp2, o.reshape(nb * t, 2 * d), z, w_out, ln_g, ln_b, alpha, tm).reshape(nb, t, d)
    ys = _outproj_ln(xs2, o_s.reshape(nd, 2 * d), z_s, w_out, ln_g, ln_b, alpha, tms).reshape(nd, 1, d)
    return yp, ys, sp, ss


def kernel(x_prompt, x_sample, cache_k_cmp, cache_v_cmp, cache_k_sel, cache_v_sel, cache_k_win, cache_v_win,
           state_ret, page_table, nsa_w_in, nsa_w_out, nsa_pe_k, nsa_w1_k, nsa_w2_k, nsa_pe_v, nsa_w1_v,
           nsa_w2_v, ret_w_in, ret_gn_g, ret_w_out, ln_g, ln_b):
    depth = ln_g.shape[0]
    alpha = (2.0 * depth) ** 0.25
    n_nsa = nsa_w_in.shape[0]
    past = page_table.shape[1] * PAGE_SIZE
    t = x_prompt.shape[1]
    pools = [_channel_major(c) for c in (cache_k_cmp, cache_v_cmp, cache_k_sel, cache_v_sel)]
    wins = [_channel_major(c) for c in (cache_k_win, cache_v_win)]
    xp, xs = x_prompt, x_sample
    cache_p, cache_s, win_new, ret_p, ret_s = [], [], [], [], []
    for i in range(depth):
        li = i // 2
        if i % 2 == 0:
            cmp_k = _compress_weights(nsa_pe_k[li], nsa_w1_k[li], nsa_w2_k[li])
            cmp_v = _compress_weights(nsa_pe_v[li], nsa_w1_v[li], nsa_w2_v[li])
            xp, xs, cache_p, cache_s, win_new = _nsa_layer(
                xp, xs, li, n_nsa, page_table, pools, wins, nsa_w_in[li], nsa_w_out[li], cmp_k, cmp_v,
                ln_g[i], ln_b[i], alpha, cache_p, cache_s, win_new)
        else:
            xp, xs, sp, ss = _ret_layer(xp, xs, li, state_ret, past, ret_w_in[li], ret_gn_g[li], ret_w_out[li],
                                        ln_g[i], ln_b[i], alpha, ret_s)
            ret_p.append(sp)
            ret_s = [ss]
    wb = min(WINDOW, t)
    out_p = [_token_major(c) for c in cache_p[:4]] + [_token_major(c[..., t - wb:]) for c in cache_p[4:]]
    out_s = [jnp.moveaxis(_token_major(c), 2, 1) for c in cache_s[:4]] + [_token_major(w) for w in win_new]
    return (xp, xs,
            out_p[0], out_s[0], out_p[1], out_s[1], out_p[2], out_s[2], out_p[3], out_s[3],
            out_p[4], out_s[4], out_p[5], out_s[5],
            jnp.stack(ret_p), ret_s[0])
```
